```python
import jax, jax.numpy as jnp
from jax import lax
import numpy as np

D_MODEL = 1024
BATCH = 8
SEQ = 4096
DEPTH = 2

F32 = jnp.float32
GROUP_WIDTH = D_MODEL // 4
HEAD_DIM = 64
N_HEADS = GROUP_WIDTH // HEAD_DIM
CHUNK = 128
CONV_WIDTH = 4
RWKV_DECAY_LORA = 64
RWKV_AAA_LORA = 64
RWKV_GATE_LORA = 128
RWKV_LN_EPS = 64e-5
S5_GROUP = 16
S5_GROUPS = GROUP_WIDTH // S5_GROUP
S5_STATE = 64
D_FF = 4 * D_MODEL
ROPE_BASE = 10000.0
RMS_EPS = 1e-6
HEAD_LN_EPS = 1e-5

M_COLS = 4 * GROUP_WIDTH + 2 * N_HEADS
R_COLS = 3 * GROUP_WIDTH + RWKV_DECAY_LORA + RWKV_AAA_LORA + RWKV_GATE_LORA
T_COLS = 4 * GROUP_WIDTH
S_COLS = GROUP_WIDTH
IN_COLS = M_COLS + R_COLS + T_COLS + S_COLS

kernel_name = "hybrid_parallel_mlstm_rwkv7_retnet_s5"


def rms_norm(x, g):
    xf = x.astype(F32)
    y = xf * lax.rsqrt(jnp.mean(xf * xf, axis=-1, keepdims=True) + RMS_EPS)
    return (y * g.astype(F32)).astype(x.dtype)


def head_layer_norm(h, g, b=None, eps=HEAD_LN_EPS):
    mu = jnp.mean(h, axis=-1, keepdims=True)
    var = jnp.mean(jnp.square(h - mu), axis=-1, keepdims=True)
    y = ((h - mu) * lax.rsqrt(var + eps)).reshape(h.shape[0], h.shape[1], -1) * g
    if b is not None:
        y = y + b
    return y


def to_chunks(t):
    b, s, h, d = t.shape
    return t.reshape(b, s // CHUNK, CHUNK, h, d).transpose(1, 0, 3, 2, 4)


def from_chunks(t):
    nc, b, h, l, d = t.shape
    return t.transpose(1, 0, 3, 2, 4).reshape(b, nc * l, h, d)


def gate_chunks(t):
    b, s, h = t.shape
    return t.reshape(b, s // CHUNK, CHUNK, h).transpose(1, 0, 3, 2)


def token_shift(t):
    return jnp.pad(t, ((0, 0), (1, 0), (0, 0)))[:, :-1]


def causal_conv(t, w, b):
    s = t.shape[1]
    tp = jnp.pad(t, ((0, 0), (CONV_WIDTH - 1, 0), (0, 0)))
    y = b
    for j in range(CONV_WIDTH):
        y = y + tp[:, j:j + s] * w[j]
    return y


def rotary(t):
    s = t.shape[1]
    half = HEAD_DIM // 2
    inv = ROPE_BASE ** (-jnp.arange(half, dtype=F32) / half)
    ang = jnp.arange(s, dtype=F32)[:, None] * inv[None, :]
    cos = jnp.cos(ang)[None, :, None, :]
    sin = jnp.sin(ang)[None, :, None, :]
    t1, t2 = t[..., :half], t[..., half:]
    return jnp.concatenate([t1 * cos - t2 * sin, t1 * sin + t2 * cos], axis=-1)


def mlstm_mixer(pm, conv_w, conv_b, i_bias, f_bias, norm_g):
    pm = pm.astype(F32)
    bsz, s, _ = pm.shape
    W, H, Dh = GROUP_WIDTH, N_HEADS, HEAD_DIM
    qk = jax.nn.silu(causal_conv(pm[..., :2 * W], conv_w, conv_b))
    q = to_chunks(qk[..., :W].reshape(bsz, s, H, Dh))
    k = to_chunks(qk[..., W:].reshape(bsz, s, H, Dh)) * (Dh ** -0.5)
    v = to_chunks(pm[..., 2 * W:3 * W].reshape(bsz, s, H, Dh))
    o_gate = jax.nn.sigmoid(pm[..., 3 * W:4 * W])
    log_i = gate_chunks(pm[..., 4 * W:4 * W + H] + i_bias)
    log_f = jax.nn.log_sigmoid(gate_chunks(pm[..., 4 * W + H:] + f_bias))
    g_cum = jnp.cumsum(log_f, axis=-1)
    causal = jnp.tril(jnp.ones((CHUNK, CHUNK), dtype=bool))

    def step(carry, xs):
        C, n, m = carry
        qc, kc, vc, ic, gc = xs
        G = gc[..., -1]
        d_intra = jnp.where(causal, gc[..., :, None] - gc[..., None, :] + ic[..., None, :], -jnp.inf)
        d_inter = gc + m[..., None]
        m_row = jnp.maximum(d_inter, jnp.max(d_intra, axis=-1))
        s_qk = jnp.einsum('bhjd,bhsd->bhjs', qc, kc) * jnp.exp(d_intra - m_row[..., None])
        w_inter = jnp.exp(d_inter - m_row)
        num = jnp.einsum('bhjs,bhsd->bhjd', s_qk, vc) + w_inter[..., None] * jnp.einsum('bhjk,bhkv->bhjv', qc, C)
        den = jnp.sum(s_qk, axis=-1) + w_inter * jnp.einsum('bhjk,bhk->bhj', qc, n)
        h = num / jnp.maximum(jnp.abs(den), jnp.exp(-m_row))[..., None]
        d_state = G[..., None] - gc + ic
        m_new = jnp.maximum(G + m, jnp.max(d_state, axis=-1))
        w_state = jnp.exp(d_state - m_new[..., None])
        carry_scale = jnp.exp(G + m - m_new)
        C = carry_scale[..., None, None] * C + jnp.einsum('bhs,bhsk,bhsv->bhkv', w_state, kc, vc)
        n = carry_scale[..., None] * n + jnp.einsum('bhs,bhsk->bhk', w_state, kc)
        return (C, n, m_new), h

    init = (jnp.zeros((bsz, H, Dh, Dh), F32), jnp.zeros((bsz, H, Dh), F32), jnp.zeros((bsz, H), F32))
    _, h = lax.scan(step, init, (q, k, v, log_i, g_cum))
    return head_layer_norm(from_chunks(h), norm_g) * o_gate


def rwkv7_mixer(pr, mu, w0, w_up, a0, a_up, g_up, k_k, k_a, r_k, ln_g, ln_b):
    pr = pr.astype(F32)
    bsz, s, _ = pr.shape
    W, H, N = GROUP_WIDTH, N_HEADS, HEAD_DIM
    pr = pr + mu * (token_shift(pr) - pr)
    r = pr[..., :W]
    k = pr[..., W:2 * W]
    v = pr[..., 2 * W:3 * W]
    c0 = 3 * W
    c1 = c0 + RWKV_DECAY_LORA
    c2 = c1 + RWKV_AAA_LORA
    w_log = -jax.nn.softplus(-(w0 + jnp.tanh(pr[..., c0:c1]) @ w_up)) - 0.5
    decay = jnp.exp(-jnp.exp(w_log))
    a = jax.nn.sigmoid(a0 + pr[..., c1:c2] @ a_up)
    g = jax.nn.sigmoid(pr[..., c2:]) @ g_up
    heads = lambda t: t.reshape(bsz, s, H, N)
    kk = heads(k * k_k)
    kk = kk / jnp.maximum(jnp.linalg.norm(kk, axis=-1, keepdims=True), 1e-12)
    k = k * (1.0 + (a - 1.0) * k_a)
    r, decay, k, v, a = heads(r), heads(decay), heads(k), heads(v), heads(a)
    tm = lambda t: jnp.swapaxes(t, 0, 1)

    def step(state, xs):
        r_t, w_t, k_t, v_t, kk_t, a_t = xs
        sa = jnp.einsum('bhvk,bhk->bhv', state, -kk_t)
        state = (state * w_t[:, :, None, :]
                 + jnp.einsum('bhv,bhk->bhvk', sa, kk_t * a_t)
                 + jnp.einsum('bhv,bhk->bhvk', v_t, k_t))
        return state, jnp.einsum('bhvk,bhk->bhv', state, r_t)

    _, y = lax.scan(step, jnp.zeros((bsz, H, N, N), F32),
                    (tm(r), tm(decay), tm(k), tm(v), tm(kk), tm(a)))
    y = head_layer_norm(tm(y), ln_g, ln_b, RWKV_LN_EPS)
    bonus = jnp.sum(r * k * r_k, axis=-1, keepdims=True) * v
    return (y + bonus.reshape(bsz, s, W)) * g


def retention_mixer(pt, gn_g, gn_b):
    pt = pt.astype(F32)
    bsz, s, _ = pt.shape
    W, H, Dh = GROUP_WIDTH, N_HEADS, HEAD_DIM
    q = rotary(pt[..., :W].reshape(bsz, s, H, Dh))
    k = rotary(pt[..., W:2 * W].reshape(bsz, s, H, Dh)) * (Dh ** -0.5)
    v = pt[..., 2 * W:3 * W].reshape(bsz, s, H, Dh)
    gate = pt[..., 3 * W:]
    log_gamma = jnp.log(1.0 - 2.0 ** (-5.0 - jnp.arange(H, dtype=F32)))
    idx = jnp.arange(CHUNK, dtype=F32)
    diff = idx[:, None] - idx[None, :]
    causal = diff >= 0
    decay_mat = jnp.where(causal, jnp.exp(jnp.where(causal, diff, 0.0) * log_gamma[:, None, None]), 0.0)
    q_decay = jnp.exp((idx + 1.0) * log_gamma[:, None])
    k_decay = jnp.exp((CHUNK - 1.0 - idx) * log_gamma[:, None])
    chunk_decay = jnp.exp(CHUNK * log_gamma)

    def step(R, xs):
        qc, kc, vc = xs
        sc = jnp.einsum('bhld,bhmd->bhlm', qc, kc) * decay_mat
        o = jnp.einsum('bhlm,bhmd->bhld', sc, vc) + jnp.einsum('bhlk,bhkv->bhlv', qc, R) * q_decay[..., None]
        R = R * chunk_decay[:, None, None] + jnp.einsum('bhmk,bhmv->bhkv', kc * k_decay[..., None], vc)
        return R, o

    _, o = lax.scan(step, jnp.zeros((bsz, H, Dh, Dh), F32), (to_chunks(q), to_chunks(k), to_chunks(v)))
    return jax.nn.silu(gate) * head_layer_norm(from_chunks(o), gn_g, gn_b)


def complex_affine_combine(e1, e2):
    a1r, a1i, b1r, b1i = e1
    a2r, a2i, b2r, b2i = e2
    return (a2r * a1r - a2i * a1i, a2r * a1i + a2i * a1r,
            a2r * b1r - a2i * b1i + b2r, a2r * b1i + a2i * b1r + b2i)


def s5_mixer(u, lam_re, lam_im, log_dt, b_re, b_im, c_re, c_im, d_skip, w_glu, b_glu):
    uf = u.astype(F32)
    bsz, s, _ = uf.shape
    W = GROUP_WIDTH
    lam_re, lam_im = lam_re.astype(F32), lam_im.astype(F32)
    b_re, b_im, c_re, c_im = b_re.astype(F32), b_im.astype(F32), c_re.astype(F32), c_im.astype(F32)
    dt = jnp.exp(log_dt.astype(F32))[:, None]
    mag = jnp.exp(lam_re * dt)
    abar_re, abar_im = mag * jnp.cos(lam_im * dt), mag * jnp.sin(lam_im * dt)
    den = lam_re * lam_re + lam_im * lam_im
    num_re, num_im = abar_re - 1.0, abar_im
    f_re = (num_re * lam_re + num_im * lam_im) / den
    f_im = (num_im * lam_re - num_re * lam_im) / den
    bbar_re = f_re[..., None] * b_re - f_im[..., None] * b_im
    bbar_im = f_re[..., None] * b_im + f_im[..., None] * b_re
    ug = uf.reshape(bsz, s, S5_GROUPS, S5_GROUP)
    bu_re = jnp.einsum('bsgc,gpc->bsgp', ug, bbar_re)
    bu_im = jnp.einsum('bsgc,gpc->bsgp', ug, bbar_im)
    a_re_full = jnp.broadcast_to(abar_re, bu_re.shape)
    a_im_full = jnp.broadcast_to(abar_im, bu_re.shape)
    _, _, x_re, x_im = lax.associative_scan(complex_affine_combine, (a_re_full, a_im_full, bu_re, bu_im), axis=1)
    y = jnp.einsum('bsgp,gcp->bsgc', x_re, c_re) - jnp.einsum('bsgp,gcp->bsgc', x_im, c_im)
    y = jax.nn.gelu(y.reshape(bsz, s, W) + d_skip * uf)
    z = y @ w_glu + b_glu
    return z[..., :W] * jax.nn.sigmoid(z[..., W:])


def setup_inputs(seed: int = 0) -> dict:
    key = jax.random.key(seed)
    ks = iter(jax.random.split(key, 48))
    L, W, H, G, P = DEPTH, GROUP_WIDTH, N_HEADS, S5_GROUPS, S5_STATE
    nrm = lambda shape, scale: scale * jax.random.normal(next(ks), shape, F32)
    unif = lambda shape, lo, hi: jax.random.uniform(next(ks), shape, F32, lo, hi)
    return {
        'x': nrm((BATCH, SEQ, D_MODEL), 1.0),
        'norm_mix_pre': 1.0 + nrm((L, D_MODEL), 0.05),
        'norm_mix_post': 1.0 + nrm((L, D_MODEL), 0.05),
        'w_in': nrm((L, D_MODEL, IN_COLS), D_MODEL ** -0.5),
        'm_conv_w': nrm((L, CONV_WIDTH, 2 * W), CONV_WIDTH ** -0.5),
        'm_conv_b': nrm((L, 2 * W), 0.02),
        'm_i_bias': nrm((L, H), 0.1),
        'm_f_bias': jnp.broadcast_to(jnp.linspace(3.0, 6.0, H, dtype=F32), (L, H)) + nrm((L, H), 0.1),
        'm_norm': 1.0 + nrm((L, W), 0.05),
        'r_mu': unif((L, R_COLS), 0.0, 1.0),
        'r_w0': unif((L, W), -6.0, 1.0),
        'r_w_up': nrm((L, RWKV_DECAY_LORA, W), 0.1),
        'r_a0': nrm((L, W), 0.1),
        'r_a_up': nrm((L, RWKV_AAA_LORA, W), 0.1),
        'r_g_up': nrm((L, RWKV_GATE_LORA, W), RWKV_GATE_LORA ** -0.5),
        'r_k_k': 0.85 + nrm((L, W), 0.05),
        'r_k_a': 1.0 + nrm((L, W), 0.05),
        'r_r_k': nrm((L, H, HEAD_DIM), 0.1),
        'r_ln_g': 1.0 + nrm((L, W), 0.05),
        'r_ln_b': nrm((L, W), 0.02),
        't_gn_g': 1.0 + nrm((L, W), 0.05),
        't_gn_b': nrm((L, W), 0.02),
        's_lam_re': -0.5 + nrm((L, G, P), 0.01),
        's_lam_im': jnp.broadcast_to(jnp.pi * jnp.arange(P, dtype=F32), (L, G, P)) + nrm((L, G, P), 0.01),
        's_log_dt': unif((L, G), float(np.log(1e-3)), float(np.log(1e-1))),
        's_b_re': nrm((L, G, P, S5_GROUP), (2 * S5_GROUP) ** -0.5),
        's_b_im': nrm((L, G, P, S5_GROUP), (2 * S5_GROUP) ** -0.5),
        's_c_re': nrm((L, G, S5_GROUP, P), P ** -0.5),
        's_c_im': nrm((L, G, S5_GROUP, P), P ** -0.5),
        's_d': nrm((L, W), 0.5),
        's_w_glu': nrm((L, W, 2 * W), W ** -0.5),
        's_b_glu': nrm((L, 2 * W), 0.02),
        'w_out': nrm((L, D_MODEL, D_MODEL), D_MODEL ** -0.5),
        'norm_ffn_pre': 1.0 + nrm((L, D_MODEL), 0.05),
        'norm_ffn_post': 1.0 + nrm((L, D_MODEL), 0.05),
        'w_ff1': nrm((L, D_MODEL, D_FF), D_MODEL ** -0.5),
        'w_ff2': nrm((L, D_FF, D_MODEL), D_FF ** -0.5),
    }


def reference(x, norm_mix_pre, norm_mix_post, w_in, m_conv_w, m_conv_b, m_i_bias, m_f_bias, m_norm,
              r_mu, r_w0, r_w_up, r_a0, r_a_up, r_g_up, r_k_k, r_k_a, r_r_k, r_ln_g, r_ln_b,
              t_gn_g, t_gn_b, s_lam_re, s_lam_im, s_log_dt, s_b_re, s_b_im, s_c_re, s_c_im, s_d,
              s_w_glu, s_b_glu, w_out, norm_ffn_pre, norm_ffn_post, w_ff1, w_ff2):
    dtype = x.dtype
    o_r = M_COLS
    o_t = o_r + R_COLS
    o_s = o_t + T_COLS
    for l in range(DEPTH):
        h = rms_norm(x, norm_mix_pre[l])
        p = h @ w_in[l]
        y_m = mlstm_mixer(p[..., :o_r], m_conv_w[l], m_conv_b[l], m_i_bias[l], m_f_bias[l], m_norm[l])
        y_r = rwkv7_mixer(p[..., o_r:o_t], r_mu[l], r_w0[l], r_w_up[l], r_a0[l], r_a_up[l], r_g_up[l],
                          r_k_k[l], r_k_a[l], r_r_k[l], r_ln_g[l], r_ln_b[l])
        y_t = retention_mixer(p[..., o_t:o_s], t_gn_g[l], t_gn_b[l])
        y_s = s5_mixer(p[..., o_s:], s_lam_re[l], s_lam_im[l], s_log_dt[l], s_b_re[l], s_b_im[l],
                       s_c_re[l], s_c_im[l], s_d[l], s_w_glu[l], s_b_glu[l])
        y = jnp.concatenate([y_m, y_r, y_t, y_s], axis=-1).astype(dtype) @ w_out[l]
        x = x + rms_norm(y, norm_mix_post[l]).astype(dtype)
        h = rms_norm(x, norm_ffn_pre[l])
        f = jnp.square(jax.nn.relu(h @ w_ff1[l])) @ w_ff2[l]
        x = x + rms_norm(f, norm_ffn_post[l]).astype(dtype)
    return x
```

```python
import functools

import jax
import jax.numpy as jnp
from jax import lax
from jax.experimental import pallas as pl
from jax.experimental.pallas import tpu as pltpu

F32 = jnp.float32
BF16 = jnp.bfloat16
HI = lax.Precision.HIGHEST

D_MODEL = 1024
GW = 256
NH = 4
DH = 64
HEAD_SHIFT = 6
D_FF = 4 * D_MODEL
S5_GROUPS = 16
S5_GROUP = 16
S5_STATE = 64
S5_N = S5_GROUPS * S5_STATE
CONV_WIDTH = 4
ROPE_BASE = 10000.0
RMS_EPS = 1e-6
HEAD_LN_EPS = 1e-5
RWKV_LN_EPS = 64e-5

SUBLANES = 8
LANES = 128
M_CHUNK = 128
R_CHUNK = 64
S5_T = 128
TM_PROJ = 512
VMEM_LIMIT = 56 * 1024 * 1024


def _cparams(sem):
    return pltpu.CompilerParams(dimension_semantics=sem, vmem_limit_bytes=VMEM_LIMIT)


def _dot(a, b, precision=None):
    return jnp.dot(a, b, preferred_element_type=F32, precision=precision)


def _dot_nt(a, b, precision=None):
    return lax.dot_general(a, b, (((1,), (1,)), ((), ())), preferred_element_type=F32, precision=precision)


def _dot_tn(a, b, precision=None):
    return lax.dot_general(a, b, (((0,), (0,)), ((), ())), preferred_element_type=F32, precision=precision)


def _sigmoid(x):
    return 1.0 / (1.0 + jnp.exp(-x))


def _softplus(x):
    return jnp.maximum(x, 0.0) + jnp.log(1.0 + jnp.exp(-jnp.abs(x)))


def _head_masks():
    lane = lax.broadcasted_iota(jnp.int32, (1, GW), 1)
    return [jnp.where((lane >> HEAD_SHIFT) == h, 1.0, 0.0).astype(F32) for h in range(NH)]


def _same_head_matrix(scale):
    r = lax.broadcasted_iota(jnp.int32, (GW, GW), 0)
    c = lax.broadcasted_iota(jnp.int32, (GW, GW), 1)
    return jnp.where((r >> HEAD_SHIFT) == (c >> HEAD_SHIFT), scale, 0.0).astype(F32)


def _head_norm(x, eps):
    avg = _same_head_matrix(1.0 / DH)
    mu = _dot(x, avg, HI)
    xc = x - mu
    var = _dot(xc * xc, avg, HI)
    return xc * lax.rsqrt(var + eps)


def _rms(x, g):
    return x * lax.rsqrt(jnp.mean(x * x, axis=-1, keepdims=True) + RMS_EPS) * g


def _inproj_kernel(x_ref, g_ref, wm_ref, wg_ref, wr_ref, wt_ref, ws_ref,
                   pm_ref, pg_ref, pr_ref, pt_ref, ps_ref):
    h = _rms(x_ref[...], g_ref[...]).astype(BF16)
    pm_ref[...] = _dot(h, wm_ref[...])
    pg_ref[...] = _dot(h, wg_ref[...])
    pr_ref[...] = _dot(h, wr_ref[...])
    pt_ref[...] = _dot(h, wt_ref[...])
    ps_ref[...] = _dot(h, ws_ref[...])


def _inproj(x2, g, wm, wg, wr, wt, ws, bsz, seq):
    n = bsz * seq
    tm = TM_PROJ
    nt = seq // tm
    row = lambda b, i: (b * nt + i, 0)
    const = lambda b, i: (0, 0)
    wspec = lambda w: pl.BlockSpec(w.shape, const, pipeline_mode=pl.Buffered(1))
    return pl.pallas_call(
        _inproj_kernel,
        grid=(bsz, nt),
        in_specs=[pl.BlockSpec((tm, D_MODEL), row), pl.BlockSpec((1, D_MODEL), const),
                  wspec(wm), wspec(wg), wspec(wr), wspec(wt), wspec(ws)],
        out_specs=[pl.BlockSpec((tm, 4 * GW), row), pl.BlockSpec((tm, LANES), row),
                   pl.BlockSpec((tm, 4 * GW), row), pl.BlockSpec((tm, 4 * GW), row),
                   pl.BlockSpec((tm, GW), lambda b, i: (i, b))],
        out_shape=[jax.ShapeDtypeStruct((n, 4 * GW), F32), jax.ShapeDtypeStruct((n, LANES), F32),
                   jax.ShapeDtypeStruct((n, 4 * GW), F32), jax.ShapeDtypeStruct((n, 4 * GW), F32),
                   jax.ShapeDtypeStruct((seq, bsz * GW), F32)],
        compiler_params=_cparams(("parallel", "parallel")),
        name="inproj",
    )(x2, g, wm, wg, wr, wt, ws)


def _mlstm_kernel(pm_ref, pg_ref, cw_ref, cb_ref, gb_ref, ng_ref, y_ref, xs_ref, c_ref, n_ref, m_ref):
    L = M_CHUNK

    @pl.when(pl.program_id(1) == 0)
    def _():
        xs_ref[0:SUBLANES, :] = jnp.zeros((SUBLANES, 2 * GW), F32)
        c_ref[...] = jnp.zeros_like(c_ref)
        n_ref[...] = jnp.zeros_like(n_ref)
        m_ref[...] = jnp.zeros_like(m_ref)

    xqk = pm_ref[:, 0:2 * GW]
    xs_ref[SUBLANES:SUBLANES + L, :] = xqk
    conv = cb_ref[...]
    for j in range(CONV_WIDTH):
        conv = conv + xs_ref[pl.ds(SUBLANES - (CONV_WIDTH - 1) + j, L), :] * cw_ref[j:j + 1, :]
    xs_ref[0:SUBLANES, :] = xqk[L - SUBLANES:L, :]
    qk = conv * _sigmoid(conv)
    q = qk[:, 0:GW]
    k = qk[:, GW:2 * GW] * (DH ** -0.5)
    v = pm_ref[:, 2 * GW:3 * GW]
    og = _sigmoid(pm_ref[:, 3 * GW:4 * GW])

    lane = lax.broadcasted_iota(jnp.int32, (1, LANES), 1)
    g2 = pg_ref[...] + gb_ref[...]
    lf = jnp.where((lane >= NH) & (lane < 2 * NH), -_softplus(-g2), 0.0)
    ri = lax.broadcasted_iota(jnp.int32, (L, L), 0)
    ci = lax.broadcasted_iota(jnp.int32, (L, L), 1)
    causal = ri >= ci
    gcum = _dot(jnp.where(causal, 1.0, 0.0).astype(F32), lf, HI)
    colmat = jnp.where(lane < NH, g2, gcum)
    rowmat = colmat.T

    masks = _head_masks()
    c_old = c_ref[...]
    n_old = n_ref[...]
    m_old = m_ref[...]
    q_c = _dot(q.astype(BF16), c_old.astype(BF16))
    kb = k.astype(BF16)
    vb = v.astype(BF16)

    num = jnp.zeros((L, GW), F32)
    den = jnp.zeros((L, GW), F32)
    wst = jnp.zeros((L, GW), F32)
    cs_l = jnp.zeros((1, GW), F32)
    mn_l = jnp.zeros((1, GW), F32)
    for h in range(NH):
        mh = masks[h]
        gcol = colmat[:, NH + h:NH + h + 1]
        icol = colmat[:, h:h + 1]
        grow = rowmat[NH + h:NH + h + 1, :]
        irow = rowmat[h:h + 1, :]
        mprev = m_old[:, h * DH:h * DH + 1]
        d_intra = jnp.where(causal, gcol - grow + irow, -jnp.inf)
        d_inter = gcol + mprev
        m_row = jnp.maximum(d_inter, jnp.max(d_intra, axis=-1, keepdims=True))
        sc = _dot_nt((q * mh).astype(BF16), kb)
        s_qk = sc * jnp.exp(d_intra - m_row)
        w_inter = jnp.exp(d_inter - m_row)
        num = num + mh * (_dot(s_qk.astype(BF16), vb) + w_inter * q_c)
        den_h = (jnp.sum(s_qk, axis=-1, keepdims=True)
                 + w_inter * jnp.sum(q * (mh * n_old), axis=-1, keepdims=True))
        den = den + mh * jnp.maximum(jnp.abs(den_h), jnp.exp(-m_row))
        g_last = gcum[L - 1:L, NH + h:NH + h + 1]
        d_state = g_last - gcol + icol
        m_new = jnp.maximum(g_last + mprev, jnp.max(d_state, axis=0, keepdims=True))
        wst = wst + mh * jnp.exp(d_state - m_new)
        cs_l = cs_l + mh * jnp.exp(g_last + mprev - m_new)
        mn_l = mn_l + mh * m_new

    hh = num / den
    kw = k * wst
    c_ref[...] = cs_l * c_old + _same_head_matrix(1.0) * _dot_tn(kw.astype(BF16), vb)
    n_ref[...] = cs_l * n_old + jnp.sum(kw, axis=0, keepdims=True)
    m_ref[...] = mn_l
    y_ref[...] = _head_norm(hh, HEAD_LN_EPS) * ng_ref[...] * og


def _mlstm(pm, pg, cw, cb, gb, ng, bsz, seq):
    L = M_CHUNK
    nc = seq // L
    row = lambda b, i: (b * nc + i, 0)
    const = lambda b, i: (0, 0)
    return pl.pallas_call(
        _mlstm_kernel,
        grid=(bsz, nc),
        in_specs=[pl.BlockSpec((L, 4 * GW), row), pl.BlockSpec((L, LANES), row),
                  pl.BlockSpec(cw.shape, const), pl.BlockSpec(cb.shape, const),
                  pl.BlockSpec(gb.shape, const), pl.BlockSpec(ng.shape, const)],
        out_specs=pl.BlockSpec((L, GW), row),
        out_shape=jax.ShapeDtypeStruct((bsz * seq, GW), F32),
        scratch_shapes=[pltpu.VMEM((SUBLANES + L, 2 * GW), F32), pltpu.VMEM((GW, GW), F32),
                        pltpu.VMEM((1, GW), F32), pltpu.VMEM((1, GW), F32)],
        compiler_params=_cparams(("parallel", "arbitrary")),
        name="mlstm",
    )(pm, pg, cw, cb, gb, ng)


def _stack_heads(x, masks):
    return jnp.concatenate([x * mh for mh in masks], axis=0)


def _unstack_heads(xs, L):
    out = xs[0:L]
    for h in range(1, NH):
        out = out + xs[h * L:(h + 1) * L]
    return out


def _unit_lower_inverse(a, ri, ci, log2_block):
    same = lambda s: (ri >> s) == (ci >> s)
    eye = jnp.where(ri == ci, 1.0, 0.0).astype(F32)
    a8 = jnp.where(same(3), a, 0.0)
    a8_2 = _dot(a8, a8, HI)
    a8_4 = _dot(a8_2, a8_2, HI)
    inv = _dot(_dot(eye - a8, eye + a8_2, HI), eye + a8_4, HI)
    for s in range(3, log2_block):
        off = jnp.where(same(s + 1) & jnp.logical_not(same(s)), a, 0.0)
        inv = inv - _dot(inv, _dot(off, inv, HI), HI)
    return inv


def _rwkv_kernel(pr_ref, mu_ref, w0_ref, a0_ref, wc_ref, gup_ref, kk_ref, ka_ref, rk_ref, lg_ref, lb_ref,
                 y_ref, xs_ref, st_ref):
    L = R_CHUNK

    @pl.when(pl.program_id(1) == 0)
    def _():
        xs_ref[0:SUBLANES, :] = jnp.zeros((SUBLANES, 4 * GW), F32)
        st_ref[...] = jnp.zeros_like(st_ref)

    pr = pr_ref[...]
    xs_ref[SUBLANES:SUBLANES + L, :] = pr
    shifted = xs_ref[pl.ds(SUBLANES - 1, L), :]
    xs_ref[0:SUBLANES, :] = pr[L - SUBLANES:L, :]
    prm = pr + mu_ref[...] * (shifted - pr)

    r = prm[:, 0:GW]
    k = prm[:, GW:2 * GW]
    v = prm[:, 2 * GW:3 * GW]
    z = prm[:, 3 * GW:3 * GW + LANES]
    lane = lax.broadcasted_iota(jnp.int32, (1, LANES), 1)
    wa = _dot(jnp.where(lane < DH, jnp.tanh(z), z), wc_ref[...], HI)
    w_log = -_softplus(-(w0_ref[...] + wa[:, 0:GW])) - 0.5
    logw = -jnp.exp(w_log)
    a = _sigmoid(a0_ref[...] + wa[:, GW:2 * GW])
    g = _dot(_sigmoid(prm[:, 3 * GW + LANES:4 * GW]), gup_ref[...], HI)

    ones_bd = _same_head_matrix(1.0)
    kk = k * kk_ref[...]
    kk = kk / jnp.maximum(jnp.sqrt(_dot(kk * kk, ones_bd, HI)), 1e-12)
    k2 = k * (1.0 + (a - 1.0) * ka_ref[...])
    bonus = _dot(r * k2 * rk_ref[...], ones_bd, HI) * v
    b = kk * a

    ri = lax.broadcasted_iota(jnp.int32, (L, L), 0)
    ci = lax.broadcasted_iota(jnp.int32, (L, L), 1)
    cum = _dot(jnp.where(ri >= ci, 1.0, 0.0).astype(F32), logw, HI)
    p_incl = jnp.exp(cum)
    p_excl = jnp.exp(cum - logw)
    p_inv = jnp.exp(-cum)
    p_end = jnp.exp(cum[L - 1:L, :] - cum)
    p_last = p_incl[L - 1:L, :]

    masks = _head_masks()
    kk_s = _stack_heads(kk * p_excl, masks)
    r_s = _stack_heads(r * p_incl, masks)
    bi_s = _stack_heads(b * p_inv, masks)
    ki_s = _stack_heads(k2 * p_inv, masks)
    be_s = _stack_heads(b * p_end, masks)
    ke_s = _stack_heads(k2 * p_end, masks)
    v_s = _stack_heads(v, masks)

    n4 = NH * L
    x = _dot_nt(jnp.concatenate([kk_s, r_s], axis=0), jnp.concatenate([bi_s, ki_s], axis=0), HI)
    ri4 = lax.broadcasted_iota(jnp.int32, (n4, n4), 0)
    ci4 = lax.broadcasted_iota(jnp.int32, (n4, n4), 1)
    strict = ri4 > ci4
    incl = ri4 >= ci4
    a_b = jnp.where(strict, x[0:n4, 0:n4], 0.0)
    a_k = jnp.where(strict, x[0:n4, n4:2 * n4], 0.0)
    b_b = jnp.where(incl, x[n4:2 * n4, 0:n4], 0.0)
    b_k = jnp.where(incl, x[n4:2 * n4, n4:2 * n4], 0.0)

    t_inv = _unit_lower_inverse(a_b, ri4, ci4, HEAD_SHIFT)
    tw = _dot(t_inv, jnp.concatenate([kk_s, _dot(a_k, v_s, HI)], axis=1), HI)
    w_q = tw[:, 0:GW]
    u_0 = tw[:, GW:2 * GW]
    bbw = _dot(b_b, tw, HI)
    q_eff = _unstack_heads(r_s - bbw[:, 0:GW], L)
    y_0 = _unstack_heads(_dot(b_k, v_s, HI) - bbw[:, GW:2 * GW], L)
    bew = _dot_tn(be_s, tw, HI)
    rg = lax.broadcasted_iota(jnp.int32, (GW, GW), 0)
    cg = lax.broadcasted_iota(jnp.int32, (GW, GW), 1)
    m_t = jnp.where(rg == cg, p_last, 0.0) - bew[:, 0:GW]
    n_t = _dot_tn(ke_s, v_s, HI) - bew[:, GW:2 * GW]

    st = st_ref[...]
    y = _dot(q_eff, st, HI) + y_0
    st_ref[...] = _dot(m_t, st, HI) + n_t

    y = _head_norm(y, RWKV_LN_EPS) * lg_ref[...] + lb_ref[...]
    y_ref[...] = (y + bonus) * g


def _rwkv(pr, mu, w0, a0, wc, gup, kk, ka, rk, lg, lb, bsz, seq):
    L = R_CHUNK
    nc = seq // L
    row = lambda b, i: (b * nc + i, 0)
    const = lambda b, i: (0, 0)
    params = (mu, w0, a0, wc, gup, kk, ka, rk, lg, lb)
    return pl.pallas_call(
        _rwkv_kernel,
        grid=(bsz, nc),
        in_specs=[pl.BlockSpec((L, 4 * GW), row)] + [pl.BlockSpec(p.shape, const) for p in params],
        out_specs=pl.BlockSpec((L, GW), row),
        out_shape=jax.ShapeDtypeStruct((bsz * seq, GW), F32),
        scratch_shapes=[pltpu.VMEM((SUBLANES + L, 4 * GW), F32), pltpu.VMEM((GW, GW), F32)],
        compiler_params=_cparams(("parallel", "arbitrary")),
        name="rwkv",
    )(pr, *params)


def _ret_kernel(pt_ref, cos_ref, sa_ref, sb_ref, dm_ref, qd_ref, kd_ref, cd_ref, gg_ref, gb_ref,
                y_ref, r_ref):
    L = M_CHUNK

    @pl.when(pl.program_id(1) == 0)
    def _():
        r_ref[...] = jnp.zeros_like(r_ref)

    cos = cos_ref[...]
    sa = sa_ref[...]
    sb = sb_ref[...]

    def rope(t):
        return t * cos + pltpu.roll(t, GW - DH // 2, 1) * sa + pltpu.roll(t, DH // 2, 1) * sb

    q = rope(pt_ref[:, 0:GW])
    k = rope(pt_ref[:, GW:2 * GW]) * (DH ** -0.5)
    v = pt_ref[:, 2 * GW:3 * GW]
    gate = pt_ref[:, 3 * GW:4 * GW]

    masks = _head_masks()
    kb = k.astype(BF16)
    vb = v.astype(BF16)
    r_old = r_ref[...]
    o = _dot(q.astype(BF16), r_old.astype(BF16)) * qd_ref[...]
    for h in range(NH):
        sc = _dot_nt((q * masks[h]).astype(BF16), kb) * dm_ref[h]
        o = o + masks[h] * _dot(sc.astype(BF16), vb)
    r_ref[...] = r_old * cd_ref[...] + _same_head_matrix(1.0) * _dot_tn((k * kd_ref[...]).astype(BF16), vb)
    y_ref[...] = gate * _sigmoid(gate) * (_head_norm(o, HEAD_LN_EPS) * gg_ref[...] + gb_ref[...])


def _retention(pt, cos, sa, sb, dm, qd, kd, cd, gg, gb, bsz, seq):
    L = M_CHUNK
    nc = seq // L
    row = lambda b, i: (b * nc + i, 0)
    pos = lambda b, i: (i, 0)
    const2 = lambda b, i: (0, 0)
    const3 = lambda b, i: (0, 0, 0)
    return pl.pallas_call(
        _ret_kernel,
        grid=(bsz, nc),
        in_specs=[pl.BlockSpec((L, 4 * GW), row),
                  pl.BlockSpec((L, GW), pos), pl.BlockSpec((L, GW), pos), pl.BlockSpec((L, GW), pos),
                  pl.BlockSpec(dm.shape, const3), pl.BlockSpec(qd.shape, const2), pl.BlockSpec(kd.shape, const2),
                  pl.BlockSpec(cd.shape, const2), pl.BlockSpec(gg.shape, const2), pl.BlockSpec(gb.shape, const2)],
        out_specs=pl.BlockSpec((L, GW), row),
        out_shape=jax.ShapeDtypeStruct((bsz * seq, GW), F32),
        scratch_shapes=[pltpu.VMEM((GW, GW), F32)],
        compiler_params=_cparams(("parallel", "arbitrary")),
        name="retention",
    )(pt, cos, sa, sb, dm, qd, kd, cd, gg, gb)


def _gelu_tanh(x):
    return 0.5 * x * (1.0 + jnp.tanh(0.7978845608028654 * (x + 0.044715 * (x * x * x))))


def _s5_kernel(u_ref, ar_ref, ai_ref, bm_ref, cm_ref, d_ref, wg_ref, bg_ref, y_ref, bu_ref, x_ref, *, nb):
    T = u_ref.shape[0] // nb

    @pl.when(pl.program_id(0) == 0)
    def _():
        x_ref[...] = jnp.zeros_like(x_ref)

    u = u_ref[...]
    bu_ref[...] = _dot(u.astype(BF16), bm_ref[...])
    ar = jnp.broadcast_to(ar_ref[...], (nb, S5_N))
    ai = jnp.broadcast_to(ai_ref[...], (nb, S5_N))

    def step(t, carry):
        xr, xi = carry
        r0 = pl.multiple_of(t * nb, nb)
        nr = ar * xr - ai * xi + bu_ref[pl.ds(r0, nb), 0:S5_N]
        ni = ar * xi + ai * xr + bu_ref[pl.ds(r0, nb), S5_N:2 * S5_N]
        bu_ref[pl.ds(r0, nb), 0:S5_N] = nr
        bu_ref[pl.ds(r0, nb), S5_N:2 * S5_N] = ni
        return nr, ni

    xr, xi = lax.fori_loop(0, T, step, (x_ref[:, 0:S5_N], x_ref[:, S5_N:2 * S5_N]), unroll=4)
    x_ref[:, 0:S5_N] = xr
    x_ref[:, S5_N:2 * S5_N] = xi

    y = _dot(bu_ref[...].astype(BF16), cm_ref[...])
    y = _gelu_tanh(y + d_ref[...] * u)
    z = _dot(y.astype(BF16), wg_ref[...]) + bg_ref[...]
    y_ref[...] = z[:, 0:GW] * _sigmoid(z[:, GW:2 * GW])


def _s5(ps_tm, ar, ai, bm, cm, d, wg, bg, bsz, seq):
    T = S5_T
    rows = T * bsz
    const = lambda i: (0, 0)
    params = (ar, ai, bm, cm, d, wg, bg)
    return pl.pallas_call(
        functools.partial(_s5_kernel, nb=bsz),
        grid=(seq // T,),
        in_specs=[pl.BlockSpec((rows, GW), lambda i: (i, 0))] + [pl.BlockSpec(p.shape, const) for p in params],
        out_specs=pl.BlockSpec((rows, GW), lambda i: (i, 0)),
        out_shape=jax.ShapeDtypeStruct((seq * bsz, GW), F32),
        scratch_shapes=[pltpu.VMEM((rows, 2 * S5_N), F32), pltpu.VMEM((bsz, 2 * S5_N), F32)],
        compiler_params=_cparams(("arbitrary",)),
        name="s5",
    )(ps_tm, *params)


def _outffn_kernel(x_ref, ym_ref, yr_ref, yt_ref, ys_ref, wo_ref, g1_ref, g2_ref, g3_ref, w1_ref, w2_ref, o_ref):
    y = _dot(ym_ref[...].astype(BF16), wo_ref[0:GW, :])
    y = y + _dot(yr_ref[...].astype(BF16), wo_ref[GW:2 * GW, :])
    y = y + _dot(yt_ref[...].astype(BF16), wo_ref[2 * GW:3 * GW, :])
    y = y + _dot(ys_ref[...].astype(BF16), wo_ref[3 * GW:4 * GW, :])
    x1 = x_ref[...] + _rms(y, g1_ref[...])
    h = _rms(x1, g2_ref[...]).astype(BF16)
    f = jnp.zeros_like(x1)
    for j in range(D_FF // D_MODEL):
        a = jnp.maximum(_dot(h, w1_ref[:, j * D_MODEL:(j + 1) * D_MODEL]), 0.0)
        f = f + _dot((a * a).astype(BF16), w2_ref[j * D_MODEL:(j + 1) * D_MODEL, :])
    o_ref[...] = x1 + _rms(f, g3_ref[...])


def _outffn(x2, ym, yr, yt, ys_tm, wo, g1, g2, g3, w1, w2, bsz, seq):
    tm = TM_PROJ
    nt = seq // tm
    row = lambda b, i: (b * nt + i, 0)
    const = lambda b, i: (0, 0)
    wspec = lambda w: pl.BlockSpec(w.shape, const, pipeline_mode=pl.Buffered(1))
    return pl.pallas_call(
        _outffn_kernel,
        grid=(bsz, nt),
        in_specs=[pl.BlockSpec((tm, D_MODEL), row),
                  pl.BlockSpec((tm, GW), row), pl.BlockSpec((tm, GW), row), pl.BlockSpec((tm, GW), row),
                  pl.BlockSpec((tm, GW), lambda b, i: (i, b)),
                  wspec(wo), pl.BlockSpec((1, D_MODEL), const), pl.BlockSpec((1, D_MODEL), const),
                  pl.BlockSpec((1, D_MODEL), const), wspec(w1), wspec(w2)],
        out_specs=pl.BlockSpec((tm, D_MODEL), row),
        out_shape=jax.ShapeDtypeStruct((bsz * seq, D_MODEL), F32),
        compiler_params=_cparams(("parallel", "parallel")),
        name="outffn",
    )(x2, ym, yr, yt, ys_tm, wo, g1, g2, g3, w1, w2)


def _rope_tables(seq):
    half = DH // 2
    inv = ROPE_BASE ** (-jnp.arange(half, dtype=F32) / half)
    ang = jnp.arange(seq, dtype=F32)[:, None] * inv[None, :]
    cos, sin = jnp.cos(ang), jnp.sin(ang)
    zero = jnp.zeros_like(sin)
    tile = lambda first, second: jnp.tile(jnp.concatenate([first, second], axis=-1), (1, NH))
    return tile(cos, cos), tile(-sin, zero), tile(zero, sin)


def _retention_tables():
    L = M_CHUNK
    log_gamma = jnp.log(1.0 - 2.0 ** (-5.0 - jnp.arange(NH, dtype=F32)))
    idx = jnp.arange(L, dtype=F32)
    diff = idx[:, None] - idx[None, :]
    causal = diff >= 0
    decay_mat = jnp.where(causal, jnp.exp(jnp.where(causal, diff, 0.0) * log_gamma[:, None, None]), 0.0)
    lanes = lambda t: jnp.repeat(t.T, DH, axis=1)
    q_decay = lanes(jnp.exp((idx + 1.0) * log_gamma[:, None]))
    k_decay = lanes(jnp.exp((L - 1.0 - idx) * log_gamma[:, None]))
    chunk_decay = jnp.repeat(jnp.exp(L * log_gamma), DH)[None, :]
    return decay_mat, q_decay, k_decay, chunk_decay


def _s5_tables(lam_re, lam_im, log_dt, b_re, b_im, c_re, c_im):
    G, P, C = S5_GROUPS, S5_STATE, S5_GROUP
    dt = jnp.exp(log_dt)[:, None]
    mag = jnp.exp(lam_re * dt)
    abar_re, abar_im = mag * jnp.cos(lam_im * dt), mag * jnp.sin(lam_im * dt)
    den = lam_re * lam_re + lam_im * lam_im
    num_re, num_im = abar_re - 1.0, abar_im
    f_re = (num_re * lam_re + num_im * lam_im) / den
    f_im = (num_im * lam_re - num_re * lam_im) / den
    bbar_re = f_re[..., None] * b_re - f_im[..., None] * b_im
    bbar_im = f_re[..., None] * b_im + f_im[..., None] * b_re
    eye = jnp.eye(G, dtype=F32)
    bd_in = lambda t: jnp.einsum('gpc,gh->gchp', t, eye).reshape(G * C, G * P)
    bd_out = lambda t: jnp.einsum('gcp,gh->gphc', t, eye).reshape(G * P, G * C)
    bmat = jnp.concatenate([bd_in(bbar_re), bd_in(bbar_im)], axis=1)
    cmat = jnp.concatenate([bd_out(c_re), -bd_out(c_im)], axis=0)
    return abar_re.reshape(1, G * P), abar_im.reshape(1, G * P), bmat.astype(BF16), cmat.astype(BF16)


def _pad_lanes(t, width=LANES):
    return jnp.pad(t, ((0, 0), (0, width - t.shape[-1])))


def kernel(x, norm_mix_pre, norm_mix_post, w_in, m_conv_w, m_conv_b, m_i_bias, m_f_bias, m_norm, r_mu, r_w0, r_w_up, r_a0, r_a_up, r_g_up, r_k_k, r_k_a, r_r_k, r_ln_g, r_ln_b, t_gn_g, t_gn_b, s_lam_re, s_lam_im, s_log_dt, s_b_re, s_b_im, s_c_re, s_c_im, s_d, s_w_glu, s_b_glu, w_out, norm_ffn_pre, norm_ffn_post, w_ff1, w_ff2):
    bsz, seq, _ = x.shape
    depth = w_in.shape[0]
    assert seq % TM_PROJ == 0 and seq % M_CHUNK == 0 and seq % S5_T == 0 and bsz % SUBLANES == 0
    row = lambda t: t.reshape(1, -1)
    m_cols = 4 * GW + 2 * NH
    o_r = m_cols
    o_t = o_r + 4 * GW
    o_s = o_t + 4 * GW

    cos, sin_a, sin_b = _rope_tables(seq)
    dm, qd, kd, cd = _retention_tables()

    x2 = x.reshape(bsz * seq, D_MODEL)
    for l in range(depth):
        wl = w_in[l]
        wm = wl[:, 0:4 * GW].astype(BF16)
        wg = _pad_lanes(wl[:, 4 * GW:m_cols]).astype(BF16)
        wr = wl[:, o_r:o_t].astype(BF16)
        wt = wl[:, o_t:o_s].astype(BF16)
        ws = wl[:, o_s:].astype(BF16)
        pm, pg, pr, pt, ps_tm = _inproj(x2, row(norm_mix_pre[l]), wm, wg, wr, wt, ws, bsz, seq)

        gate_bias = _pad_lanes(jnp.concatenate([m_i_bias[l], m_f_bias[l]])[None, :])
        y_m = _mlstm(pm, pg, m_conv_w[l], row(m_conv_b[l]), gate_bias, row(m_norm[l]), bsz, seq)

        zeros = jnp.zeros((DH, GW), F32)
        w_comb = jnp.concatenate([jnp.concatenate([r_w_up[l], zeros], axis=1),
                                  jnp.concatenate([zeros, r_a_up[l]], axis=1)], axis=0)
        y_r = _rwkv(pr, row(r_mu[l]), row(r_w0[l]), row(r_a0[l]), w_comb, r_g_up[l], row(r_k_k[l]),
                    row(r_k_a[l]), row(r_r_k[l]), row(r_ln_g[l]), row(r_ln_b[l]), bsz, seq)

        y_t = _retention(pt, cos, sin_a, sin_b, dm, qd, kd, cd, row(t_gn_g[l]), row(t_gn_b[l]), bsz, seq)

        ar, ai, bmat, cmat = _s5_tables(s_lam_re[l], s_lam_im[l], s_log_dt[l], s_b_re[l], s_b_im[l],
                                        s_c_re[l], s_c_im[l])
        y_s_tm = _s5(ps_tm.reshape(seq * bsz, GW), ar, ai, bmat, cmat, row(s_d[l]),
                     s_w_glu[l].astype(BF16), row(s_b_glu[l]), bsz, seq)

        x2 = _outffn(x2, y_m, y_r, y_t, y_s_tm.reshape(seq, bsz * GW), w_out[l].astype(BF16),
                     row(norm_mix_post[l]), row(norm_ffn_pre[l]), row(norm_ffn_post[l]),
                     w_ff1[l].astype(BF16), w_ff2[l].astype(BF16), bsz, seq)
    return x2.reshape(bsz, seq, D_MODEL)
```

```python
import functools

import jax
import jax.numpy as jnp
from jax import lax
from jax.experimental import pallas as pl
from jax.experimental.pallas import tpu as pltpu

F32 = jnp.float32
BF16 = jnp.bfloat16

D_MODEL = 1024
GW = 256
NH = 4
DH = 64
HEAD_SHIFT = 6
D_FF = 4 * D_MODEL
S5_GROUPS = 16
S5_GROUP = 16
S5_STATE = 64
S5_N = S5_GROUPS * S5_STATE
CONV_WIDTH = 4
ROPE_BASE = 10000.0
RMS_EPS = 1e-6
HEAD_LN_EPS = 1e-5
RWKV_LN_EPS = 64e-5

SUBLANES = 8
LANES = 128
M_CHUNK = 128
M_BLOCK = 512
R_CHUNK = 64
R_BLOCK = 256
S5_T = 128
TM_PROJ = 512
VMEM_LIMIT = 56 * 1024 * 1024


def _cparams(sem):
    return pltpu.CompilerParams(dimension_semantics=sem, vmem_limit_bytes=VMEM_LIMIT)


def _dot(a, b, precision=None):
    return jnp.dot(a, b, preferred_element_type=F32, precision=precision)


def _dot_nt(a, b, precision=None):
    return lax.dot_general(a, b, (((1,), (1,)), ((), ())), preferred_element_type=F32, precision=precision)


def _dot_tn(a, b, precision=None):
    return lax.dot_general(a, b, (((0,), (0,)), ((), ())), preferred_element_type=F32, precision=precision)


_FORMS = {"nn": _dot, "nt": _dot_nt, "tn": _dot_tn}


def _bdot(a, b, form="nn"):
    return _FORMS[form](a.astype(BF16), b.astype(BF16))


def _split_bf16(a, terms):
    parts = []
    for _ in range(terms - 1):
        hi = a.astype(BF16)
        parts.append(hi)
        a = a - hi.astype(F32)
    parts.append(a.astype(BF16))
    return parts


def _dot_sel(a, sel, terms):
    sel = sel.astype(BF16)
    return sum(_dot(p, sel) for p in _split_bf16(a, terms))


def _sel_dot(sel, b, terms):
    sel = sel.astype(BF16)
    return sum(_dot(sel, p) for p in _split_bf16(b, terms))


def _sigmoid(x):
    return 1.0 / (1.0 + jnp.exp(-x))


def _softplus(x):
    return jnp.maximum(x, 0.0) + jnp.log(1.0 + jnp.exp(-jnp.abs(x)))


def _head_masks():
    lane = lax.broadcasted_iota(jnp.int32, (1, GW), 1)
    return [jnp.where((lane >> HEAD_SHIFT) == h, 1.0, 0.0).astype(F32) for h in range(NH)]


def _same_head_matrix(scale):
    r = lax.broadcasted_iota(jnp.int32, (GW, GW), 0)
    c = lax.broadcasted_iota(jnp.int32, (GW, GW), 1)
    return jnp.where((r >> HEAD_SHIFT) == (c >> HEAD_SHIFT), scale, 0.0).astype(F32)


def _head_norm(x, eps):
    avg = _same_head_matrix(1.0 / DH)
    mu = _dot_sel(x, avg, 2)
    xc = x - mu
    var = _dot_sel(xc * xc, avg, 2)
    return xc * lax.rsqrt(var + eps)


def _rms(x, g):
    return x * lax.rsqrt(jnp.mean(x * x, axis=-1, keepdims=True) + RMS_EPS) * g


def _inproj_kernel(x_ref, g_ref, wm_ref, wg_ref, wr_ref, wt_ref, ws_ref,
                   pm_ref, pg_ref, pr_ref, pt_ref, ps_ref):
    h = _rms(x_ref[...], g_ref[...]).astype(BF16)
    pm_ref[...] = _dot(h, wm_ref[...])
    pg_ref[...] = _dot(h, wg_ref[...])
    pr_ref[...] = _dot(h, wr_ref[...])
    pt_ref[...] = _dot(h, wt_ref[...])
    ps_ref[...] = _dot(h, ws_ref[...])


def _inproj(x2, g, wm, wg, wr, wt, ws, bsz, seq):
    n = bsz * seq
    tm = TM_PROJ
    nt = seq // tm
    row = lambda b, i: (b * nt + i, 0)
    const = lambda b, i: (0, 0)
    wspec = lambda w: pl.BlockSpec(w.shape, const, pipeline_mode=pl.Buffered(1))
    return pl.pallas_call(
        _inproj_kernel,
        grid=(bsz, nt),
        in_specs=[pl.BlockSpec((tm, D_MODEL), row), pl.BlockSpec((1, D_MODEL), const),
                  wspec(wm), wspec(wg), wspec(wr), wspec(wt), wspec(ws)],
        out_specs=[pl.BlockSpec((tm, 4 * GW), row), pl.BlockSpec((tm, LANES), row),
                   pl.BlockSpec((tm, 4 * GW), row), pl.BlockSpec((tm, 4 * GW), row),
                   pl.BlockSpec((tm, GW), lambda b, i: (i, b))],
        out_shape=[jax.ShapeDtypeStruct((n, 4 * GW), F32), jax.ShapeDtypeStruct((n, LANES), F32),
                   jax.ShapeDtypeStruct((n, 4 * GW), F32), jax.ShapeDtypeStruct((n, 4 * GW), F32),
                   jax.ShapeDtypeStruct((seq, bsz * GW), F32)],
        compiler_params=_cparams(("parallel", "parallel")),
        name="inproj",
    )(x2, g, wm, wg, wr, wt, ws)


def _mlstm_kernel(pm_ref, pg_ref, cw_ref, cb_ref, gb_ref, ng_ref, y_ref, xs_ref, c_ref, n_ref, m_ref):
    MB, L = M_BLOCK, M_CHUNK
    log2_l = L.bit_length() - 1

    @pl.when(pl.program_id(1) == 0)
    def _():
        xs_ref[0:SUBLANES, :] = jnp.zeros((SUBLANES, 2 * GW), F32)
        c_ref[...] = jnp.zeros_like(c_ref)
        n_ref[...] = jnp.zeros_like(n_ref)
        m_ref[...] = jnp.zeros_like(m_ref)

    xqk = pm_ref[:, 0:2 * GW]
    xs_ref[SUBLANES:SUBLANES + MB, :] = xqk
    conv = cb_ref[...]
    for j in range(CONV_WIDTH):
        conv = conv + xs_ref[pl.ds(SUBLANES - (CONV_WIDTH - 1) + j, MB), :] * cw_ref[j:j + 1, :]
    xs_ref[0:SUBLANES, :] = xqk[MB - SUBLANES:MB, :]
    qk = conv * _sigmoid(conv)
    q_all = qk[:, 0:GW]
    k_all = qk[:, GW:2 * GW] * (DH ** -0.5)
    v_all = pm_ref[:, 2 * GW:3 * GW]

    lane = lax.broadcasted_iota(jnp.int32, (1, LANES), 1)
    g2 = pg_ref[...] + gb_ref[...]
    lf = jnp.where((lane >= NH) & (lane < 2 * NH), -_softplus(-g2), 0.0)
    rb = lax.broadcasted_iota(jnp.int32, (MB, MB), 0)
    cb = lax.broadcasted_iota(jnp.int32, (MB, MB), 1)
    sel = jnp.where(((rb >> log2_l) == (cb >> log2_l)) & (rb >= cb), 1.0, 0.0)
    gcum_all = _sel_dot(sel, lf, 3)
    colmat_all = jnp.where(lane < NH, g2, gcum_all)
    rowmat_all = colmat_all.T

    ri = lax.broadcasted_iota(jnp.int32, (L, L), 0)
    ci = lax.broadcasted_iota(jnp.int32, (L, L), 1)
    causal = ri >= ci
    masks = _head_masks()
    same_head = _same_head_matrix(1.0)
    c_st = c_ref[...]
    n_st = n_ref[...]
    m_st = m_ref[...]
    hs = []
    for c in range(MB // L):
        rows = slice(c * L, (c + 1) * L)
        q, k, v = q_all[rows], k_all[rows], v_all[rows]
        colmat = colmat_all[rows]
        rowmat = rowmat_all[:, rows]
        q_c = _bdot(q, c_st)
        kb = k.astype(BF16)
        vb = v.astype(BF16)

        num = jnp.zeros((L, GW), F32)
        den = jnp.zeros((L, GW), F32)
        wst = jnp.zeros((L, GW), F32)
        cs_l = jnp.zeros((1, GW), F32)
        mn_l = jnp.zeros((1, GW), F32)
        for h in range(NH):
            mh = masks[h]
            gcol = colmat[:, NH + h:NH + h + 1]
            icol = colmat[:, h:h + 1]
            grow = rowmat[NH + h:NH + h + 1, :]
            irow = rowmat[h:h + 1, :]
            mprev = m_st[:, h * DH:h * DH + 1]
            d_intra = jnp.where(causal, gcol - grow + irow, -jnp.inf)
            d_inter = gcol + mprev
            m_row = jnp.maximum(d_inter, jnp.max(d_intra, axis=-1, keepdims=True))
            sc = _dot_nt((q * mh).astype(BF16), kb)
            s_qk = sc * jnp.exp(d_intra - m_row)
            w_inter = jnp.exp(d_inter - m_row)
            num = num + mh * (_dot(s_qk.astype(BF16), vb) + w_inter * q_c)
            den_h = (jnp.sum(s_qk, axis=-1, keepdims=True)
                     + w_inter * jnp.sum(q * (mh * n_st), axis=-1, keepdims=True))
            den = den + mh * jnp.maximum(jnp.abs(den_h), jnp.exp(-m_row))
            g_last = colmat[L - 1:L, NH + h:NH + h + 1]
            d_state = g_last - gcol + icol
            m_new = jnp.maximum(g_last + mprev, jnp.max(d_state, axis=0, keepdims=True))
            wst = wst + mh * jnp.exp(d_state - m_new)
            cs_l = cs_l + mh * jnp.exp(g_last + mprev - m_new)
            mn_l = mn_l + mh * m_new

        hs.append(num / den)
        kw = k * wst
        c_st = cs_l * c_st + same_head * _dot_tn(kw.astype(BF16), vb)
        n_st = cs_l * n_st + jnp.sum(kw, axis=0, keepdims=True)
        m_st = mn_l
    c_ref[...] = c_st
    n_ref[...] = n_st
    m_ref[...] = m_st
    og = _sigmoid(pm_ref[:, 3 * GW:4 * GW])
    y_ref[...] = _head_norm(jnp.concatenate(hs, axis=0), HEAD_LN_EPS) * ng_ref[...] * og


def _mlstm(pm, pg, cw, cb, gb, ng, bsz, seq):
    MB = M_BLOCK
    nb = seq // MB
    row = lambda b, i: (b * nb + i, 0)
    const = lambda b, i: (0, 0)
    return pl.pallas_call(
        _mlstm_kernel,
        grid=(bsz, nb),
        in_specs=[pl.BlockSpec((MB, 4 * GW), row), pl.BlockSpec((MB, LANES), row),
                  pl.BlockSpec(cw.shape, const), pl.BlockSpec(cb.shape, const),
                  pl.BlockSpec(gb.shape, const), pl.BlockSpec(ng.shape, const)],
        out_specs=pl.BlockSpec((MB, GW), row),
        out_shape=jax.ShapeDtypeStruct((bsz * seq, GW), F32),
        scratch_shapes=[pltpu.VMEM((SUBLANES + MB, 2 * GW), F32), pltpu.VMEM((GW, GW), F32),
                        pltpu.VMEM((1, GW), F32), pltpu.VMEM((1, GW), F32)],
        compiler_params=_cparams(("parallel", "arbitrary")),
        name="mlstm",
    )(pm, pg, cw, cb, gb, ng)


def _stack_heads(x, masks):
    return jnp.concatenate([x * mh for mh in masks], axis=0)


def _unstack_heads(xs, L):
    out = xs[0:L]
    for h in range(1, NH):
        out = out + xs[h * L:(h + 1) * L]
    return out


def _unit_lower_inverse(a, ri, ci, log2_block):
    same = lambda s: (ri >> s) == (ci >> s)
    eye = jnp.where(ri == ci, 1.0, 0.0).astype(F32)
    a8 = jnp.where(same(3), a, 0.0)
    a8_2 = _bdot(a8, a8)
    a8_4 = _bdot(a8_2, a8_2)
    inv = _bdot(_bdot(eye - a8, eye + a8_2), eye + a8_4)
    for s in range(3, log2_block):
        off = jnp.where(same(s + 1) & jnp.logical_not(same(s)), a, 0.0)
        inv = inv - _bdot(inv, _bdot(off, inv))
    return inv


def _rwkv_kernel(pr_ref, mu_ref, w0_ref, a0_ref, wc_ref, gup_ref, kk_ref, ka_ref, rk_ref, lg_ref, lb_ref,
                 y_ref, xs_ref, st_ref):
    RB, L = R_BLOCK, R_CHUNK
    log2_l = L.bit_length() - 1

    @pl.when(pl.program_id(1) == 0)
    def _():
        xs_ref[0:SUBLANES, :] = jnp.zeros((SUBLANES, 4 * GW), F32)
        st_ref[...] = jnp.zeros_like(st_ref)

    pr = pr_ref[...]
    xs_ref[SUBLANES:SUBLANES + RB, :] = pr
    shifted = xs_ref[pl.ds(SUBLANES - 1, RB), :]
    xs_ref[0:SUBLANES, :] = pr[RB - SUBLANES:RB, :]
    prm = pr + mu_ref[...] * (shifted - pr)

    r = prm[:, 0:GW]
    k = prm[:, GW:2 * GW]
    v = prm[:, 2 * GW:3 * GW]
    z = prm[:, 3 * GW:3 * GW + LANES]
    lane = lax.broadcasted_iota(jnp.int32, (1, LANES), 1)
    wa = _bdot(jnp.where(lane < DH, jnp.tanh(z), z), wc_ref[...])
    w_log = -_softplus(-(w0_ref[...] + wa[:, 0:GW])) - 0.5
    logw = -jnp.exp(w_log)
    a = _sigmoid(a0_ref[...] + wa[:, GW:2 * GW])
    g = _bdot(_sigmoid(prm[:, 3 * GW + LANES:4 * GW]), gup_ref[...])

    ones_bd = _same_head_matrix(1.0)
    kk = k * kk_ref[...]
    kk = kk / jnp.maximum(jnp.sqrt(_dot_sel(kk * kk, ones_bd, 2)), 1e-12)
    k2 = k * (1.0 + (a - 1.0) * ka_ref[...])
    bonus = _dot_sel(r * k2 * rk_ref[...], ones_bd, 2) * v
    b = kk * a

    ri = lax.broadcasted_iota(jnp.int32, (2 * RB, RB), 0)
    ci = lax.broadcasted_iota(jnp.int32, (2 * RB, RB), 1)
    rr = ri & (RB - 1)
    sel = jnp.where(((rr >> log2_l) == (ci >> log2_l)) & ((ri >= RB) | (rr >= ci)), 1.0, 0.0)
    cums = _sel_dot(sel, logw, 3)
    cum = cums[0:RB]
    tot = cums[RB:2 * RB]
    kkd = kk * jnp.exp(cum - logw)
    rd = r * jnp.exp(cum)
    p_inv = jnp.exp(-cum)
    bi = b * p_inv
    ki = k2 * p_inv
    p_end = jnp.exp(tot - cum)
    be = b * p_end
    ke = k2 * p_end
    p_tot = jnp.exp(tot)

    masks = _head_masks()
    n4 = NH * L
    ri4 = lax.broadcasted_iota(jnp.int32, (n4, n4), 0)
    ci4 = lax.broadcasted_iota(jnp.int32, (n4, n4), 1)
    strict = ri4 > ci4
    incl = ri4 >= ci4
    diag = ri4 == ci4

    st = st_ref[...]
    ys = []
    for c in range(RB // L):
        rows = slice(c * L, (c + 1) * L)
        kk_s = _stack_heads(kkd[rows], masks)
        r_s = _stack_heads(rd[rows], masks)
        bi_s = _stack_heads(bi[rows], masks)
        ki_s = _stack_heads(ki[rows], masks)
        be_s = _stack_heads(be[rows], masks)
        ke_s = _stack_heads(ke[rows], masks)
        v_s = _stack_heads(v[rows], masks)

        x = _bdot(jnp.concatenate([kk_s, r_s], axis=0), jnp.concatenate([bi_s, ki_s], axis=0), "nt")
        a_b = jnp.where(strict, x[0:n4, 0:n4], 0.0)
        a_k = jnp.where(strict, x[0:n4, n4:2 * n4], 0.0)
        b_b = jnp.where(incl, x[n4:2 * n4, 0:n4], 0.0)
        b_k = jnp.where(incl, x[n4:2 * n4, n4:2 * n4], 0.0)

        t_inv = _unit_lower_inverse(a_b, ri4, ci4, log2_l)
        tw = _bdot(t_inv, jnp.concatenate([kk_s, _bdot(a_k, v_s)], axis=1))
        bbw = _bdot(b_b, tw)
        q_eff = _unstack_heads(r_s - bbw[:, 0:GW], L)
        y_0 = _unstack_heads(_bdot(b_k, v_s) - bbw[:, GW:2 * GW], L)
        bew = _bdot(be_s, tw, "tn")
        m_t = jnp.where(diag, p_tot[c * L:c * L + 1, :], 0.0) - bew[:, 0:GW]
        n_t = _bdot(ke_s, v_s, "tn") - bew[:, GW:2 * GW]

        ys.append(_bdot(q_eff, st) + y_0)
        st = _bdot(m_t, st) + n_t
    st_ref[...] = st

    y = _head_norm(jnp.concatenate(ys, axis=0), RWKV_LN_EPS) * lg_ref[...] + lb_ref[...]
    y_ref[...] = (y + bonus) * g


def _rwkv(pr, mu, w0, a0, wc, gup, kk, ka, rk, lg, lb, bsz, seq):
    RB = R_BLOCK
    nb = seq // RB
    row = lambda b, i: (b * nb + i, 0)
    const = lambda b, i: (0, 0)
    params = (mu, w0, a0, wc, gup, kk, ka, rk, lg, lb)
    return pl.pallas_call(
        _rwkv_kernel,
        grid=(bsz, nb),
        in_specs=[pl.BlockSpec((RB, 4 * GW), row)] + [pl.BlockSpec(p.shape, const) for p in params],
        out_specs=pl.BlockSpec((RB, GW), row),
        out_shape=jax.ShapeDtypeStruct((bsz * seq, GW), F32),
        scratch_shapes=[pltpu.VMEM((SUBLANES + RB, 4 * GW), F32), pltpu.VMEM((GW, GW), F32)],
        compiler_params=_cparams(("parallel", "arbitrary")),
        name="rwkv",
    )(pr, *params)


def _ret_kernel(pt_ref, cos_ref, sa_ref, sb_ref, dm_ref, qd_ref, kd_ref, cd_ref, gg_ref, gb_ref,
                y_ref, r_ref):
    MB, L = M_BLOCK, M_CHUNK

    @pl.when(pl.program_id(1) == 0)
    def _():
        r_ref[...] = jnp.zeros_like(r_ref)

    cos = cos_ref[...]
    sa = sa_ref[...]
    sb = sb_ref[...]

    def rope(t):
        return t * cos + pltpu.roll(t, GW - DH // 2, 1) * sa + pltpu.roll(t, DH // 2, 1) * sb

    q_all = rope(pt_ref[:, 0:GW])
    k_all = rope(pt_ref[:, GW:2 * GW]) * (DH ** -0.5)
    v_all = pt_ref[:, 2 * GW:3 * GW]

    masks = _head_masks()
    same_head = _same_head_matrix(1.0)
    r_st = r_ref[...]
    os_ = []
    for c in range(MB // L):
        rows = slice(c * L, (c + 1) * L)
        q, k = q_all[rows], k_all[rows]
        kb = k.astype(BF16)
        vb = v_all[rows].astype(BF16)
        o = _bdot(q, r_st) * qd_ref[...]
        for h in range(NH):
            sc = _dot_nt((q * masks[h]).astype(BF16), kb) * dm_ref[h]
            o = o + masks[h] * _dot(sc.astype(BF16), vb)
        os_.append(o)
        r_st = r_st * cd_ref[...] + same_head * _dot_tn((k * kd_ref[...]).astype(BF16), vb)
    r_ref[...] = r_st
    gate = pt_ref[:, 3 * GW:4 * GW]
    y_ref[...] = gate * _sigmoid(gate) * (_head_norm(jnp.concatenate(os_, axis=0), HEAD_LN_EPS) * gg_ref[...]
                                          + gb_ref[...])


def _retention(pt, cos, sa, sb, dm, qd, kd, cd, gg, gb, bsz, seq):
    MB = M_BLOCK
    nb = seq // MB
    row = lambda b, i: (b * nb + i, 0)
    pos = lambda b, i: (i, 0)
    const2 = lambda b, i: (0, 0)
    const3 = lambda b, i: (0, 0, 0)
    return pl.pallas_call(
        _ret_kernel,
        grid=(bsz, nb),
        in_specs=[pl.BlockSpec((MB, 4 * GW), row),
                  pl.BlockSpec((MB, GW), pos), pl.BlockSpec((MB, GW), pos), pl.BlockSpec((MB, GW), pos),
                  pl.BlockSpec(dm.shape, const3), pl.BlockSpec(qd.shape, const2), pl.BlockSpec(kd.shape, const2),
                  pl.BlockSpec(cd.shape, const2), pl.BlockSpec(gg.shape, const2), pl.BlockSpec(gb.shape, const2)],
        out_specs=pl.BlockSpec((MB, GW), row),
        out_shape=jax.ShapeDtypeStruct((bsz * seq, GW), F32),
        scratch_shapes=[pltpu.VMEM((GW, GW), F32)],
        compiler_params=_cparams(("parallel", "arbitrary")),
        name="retention",
    )(pt, cos, sa, sb, dm, qd, kd, cd, gg, gb)


def _gelu_tanh(x):
    return 0.5 * x * (1.0 + jnp.tanh(0.7978845608028654 * (x + 0.044715 * (x * x * x))))


def _s5_kernel(u_ref, ar_ref, ai_ref, bm_ref, cm_ref, d_ref, wg_ref, bg_ref, y_ref, bu_ref, x_ref, *, nb):
    T = u_ref.shape[0] // nb

    @pl.when(pl.program_id(0) == 0)
    def _():
        x_ref[...] = jnp.zeros_like(x_ref)

    u = u_ref[...]
    bu_ref[...] = _dot(u.astype(BF16), bm_ref[...])
    ar = jnp.broadcast_to(ar_ref[...], (nb, S5_N))
    ai = jnp.broadcast_to(ai_ref[...], (nb, S5_N))

    def step(t, carry):
        xr, xi = carry
        r0 = pl.multiple_of(t * nb, nb)
        nr = ar * xr - ai * xi + bu_ref[pl.ds(r0, nb), 0:S5_N]
        ni = ar * xi + ai * xr + bu_ref[pl.ds(r0, nb), S5_N:2 * S5_N]
        bu_ref[pl.ds(r0, nb), 0:S5_N] = nr
        bu_ref[pl.ds(r0, nb), S5_N:2 * S5_N] = ni
        return nr, ni

    xr, xi = lax.fori_loop(0, T, step, (x_ref[:, 0:S5_N], x_ref[:, S5_N:2 * S5_N]), unroll=4)
    x_ref[:, 0:S5_N] = xr
    x_ref[:, S5_N:2 * S5_N] = xi

    y = _dot(bu_ref[...].astype(BF16), cm_ref[...])
    y = _gelu_tanh(y + d_ref[...] * u)
    z = _dot(y.astype(BF16), wg_ref[...]) + bg_ref[...]
    y_ref[...] = z[:, 0:GW] * _sigmoid(z[:, GW:2 * GW])


def _s5(ps_tm, ar, ai, bm, cm, d, wg, bg, bsz, seq):
    T = S5_T
    rows = T * bsz
    const = lambda i: (0, 0)
    params = (ar, ai, bm, cm, d, wg, bg)
    return pl.pallas_call(
        functools.partial(_s5_kernel, nb=bsz),
        grid=(seq // T,),
        in_specs=[pl.BlockSpec((rows, GW), lambda i: (i, 0))] + [pl.BlockSpec(p.shape, const) for p in params],
        out_specs=pl.BlockSpec((rows, GW), lambda i: (i, 0)),
        out_shape=jax.ShapeDtypeStruct((seq * bsz, GW), F32),
        scratch_shapes=[pltpu.VMEM((rows, 2 * S5_N), F32), pltpu.VMEM((bsz, 2 * S5_N), F32)],
        compiler_params=_cparams(("arbitrary",)),
        name="s5",
    )(ps_tm, *params)


def _outffn_kernel(x_ref, ym_ref, yr_ref, yt_ref, ys_ref, wo_ref, g1_ref, g2_ref, g3_ref, w1_ref, w2_ref, o_ref):
    y = _dot(ym_ref[...].astype(BF16), wo_ref[0:GW, :])
    y = y + _dot(yr_ref[...].astype(BF16), wo_ref[GW:2 * GW, :])
    y = y + _dot(yt_ref[...].astype(BF16), wo_ref[2 * GW:3 * GW, :])
    y = y + _dot(ys_ref[...].astype(BF16), wo_ref[3 * GW:4 * GW, :])
    x1 = x_ref[...] + _rms(y, g1_ref[...])
    h = _rms(x1, g2_ref[...]).astype(BF16)
    f = jnp.zeros_like(x1)
    for j in range(D_FF // D_MODEL):
        a = jnp.maximum(_dot(h, w1_ref[:, j * D_MODEL:(j + 1) * D_MODEL]), 0.0)
        f = f + _dot((a * a).astype(BF16), w2_ref[j * D_MODEL:(j + 1) * D_MODEL, :])
    o_ref[...] = x1 + _rms(f, g3_ref[...])


def _outffn(x2, ym, yr, yt, ys_tm, wo, g1, g2, g3, w1, w2, bsz, seq):
    tm = TM_PROJ
    nt = seq // tm
    row = lambda b, i: (b * nt + i, 0)
    const = lambda b, i: (0, 0)
    wspec = lambda w: pl.BlockSpec(w.shape, const, pipeline_mode=pl.Buffered(1))
    return pl.pallas_call(
        _outffn_kernel,
        grid=(bsz, nt),
        in_specs=[pl.BlockSpec((tm, D_MODEL), row),
                  pl.BlockSpec((tm, GW), row), pl.BlockSpec((tm, GW), row), pl.BlockSpec((tm, GW), row),
                  pl.BlockSpec((tm, GW), lambda b, i: (i, b)),
                  wspec(wo), pl.BlockSpec((1, D_MODEL), const), pl.BlockSpec((1, D_MODEL), const),
                  pl.BlockSpec((1, D_MODEL), const), wspec(w1), wspec(w2)],
        out_specs=pl.BlockSpec((tm, D_MODEL), row),
        out_shape=jax.ShapeDtypeStruct((bsz * seq, D_MODEL), F32),
        compiler_params=_cparams(("parallel", "parallel")),
        name="outffn",
    )(x2, ym, yr, yt, ys_tm, wo, g1, g2, g3, w1, w2)


def _rope_tables(seq):
    half = DH // 2
    inv = ROPE_BASE ** (-jnp.arange(half, dtype=F32) / half)
    ang = jnp.arange(seq, dtype=F32)[:, None] * inv[None, :]
    cos, sin = jnp.cos(ang), jnp.sin(ang)
    zero = jnp.zeros_like(sin)
    tile = lambda first, second: jnp.tile(jnp.concatenate([first, second], axis=-1), (1, NH))
    return tile(cos, cos), tile(-sin, zero), tile(zero, sin)


def _retention_tables():
    L = M_CHUNK
    log_gamma = jnp.log(1.0 - 2.0 ** (-5.0 - jnp.arange(NH, dtype=F32)))
    idx = jnp.arange(L, dtype=F32)
    diff = idx[:, None] - idx[None, :]
    causal = diff >= 0
    decay_mat = jnp.where(causal, jnp.exp(jnp.where(causal, diff, 0.0) * log_gamma[:, None, None]), 0.0)
    lanes = lambda t: jnp.repeat(t.T, DH, axis=1)
    q_decay = lanes(jnp.exp((idx + 1.0) * log_gamma[:, None]))
    k_decay = lanes(jnp.exp((L - 1.0 - idx) * log_gamma[:, None]))
    chunk_decay = jnp.repeat(jnp.exp(L * log_gamma), DH)[None, :]
    return decay_mat, q_decay, k_decay, chunk_decay


def _s5_tables(lam_re, lam_im, log_dt, b_re, b_im, c_re, c_im):
    G, P, C = S5_GROUPS, S5_STATE, S5_GROUP
    dt = jnp.exp(log_dt)[:, None]
    mag = jnp.exp(lam_re * dt)
    abar_re, abar_im = mag * jnp.cos(lam_im * dt), mag * jnp.sin(lam_im * dt)
    den = lam_re * lam_re + lam_im * lam_im
    num_re, num_im = abar_re - 1.0, abar_im
    f_re = (num_re * lam_re + num_im * lam_im) / den
    f_im = (num_im * lam_re - num_re * lam_im) / den
    bbar_re = f_re[..., None] * b_re - f_im[..., None] * b_im
    bbar_im = f_re[..., None] * b_im + f_im[..., None] * b_re
    eye = jnp.eye(G, dtype=F32)
    bd_in = lambda t: jnp.einsum('gpc,gh->gchp', t, eye).reshape(G * C, G * P)
    bd_out = lambda t: jnp.einsum('gcp,gh->gphc', t, eye).reshape(G * P, G * C)
    bmat = jnp.concatenate([bd_in(bbar_re), bd_in(bbar_im)], axis=1)
    cmat = jnp.concatenate([bd_out(c_re), -bd_out(c_im)], axis=0)
    return abar_re.reshape(1, G * P), abar_im.reshape(1, G * P), bmat.astype(BF16), cmat.astype(BF16)


def _pad_lanes(t, width=LANES):
    return jnp.pad(t, ((0, 0), (0, width - t.shape[-1])))


def kernel(x, norm_mix_pre, norm_mix_post, w_in, m_conv_w, m_conv_b, m_i_bias, m_f_bias, m_norm, r_mu, r_w0, r_w_up, r_a0, r_a_up, r_g_up, r_k_k, r_k_a, r_r_k, r_ln_g, r_ln_b, t_gn_g, t_gn_b, s_lam_re, s_lam_im, s_log_dt, s_b_re, s_b_im, s_c_re, s_c_im, s_d, s_w_glu, s_b_glu, w_out, norm_ffn_pre, norm_ffn_post, w_ff1, w_ff2):
    bsz, seq, _ = x.shape
    depth = w_in.shape[0]
    assert seq % TM_PROJ == 0 and seq % M_BLOCK == 0 and seq % R_BLOCK == 0 and seq % S5_T == 0
    assert bsz % SUBLANES == 0
    row = lambda t: t.reshape(1, -1)
    m_cols = 4 * GW + 2 * NH
    o_r = m_cols
    o_t = o_r + 4 * GW
    o_s = o_t + 4 * GW

    cos, sin_a, sin_b = _rope_tables(seq)
    dm, qd, kd, cd = _retention_tables()

    x2 = x.reshape(bsz * seq, D_MODEL)
    for l in range(depth):
        wl = w_in[l]
        wm = wl[:, 0:4 * GW].astype(BF16)
        wg = _pad_lanes(wl[:, 4 * GW:m_cols]).astype(BF16)
        wr = wl[:, o_r:o_t].astype(BF16)
        wt = wl[:, o_t:o_s].astype(BF16)
        ws = wl[:, o_s:].astype(BF16)
        pm, pg, pr, pt, ps_tm = _inproj(x2, row(norm_mix_pre[l]), wm, wg, wr, wt, ws, bsz, seq)

        gate_bias = _pad_lanes(jnp.concatenate([m_i_bias[l], m_f_bias[l]])[None, :])
        y_m = _mlstm(pm, pg, m_conv_w[l], row(m_conv_b[l]), gate_bias, row(m_norm[l]), bsz, seq)

        zeros = jnp.zeros((DH, GW), F32)
        w_comb = jnp.concatenate([jnp.concatenate([r_w_up[l], zeros], axis=1),
                                  jnp.concatenate([zeros, r_a_up[l]], axis=1)], axis=0)
        y_r = _rwkv(pr, row(r_mu[l]), row(r_w0[l]), row(r_a0[l]), w_comb, r_g_up[l], row(r_k_k[l]),
                    row(r_k_a[l]), row(r_r_k[l]), row(r_ln_g[l]), row(r_ln_b[l]), bsz, seq)

        y_t = _retention(pt, cos, sin_a, sin_b, dm, qd, kd, cd, row(t_gn_g[l]), row(t_gn_b[l]), bsz, seq)

        ar, ai, bmat, cmat = _s5_tables(s_lam_re[l], s_lam_im[l], s_log_dt[l], s_b_re[l], s_b_im[l],
                                        s_c_re[l], s_c_im[l])
        y_s_tm = _s5(ps_tm.reshape(seq * bsz, GW), ar, ai, bmat, cmat, row(s_d[l]),
                     s_w_glu[l].astype(BF16), row(s_b_glu[l]), bsz, seq)

        x2 = _outffn(x2, y_m, y_r, y_t, y_s_tm.reshape(seq, bsz * GW), w_out[l].astype(BF16),
                     row(norm_mix_post[l]), row(norm_ffn_pre[l]), row(norm_ffn_post[l]),
                     w_ff1[l].astype(BF16), w_ff2[l].astype(BF16), bsz, seq)
    return x2.reshape(bsz, seq, D_MODEL)
```

```python
import functools

import jax
import jax.numpy as jnp
from jax import lax
from jax.experimental import pallas as pl
from jax.experimental.pallas import tpu as pltpu

F32 = jnp.float32
BF16 = jnp.bfloat16

D_MODEL = 1024
GW = 256
NH = 4
DH = 64
HEAD_SHIFT = 6
D_FF = 4 * D_MODEL
S5_GROUPS = 16
S5_GROUP = 16
S5_STATE = 64
S5_N = S5_GROUPS * S5_STATE
CONV_WIDTH = 4
ROPE_BASE = 10000.0
RMS_EPS = 1e-6
HEAD_LN_EPS = 1e-5
RWKV_LN_EPS = 64e-5

SUBLANES = 8
LANES = 128
M_CHUNK = 128
M_BLOCK = 512
R_CHUNK = 64
R_BLOCK = 512
S5_T = 128
TM_PROJ = 512
VMEM_LIMIT = 56 * 1024 * 1024


def _cparams(sem):
    return pltpu.CompilerParams(dimension_semantics=sem, vmem_limit_bytes=VMEM_LIMIT)


def _dot(a, b, precision=None):
    return jnp.dot(a, b, preferred_element_type=F32, precision=precision)


def _dot_nt(a, b, precision=None):
    return lax.dot_general(a, b, (((1,), (1,)), ((), ())), preferred_element_type=F32, precision=precision)


def _dot_tn(a, b, precision=None):
    return lax.dot_general(a, b, (((0,), (0,)), ((), ())), preferred_element_type=F32, precision=precision)


_FORMS = {"nn": _dot, "nt": _dot_nt, "tn": _dot_tn}


def _bdot(a, b, form="nn"):
    return _FORMS[form](a.astype(BF16), b.astype(BF16))


def _split_bf16(a, terms):
    parts = []
    for _ in range(terms - 1):
        hi = a.astype(BF16)
        parts.append(hi)
        a = a - hi.astype(F32)
    parts.append(a.astype(BF16))
    return parts


def _dot_sel(a, sel, terms):
    sel = sel.astype(BF16)
    return sum(_dot(p, sel) for p in _split_bf16(a, terms))


def _sel_dot(sel, b, terms):
    sel = sel.astype(BF16)
    return sum(_dot(sel, p) for p in _split_bf16(b, terms))


def _sigmoid(x):
    return 1.0 / (1.0 + jnp.exp(-x))


def _softplus(x):
    return jnp.maximum(x, 0.0) + jnp.log(1.0 + jnp.exp(-jnp.abs(x)))


def _head_masks():
    lane = lax.broadcasted_iota(jnp.int32, (1, GW), 1)
    return [jnp.where((lane >> HEAD_SHIFT) == h, 1.0, 0.0).astype(F32) for h in range(NH)]


def _same_head_matrix(scale):
    r = lax.broadcasted_iota(jnp.int32, (GW, GW), 0)
    c = lax.broadcasted_iota(jnp.int32, (GW, GW), 1)
    return jnp.where((r >> HEAD_SHIFT) == (c >> HEAD_SHIFT), scale, 0.0).astype(F32)


def _head_norm(x, eps):
    avg = _same_head_matrix(1.0 / DH)
    mu = _dot_sel(x, avg, 2)
    xc = x - mu
    var = _dot_sel(xc * xc, avg, 2)
    return xc * lax.rsqrt(var + eps)


def _rms(x, g):
    return x * lax.rsqrt(jnp.mean(x * x, axis=-1, keepdims=True) + RMS_EPS) * g


def _inproj_kernel(x_ref, g_ref, wm_ref, wg_ref, wr_ref, wt_ref, ws_ref,
                   pm_ref, pg_ref, pr_ref, pt_ref, ps_ref):
    h = _rms(x_ref[...], g_ref[...]).astype(BF16)
    pm_ref[...] = _dot(h, wm_ref[...])
    pg_ref[...] = _dot(h, wg_ref[...])
    pr_ref[...] = _dot(h, wr_ref[...])
    pt_ref[...] = _dot(h, wt_ref[...])
    ps_ref[...] = _dot(h, ws_ref[...])


def _inproj(x2, g, wm, wg, wr, wt, ws, bsz, seq):
    n = bsz * seq
    tm = TM_PROJ
    nt = seq // tm
    row = lambda b, i: (b * nt + i, 0)
    const = lambda b, i: (0, 0)
    wspec = lambda w: pl.BlockSpec(w.shape, const, pipeline_mode=pl.Buffered(1))
    return pl.pallas_call(
        _inproj_kernel,
        grid=(bsz, nt),
        in_specs=[pl.BlockSpec((tm, D_MODEL), row), pl.BlockSpec((1, D_MODEL), const),
                  wspec(wm), wspec(wg), wspec(wr), wspec(wt), wspec(ws)],
        out_specs=[pl.BlockSpec((tm, 4 * GW), row), pl.BlockSpec((tm, LANES), row),
                   pl.BlockSpec((tm, 4 * GW), row), pl.BlockSpec((tm, 4 * GW), row),
                   pl.BlockSpec((tm, GW), lambda b, i: (i, b))],
        out_shape=[jax.ShapeDtypeStruct((n, 4 * GW), F32), jax.ShapeDtypeStruct((n, LANES), F32),
                   jax.ShapeDtypeStruct((n, 4 * GW), F32), jax.ShapeDtypeStruct((n, 4 * GW), F32),
                   jax.ShapeDtypeStruct((seq, bsz * GW), F32)],
        compiler_params=_cparams(("parallel", "parallel")),
        name="inproj",
    )(x2, g, wm, wg, wr, wt, ws)


def _mlstm_kernel(pm_ref, pg_ref, cw_ref, cb_ref, gb_ref, ng_ref, y_ref, xs_ref, c_ref, n_ref, m_ref):
    MB, L = M_BLOCK, M_CHUNK
    log2_l = L.bit_length() - 1

    @pl.when(pl.program_id(1) == 0)
    def _():
        xs_ref[0:SUBLANES, :] = jnp.zeros((SUBLANES, 2 * GW), F32)
        c_ref[...] = jnp.zeros_like(c_ref)
        n_ref[...] = jnp.zeros_like(n_ref)
        m_ref[...] = jnp.zeros_like(m_ref)

    xqk = pm_ref[:, 0:2 * GW]
    xs_ref[SUBLANES:SUBLANES + MB, :] = xqk
    conv = cb_ref[...]
    for j in range(CONV_WIDTH):
        conv = conv + xs_ref[pl.ds(SUBLANES - (CONV_WIDTH - 1) + j, MB), :] * cw_ref[j:j + 1, :]
    xs_ref[0:SUBLANES, :] = xqk[MB - SUBLANES:MB, :]
    qk = conv * _sigmoid(conv)
    q_all = qk[:, 0:GW]
    k_all = qk[:, GW:2 * GW] * (DH ** -0.5)
    v_all = pm_ref[:, 2 * GW:3 * GW]

    lane = lax.broadcasted_iota(jnp.int32, (1, LANES), 1)
    g2 = pg_ref[...] + gb_ref[...]
    lf = jnp.where((lane >= NH) & (lane < 2 * NH), -_softplus(-g2), 0.0)
    rb = lax.broadcasted_iota(jnp.int32, (MB, MB), 0)
    cb = lax.broadcasted_iota(jnp.int32, (MB, MB), 1)
    sel = jnp.where(((rb >> log2_l) == (cb >> log2_l)) & (rb >= cb), 1.0, 0.0)
    gcum_all = _sel_dot(sel, lf, 3)
    colmat_all = jnp.where(lane < NH, g2, gcum_all)
    rowmat_all = colmat_all.T

    ri = lax.broadcasted_iota(jnp.int32, (L, L), 0)
    ci = lax.broadcasted_iota(jnp.int32, (L, L), 1)
    causal = ri >= ci
    masks = _head_masks()
    same_head = _same_head_matrix(1.0)
    c_st = c_ref[...]
    n_st = n_ref[...]
    m_st = m_ref[...]
    hs = []
    for c in range(MB // L):
        rows = slice(c * L, (c + 1) * L)
        q, k, v = q_all[rows], k_all[rows], v_all[rows]
        colmat = colmat_all[rows]
        rowmat = rowmat_all[:, rows]
        q_c = _bdot(q, c_st)
        kb = k.astype(BF16)
        vb = v.astype(BF16)

        num = jnp.zeros((L, GW), F32)
        den = jnp.zeros((L, GW), F32)
        wst = jnp.zeros((L, GW), F32)
        cs_l = jnp.zeros((1, GW), F32)
        mn_l = jnp.zeros((1, GW), F32)
        for h in range(NH):
            mh = masks[h]
            gcol = colmat[:, NH + h:NH + h + 1]
            icol = colmat[:, h:h + 1]
            grow = rowmat[NH + h:NH + h + 1, :]
            irow = rowmat[h:h + 1, :]
            mprev = m_st[:, h * DH:h * DH + 1]
            d_intra = jnp.where(causal, gcol - grow + irow, -jnp.inf)
            d_inter = gcol + mprev
            m_row = jnp.maximum(d_inter, jnp.max(d_intra, axis=-1, keepdims=True))
            sc = _dot_nt((q * mh).astype(BF16), kb)
            s_qk = sc * jnp.exp(d_intra - m_row)
            w_inter = jnp.exp(d_inter - m_row)
            num = num + mh * (_dot(s_qk.astype(BF16), vb) + w_inter * q_c)
            den_h = (jnp.sum(s_qk, axis=-1, keepdims=True)
                     + w_inter * jnp.sum(q * (mh * n_st), axis=-1, keepdims=True))
            den = den + mh * jnp.maximum(jnp.abs(den_h), jnp.exp(-m_row))
            g_last = colmat[L - 1:L, NH + h:NH + h + 1]
            d_state = g_last - gcol + icol
            m_new = jnp.maximum(g_last + mprev, jnp.max(d_state, axis=0, keepdims=True))
            wst = wst + mh * jnp.exp(d_state - m_new)
            cs_l = cs_l + mh * jnp.exp(g_last + mprev - m_new)
            mn_l = mn_l + mh * m_new

        hs.append(num / den)
        kw = k * wst
        c_st = cs_l * c_st + same_head * _dot_tn(kw.astype(BF16), vb)
        n_st = cs_l * n_st + jnp.sum(kw, axis=0, keepdims=True)
        m_st = mn_l
    c_ref[...] = c_st
    n_ref[...] = n_st
    m_ref[...] = m_st
    og = _sigmoid(pm_ref[:, 3 * GW:4 * GW])
    y_ref[...] = _head_norm(jnp.concatenate(hs, axis=0), HEAD_LN_EPS) * ng_ref[...] * og


def _mlstm(pm, pg, cw, cb, gb, ng, bsz, seq):
    MB = M_BLOCK
    nb = seq // MB
    row = lambda b, i: (b * nb + i, 0)
    const = lambda b, i: (0, 0)
    return pl.pallas_call(
        _mlstm_kernel,
        grid=(bsz, nb),
        in_specs=[pl.BlockSpec((MB, 4 * GW), row), pl.BlockSpec((MB, LANES), row),
                  pl.BlockSpec(cw.shape, const), pl.BlockSpec(cb.shape, const),
                  pl.BlockSpec(gb.shape, const), pl.BlockSpec(ng.shape, const)],
        out_specs=pl.BlockSpec((MB, GW), row),
        out_shape=jax.ShapeDtypeStruct((bsz * seq, GW), F32),
        scratch_shapes=[pltpu.VMEM((SUBLANES + MB, 2 * GW), F32), pltpu.VMEM((GW, GW), F32),
                        pltpu.VMEM((1, GW), F32), pltpu.VMEM((1, GW), F32)],
        compiler_params=_cparams(("parallel", "arbitrary")),
        name="mlstm",
    )(pm, pg, cw, cb, gb, ng)


def _to_blocks(x, rows):
    return jnp.stack([x[c * rows:(c + 1) * rows, h * DH:(h + 1) * DH]
                      for c in range(x.shape[0] // rows) for h in range(NH)], axis=0)


def _from_heads(x):
    return jnp.concatenate([x[h] for h in range(NH)], axis=-1)


def _bmm(a, b):
    return jnp.einsum('hij,hjk->hik', a.astype(BF16), b.astype(BF16), preferred_element_type=F32)


def _bmm_nt(a, b):
    return jnp.einsum('hik,hjk->hij', a.astype(BF16), b.astype(BF16), preferred_element_type=F32)


def _bmm_tn(a, b):
    return jnp.einsum('hki,hkj->hij', a.astype(BF16), b.astype(BF16), preferred_element_type=F32)


def _unit_lower_inverse(a, ri, ci, log2_n):
    same = lambda s: (ri >> s) == (ci >> s)
    eye = jnp.where(ri == ci, 1.0, 0.0).astype(F32)
    a8 = jnp.where(same(3), a, 0.0)
    a8_2 = _bmm(a8, a8)
    a8_4 = _bmm(a8_2, a8_2)
    inv = _bmm(_bmm(eye - a8, eye + a8_2), eye + a8_4)
    for s in range(3, log2_n):
        off = jnp.where(same(s + 1) & jnp.logical_not(same(s)), a, 0.0)
        inv = inv - _bmm(inv, _bmm(off, inv))
    return inv


def _rwkv_kernel(pr_ref, mu_ref, w0_ref, a0_ref, wc_ref, gup_ref, kk_ref, ka_ref, rk_ref, lg_ref, lb_ref,
                 y_ref, xs_ref, st_ref):
    RB, L = R_BLOCK, R_CHUNK
    log2_l = L.bit_length() - 1

    @pl.when(pl.program_id(1) == 0)
    def _():
        xs_ref[0:SUBLANES, :] = jnp.zeros((SUBLANES, 4 * GW), F32)
        st_ref[...] = jnp.zeros_like(st_ref)

    pr = pr_ref[...]
    xs_ref[SUBLANES:SUBLANES + RB, :] = pr
    shifted = xs_ref[pl.ds(SUBLANES - 1, RB), :]
    xs_ref[0:SUBLANES, :] = pr[RB - SUBLANES:RB, :]
    prm = pr + mu_ref[...] * (shifted - pr)

    r = prm[:, 0:GW]
    k = prm[:, GW:2 * GW]
    v = prm[:, 2 * GW:3 * GW]
    z = prm[:, 3 * GW:3 * GW + LANES]
    lane = lax.broadcasted_iota(jnp.int32, (1, LANES), 1)
    wa = _bdot(jnp.where(lane < DH, jnp.tanh(z), z), wc_ref[...])
    w_log = -_softplus(-(w0_ref[...] + wa[:, 0:GW])) - 0.5
    logw = -jnp.exp(w_log)
    a = _sigmoid(a0_ref[...] + wa[:, GW:2 * GW])
    g = _bdot(_sigmoid(prm[:, 3 * GW + LANES:4 * GW]), gup_ref[...])

    ones_bd = _same_head_matrix(1.0)
    kk = k * kk_ref[...]
    kk = kk / jnp.maximum(jnp.sqrt(_dot_sel(kk * kk, ones_bd, 2)), 1e-12)
    k2 = k * (1.0 + (a - 1.0) * ka_ref[...])
    bonus = _dot_sel(r * k2 * rk_ref[...], ones_bd, 2) * v
    b = kk * a

    ri = lax.broadcasted_iota(jnp.int32, (2 * RB, RB), 0)
    ci = lax.broadcasted_iota(jnp.int32, (2 * RB, RB), 1)
    rr = ri & (RB - 1)
    sel = jnp.where(((rr >> log2_l) == (ci >> log2_l)) & ((ri >= RB) | (rr >= ci)), 1.0, 0.0)
    cums = _sel_dot(sel, logw, 3)
    cum = cums[0:RB]
    tot = cums[RB:2 * RB]
    p_inv = jnp.exp(-cum)
    p_end = jnp.exp(tot - cum)
    nc = RB // L
    kkd = _to_blocks(kk * jnp.exp(cum - logw), L)
    rd = _to_blocks(r * jnp.exp(cum), L)
    bi = _to_blocks(b * p_inv, L)
    ki = _to_blocks(k2 * p_inv, L)
    be = _to_blocks(b * p_end, L)
    ke = _to_blocks(k2 * p_end, L)
    vh = _to_blocks(v, L)
    p_tot = _to_blocks(jnp.exp(tot), L)[:, 0:1, :]

    rl = lax.broadcasted_iota(jnp.int32, (L, L), 0)
    cl = lax.broadcasted_iota(jnp.int32, (L, L), 1)
    strict = rl > cl
    incl = rl >= cl

    kr = jnp.concatenate([kkd, rd], axis=1)
    xb = _bmm_nt(kr, bi)
    xk = _bmm_nt(kr, ki)
    a_b = jnp.where(strict, xb[:, 0:L], 0.0)
    b_b = jnp.where(incl, xb[:, L:2 * L], 0.0)
    a_k = jnp.where(strict, xk[:, 0:L], 0.0)
    b_k = jnp.where(incl, xk[:, L:2 * L], 0.0)
    t_inv = _unit_lower_inverse(a_b, rl, cl, log2_l)
    w_q = _bmm(t_inv, kkd)
    u_0 = _bmm(t_inv, _bmm(a_k, vh))
    q_eff = rd - _bmm(b_b, w_q)
    y_0 = _bmm(b_k, vh) - _bmm(b_b, u_0)
    m_t = jnp.where(rl == cl, p_tot, 0.0) - _bmm_tn(be, w_q)
    n_t = _bmm_tn(ke, vh) - _bmm_tn(be, u_0)

    st = st_ref[...]
    ys = []
    for c in range(nc):
        blk = slice(c * NH, (c + 1) * NH)
        ys.append(_from_heads(_bmm(q_eff[blk], st) + y_0[blk]))
        st = _bmm(m_t[blk], st) + n_t[blk]
    st_ref[...] = st

    y = _head_norm(jnp.concatenate(ys, axis=0), RWKV_LN_EPS) * lg_ref[...] + lb_ref[...]
    y_ref[...] = (y + bonus) * g


def _rwkv(pr, mu, w0, a0, wc, gup, kk, ka, rk, lg, lb, bsz, seq):
    RB = R_BLOCK
    nb = seq // RB
    row = lambda b, i: (b * nb + i, 0)
    const = lambda b, i: (0, 0)
    params = (mu, w0, a0, wc, gup, kk, ka, rk, lg, lb)
    return pl.pallas_call(
        _rwkv_kernel,
        grid=(bsz, nb),
        in_specs=[pl.BlockSpec((RB, 4 * GW), row)] + [pl.BlockSpec(p.shape, const) for p in params],
        out_specs=pl.BlockSpec((RB, GW), row),
        out_shape=jax.ShapeDtypeStruct((bsz * seq, GW), F32),
        scratch_shapes=[pltpu.VMEM((SUBLANES + RB, 4 * GW), F32), pltpu.VMEM((NH, DH, DH), F32)],
        compiler_params=_cparams(("parallel", "arbitrary")),
        name="rwkv",
    )(pr, *params)


def _ret_kernel(pt_ref, cos_ref, sa_ref, sb_ref, dm_ref, qd_ref, kd_ref, cd_ref, gg_ref, gb_ref,
                y_ref, r_ref):
    MB, L = M_BLOCK, M_CHUNK

    @pl.when(pl.program_id(1) == 0)
    def _():
        r_ref[...] = jnp.zeros_like(r_ref)

    cos = cos_ref[...]
    sa = sa_ref[...]
    sb = sb_ref[...]

    def rope(t):
        return t * cos + pltpu.roll(t, GW - DH // 2, 1) * sa + pltpu.roll(t, DH // 2, 1) * sb

    q_all = rope(pt_ref[:, 0:GW])
    k_all = rope(pt_ref[:, GW:2 * GW]) * (DH ** -0.5)
    v_all = pt_ref[:, 2 * GW:3 * GW]

    masks = _head_masks()
    same_head = _same_head_matrix(1.0)
    r_st = r_ref[...]
    os_ = []
    for c in range(MB // L):
        rows = slice(c * L, (c + 1) * L)
        q, k = q_all[rows], k_all[rows]
        kb = k.astype(BF16)
        vb = v_all[rows].astype(BF16)
        o = _bdot(q, r_st) * qd_ref[...]
        for h in range(NH):
            sc = _dot_nt((q * masks[h]).astype(BF16), kb) * dm_ref[h]
            o = o + masks[h] * _dot(sc.astype(BF16), vb)
        os_.append(o)
        r_st = r_st * cd_ref[...] + same_head * _dot_tn((k * kd_ref[...]).astype(BF16), vb)
    r_ref[...] = r_st
    gate = pt_ref[:, 3 * GW:4 * GW]
    y_ref[...] = gate * _sigmoid(gate) * (_head_norm(jnp.concatenate(os_, axis=0), HEAD_LN_EPS) * gg_ref[...]
                                          + gb_ref[...])


def _retention(pt, cos, sa, sb, dm, qd, kd, cd, gg, gb, bsz, seq):
    MB = M_BLOCK
    nb = seq // MB
    row = lambda b, i: (b * nb + i, 0)
    pos = lambda b, i: (i, 0)
    const2 = lambda b, i: (0, 0)
    const3 = lambda b, i: (0, 0, 0)
    return pl.pallas_call(
        _ret_kernel,
        grid=(bsz, nb),
        in_specs=[pl.BlockSpec((MB, 4 * GW), row),
                  pl.BlockSpec((MB, GW), pos), pl.BlockSpec((MB, GW), pos), pl.BlockSpec((MB, GW), pos),
                  pl.BlockSpec(dm.shape, const3), pl.BlockSpec(qd.shape, const2), pl.BlockSpec(kd.shape, const2),
                  pl.BlockSpec(cd.shape, const2), pl.BlockSpec(gg.shape, const2), pl.BlockSpec(gb.shape, const2)],
        out_specs=pl.BlockSpec((MB, GW), row),
        out_shape=jax.ShapeDtypeStruct((bsz * seq, GW), F32),
        scratch_shapes=[pltpu.VMEM((GW, GW), F32)],
        compiler_params=_cparams(("parallel", "arbitrary")),
        name="retention",
    )(pt, cos, sa, sb, dm, qd, kd, cd, gg, gb)


def _gelu_tanh(x):
    return 0.5 * x * (1.0 + jnp.tanh(0.7978845608028654 * (x + 0.044715 * (x * x * x))))


def _s5_kernel(u_ref, ar_ref, ai_ref, bm_ref, cm_ref, d_ref, wg_ref, bg_ref, y_ref, bu_ref, x_ref, *, nb):
    T = u_ref.shape[0] // nb

    @pl.when(pl.program_id(0) == 0)
    def _():
        x_ref[...] = jnp.zeros_like(x_ref)

    u = u_ref[...]
    bu_ref[...] = _dot(u.astype(BF16), bm_ref[...])
    ar = jnp.broadcast_to(ar_ref[...], (nb, S5_N))
    ai = jnp.broadcast_to(ai_ref[...], (nb, S5_N))

    def step(t, carry):
        xr, xi = carry
        r0 = pl.multiple_of(t * nb, nb)
        nr = ar * xr - ai * xi + bu_ref[pl.ds(r0, nb), 0:S5_N]
        ni = ar * xi + ai * xr + bu_ref[pl.ds(r0, nb), S5_N:2 * S5_N]
        bu_ref[pl.ds(r0, nb), 0:S5_N] = nr
        bu_ref[pl.ds(r0, nb), S5_N:2 * S5_N] = ni
        return nr, ni

    xr, xi = lax.fori_loop(0, T, step, (x_ref[:, 0:S5_N], x_ref[:, S5_N:2 * S5_N]), unroll=4)
    x_ref[:, 0:S5_N] = xr
    x_ref[:, S5_N:2 * S5_N] = xi

    y = _dot(bu_ref[...].astype(BF16), cm_ref[...])
    y = _gelu_tanh(y + d_ref[...] * u)
    z = _dot(y.astype(BF16), wg_ref[...]) + bg_ref[...]
    y_ref[...] = z[:, 0:GW] * _sigmoid(z[:, GW:2 * GW])


def _s5(ps_tm, ar, ai, bm, cm, d, wg, bg, bsz, seq):
    T = S5_T
    rows = T * bsz
    const = lambda i: (0, 0)
    params = (ar, ai, bm, cm, d, wg, bg)
    return pl.pallas_call(
        functools.partial(_s5_kernel, nb=bsz),
        grid=(seq // T,),
        in_specs=[pl.BlockSpec((rows, GW), lambda i: (i, 0))] + [pl.BlockSpec(p.shape, const) for p in params],
        out_specs=pl.BlockSpec((rows, GW), lambda i: (i, 0)),
        out_shape=jax.ShapeDtypeStruct((seq * bsz, GW), F32),
        scratch_shapes=[pltpu.VMEM((rows, 2 * S5_N), F32), pltpu.VMEM((bsz, 2 * S5_N), F32)],
        compiler_params=_cparams(("arbitrary",)),
        name="s5",
    )(ps_tm, *params)


def _outffn_kernel(x_ref, ym_ref, yr_ref, yt_ref, ys_ref, wo_ref, g1_ref, g2_ref, g3_ref, w1_ref, w2_ref, o_ref):
    y = _dot(ym_ref[...].astype(BF16), wo_ref[0:GW, :])
    y = y + _dot(yr_ref[...].astype(BF16), wo_ref[GW:2 * GW, :])
    y = y + _dot(yt_ref[...].astype(BF16), wo_ref[2 * GW:3 * GW, :])
    y = y + _dot(ys_ref[...].astype(BF16), wo_ref[3 * GW:4 * GW, :])
    x1 = x_ref[...] + _rms(y, g1_ref[...])
    h = _rms(x1, g2_ref[...]).astype(BF16)
    f = jnp.zeros_like(x1)
    for j in range(D_FF // D_MODEL):
        a = jnp.maximum(_dot(h, w1_ref[:, j * D_MODEL:(j + 1) * D_MODEL]), 0.0)
        f = f + _dot((a * a).astype(BF16), w2_ref[j * D_MODEL:(j + 1) * D_MODEL, :])
    o_ref[...] = x1 + _rms(f, g3_ref[...])


def _outffn(x2, ym, yr, yt, ys_tm, wo, g1, g2, g3, w1, w2, bsz, seq):
    tm = TM_PROJ
    nt = seq // tm
    row = lambda b, i: (b * nt + i, 0)
    const = lambda b, i: (0, 0)
    wspec = lambda w: pl.BlockSpec(w.shape, const, pipeline_mode=pl.Buffered(1))
    return pl.pallas_call(
        _outffn_kernel,
        grid=(bsz, nt),
        in_specs=[pl.BlockSpec((tm, D_MODEL), row),
                  pl.BlockSpec((tm, GW), row), pl.BlockSpec((tm, GW), row), pl.BlockSpec((tm, GW), row),
                  pl.BlockSpec((tm, GW), lambda b, i: (i, b)),
                  wspec(wo), pl.BlockSpec((1, D_MODEL), const), pl.BlockSpec((1, D_MODEL), const),
                  pl.BlockSpec((1, D_MODEL), const), wspec(w1), wspec(w2)],
        out_specs=pl.BlockSpec((tm, D_MODEL), row),
        out_shape=jax.ShapeDtypeStruct((bsz * seq, D_MODEL), F32),
        compiler_params=_cparams(("parallel", "parallel")),
        name="outffn",
    )(x2, ym, yr, yt, ys_tm, wo, g1, g2, g3, w1, w2)


def _rope_tables(seq):
    half = DH // 2
    inv = ROPE_BASE ** (-jnp.arange(half, dtype=F32) / half)
    ang = jnp.arange(seq, dtype=F32)[:, None] * inv[None, :]
    cos, sin = jnp.cos(ang), jnp.sin(ang)
    zero = jnp.zeros_like(sin)
    tile = lambda first, second: jnp.tile(jnp.concatenate([first, second], axis=-1), (1, NH))
    return tile(cos, cos), tile(-sin, zero), tile(zero, sin)


def _retention_tables():
    L = M_CHUNK
    log_gamma = jnp.log(1.0 - 2.0 ** (-5.0 - jnp.arange(NH, dtype=F32)))
    idx = jnp.arange(L, dtype=F32)
    diff = idx[:, None] - idx[None, :]
    causal = diff >= 0
    decay_mat = jnp.where(causal, jnp.exp(jnp.where(causal, diff, 0.0) * log_gamma[:, None, None]), 0.0)
    lanes = lambda t: jnp.repeat(t.T, DH, axis=1)
    q_decay = lanes(jnp.exp((idx + 1.0) * log_gamma[:, None]))
    k_decay = lanes(jnp.exp((L - 1.0 - idx) * log_gamma[:, None]))
    chunk_decay = jnp.repeat(jnp.exp(L * log_gamma), DH)[None, :]
    return decay_mat, q_decay, k_decay, chunk_decay


def _s5_tables(lam_re, lam_im, log_dt, b_re, b_im, c_re, c_im):
    G, P, C = S5_GROUPS, S5_STATE, S5_GROUP
    dt = jnp.exp(log_dt)[:, None]
    mag = jnp.exp(lam_re * dt)
    abar_re, abar_im = mag * jnp.cos(lam_im * dt), mag * jnp.sin(lam_im * dt)
    den = lam_re * lam_re + lam_im * lam_im
    num_re, num_im = abar_re - 1.0, abar_im
    f_re = (num_re * lam_re + num_im * lam_im) / den
    f_im = (num_im * lam_re - num_re * lam_im) / den
    bbar_re = f_re[..., None] * b_re - f_im[..., None] * b_im
    bbar_im = f_re[..., None] * b_im + f_im[..., None] * b_re
    eye = jnp.eye(G, dtype=F32)
    bd_in = lambda t: jnp.einsum('gpc,gh->gchp', t, eye).reshape(G * C, G * P)
    bd_out = lambda t: jnp.einsum('gcp,gh->gphc', t, eye).reshape(G * P, G * C)
    bmat = jnp.concatenate([bd_in(bbar_re), bd_in(bbar_im)], axis=1)
    cmat = jnp.concatenate([bd_out(c_re), -bd_out(c_im)], axis=0)
    return abar_re.reshape(1, G * P), abar_im.reshape(1, G * P), bmat.astype(BF16), cmat.astype(BF16)


def _pad_lanes(t, width=LANES):
    return jnp.pad(t, ((0, 0), (0, width - t.shape[-1])))


def kernel(x, norm_mix_pre, norm_mix_post, w_in, m_conv_w, m_conv_b, m_i_bias, m_f_bias, m_norm, r_mu, r_w0, r_w_up, r_a0, r_a_up, r_g_up, r_k_k, r_k_a, r_r_k, r_ln_g, r_ln_b, t_gn_g, t_gn_b, s_lam_re, s_lam_im, s_log_dt, s_b_re, s_b_im, s_c_re, s_c_im, s_d, s_w_glu, s_b_glu, w_out, norm_ffn_pre, norm_ffn_post, w_ff1, w_ff2):
    bsz, seq, _ = x.shape
    depth = w_in.shape[0]
    assert seq % TM_PROJ == 0 and seq % M_BLOCK == 0 and seq % R_BLOCK == 0 and seq % S5_T == 0
    assert bsz % SUBLANES == 0
    row = lambda t: t.reshape(1, -1)
    m_cols = 4 * GW + 2 * NH
    o_r = m_cols
    o_t = o_r + 4 * GW
    o_s = o_t + 4 * GW

    cos, sin_a, sin_b = _rope_tables(seq)
    dm, qd, kd, cd = _retention_tables()

    x2 = x.reshape(bsz * seq, D_MODEL)
    for l in range(depth):
        wl = w_in[l]
        wm = wl[:, 0:4 * GW].astype(BF16)
        wg = _pad_lanes(wl[:, 4 * GW:m_cols]).astype(BF16)
        wr = wl[:, o_r:o_t].astype(BF16)
        wt = wl[:, o_t:o_s].astype(BF16)
        ws = wl[:, o_s:].astype(BF16)
        pm, pg, pr, pt, ps_tm = _inproj(x2, row(norm_mix_pre[l]), wm, wg, wr, wt, ws, bsz, seq)

        gate_bias = _pad_lanes(jnp.concatenate([m_i_bias[l], m_f_bias[l]])[None, :])
        y_m = _mlstm(pm, pg, m_conv_w[l], row(m_conv_b[l]), gate_bias, row(m_norm[l]), bsz, seq)

        zeros = jnp.zeros((DH, GW), F32)
        w_comb = jnp.concatenate([jnp.concatenate([r_w_up[l], zeros], axis=1),
                                  jnp.concatenate([zeros, r_a_up[l]], axis=1)], axis=0)
        y_r = _rwkv(pr, row(r_mu[l]), row(r_w0[l]), row(r_a0[l]), w_comb, r_g_up[l], row(r_k_k[l]),
                    row(r_k_a[l]), row(r_r_k[l]), row(r_ln_g[l]), row(r_ln_b[l]), bsz, seq)

        y_t = _retention(pt, cos, sin_a, sin_b, dm, qd, kd, cd, row(t_gn_g[l]), row(t_gn_b[l]), bsz, seq)

        ar, ai, bmat, cmat = _s5_tables(s_lam_re[l], s_lam_im[l], s_log_dt[l], s_b_re[l], s_b_im[l],
                                        s_c_re[l], s_c_im[l])
        y_s_tm = _s5(ps_tm.reshape(seq * bsz, GW), ar, ai, bmat, cmat, row(s_d[l]),
                     s_w_glu[l].astype(BF16), row(s_b_glu[l]), bsz, seq)

        x2 = _outffn(x2, y_m, y_r, y_t, y_s_tm.reshape(seq, bsz * GW), w_out[l].astype(BF16),
                     row(norm_mix_post[l]), row(norm_ffn_pre[l]), row(norm_ffn_post[l]),
                     w_ff1[l].astype(BF16), w_ff2[l].astype(BF16), bsz, seq)
    return x2.reshape(bsz, seq, D_MODEL)
```

```python
import functools

import jax
import jax.numpy as jnp
from jax import lax
from jax.experimental import pallas as pl
from jax.experimental.pallas import tpu as pltpu

F32 = jnp.float32
BF16 = jnp.bfloat16

D_MODEL = 1024
GW = 256
NH = 4
DH = 64
HEAD_SHIFT = 6
D_FF = 4 * D_MODEL
S5_GROUPS = 16
S5_GROUP = 16
S5_STATE = 64
S5_N = S5_GROUPS * S5_STATE
CONV_WIDTH = 4
ROPE_BASE = 10000.0
RMS_EPS = 1e-6
HEAD_LN_EPS = 1e-5
RWKV_LN_EPS = 64e-5

SUBLANES = 8
LANES = 128
M_CHUNK = 128
M_BLOCK = 512
R_CHUNK = 64
R_BLOCK = 512
S5_T = 128
TM_PROJ = 512
VMEM_LIMIT = 56 * 1024 * 1024


def _cparams(sem):
    return pltpu.CompilerParams(dimension_semantics=sem, vmem_limit_bytes=VMEM_LIMIT)


def _dot(a, b, precision=None):
    return jnp.dot(a, b, preferred_element_type=F32, precision=precision)


def _dot_nt(a, b, precision=None):
    return lax.dot_general(a, b, (((1,), (1,)), ((), ())), preferred_element_type=F32, precision=precision)


def _dot_tn(a, b, precision=None):
    return lax.dot_general(a, b, (((0,), (0,)), ((), ())), preferred_element_type=F32, precision=precision)


_FORMS = {"nn": _dot, "nt": _dot_nt, "tn": _dot_tn}


def _bdot(a, b, form="nn"):
    return _FORMS[form](a.astype(BF16), b.astype(BF16))


def _split_bf16(a, terms):
    parts = []
    for _ in range(terms - 1):
        hi = a.astype(BF16)
        parts.append(hi)
        a = a - hi.astype(F32)
    parts.append(a.astype(BF16))
    return parts


def _sel_dot(sel, b, terms):
    sel = sel.astype(BF16)
    return sum(_dot(sel, p) for p in _split_bf16(b, terms))


def _sigmoid(x):
    return 1.0 / (1.0 + jnp.exp(-x))


def _softplus(x):
    return jnp.maximum(x, 0.0) + jnp.log(1.0 + jnp.exp(-jnp.abs(x)))


def _head_masks():
    lane = lax.broadcasted_iota(jnp.int32, (1, GW), 1)
    return [jnp.where((lane >> HEAD_SHIFT) == h, 1.0, 0.0).astype(F32) for h in range(NH)]


def _same_head_matrix(scale):
    r = lax.broadcasted_iota(jnp.int32, (GW, GW), 0)
    c = lax.broadcasted_iota(jnp.int32, (GW, GW), 1)
    return jnp.where((r >> HEAD_SHIFT) == (c >> HEAD_SHIFT), scale, 0.0).astype(F32)


def _head_norm(x, eps):
    avg = _same_head_matrix(1.0 / DH)
    mu = _bdot(x, avg)
    xc = x - mu
    var = _bdot(xc * xc, avg)
    return xc * lax.rsqrt(var + eps)


def _rms(x, g):
    return x * lax.rsqrt(jnp.mean(x * x, axis=-1, keepdims=True) + RMS_EPS) * g


def _inproj_kernel(x_ref, g_ref, wm_ref, wg_ref, wr_ref, wt_ref, ws_ref,
                   pm_ref, pg_ref, pr_ref, pt_ref, ps_ref):
    h = _rms(x_ref[...], g_ref[...]).astype(BF16)
    pm_ref[...] = _dot(h, wm_ref[...])
    pg_ref[...] = _dot(h, wg_ref[...])
    pr_ref[...] = _dot(h, wr_ref[...])
    pt_ref[...] = _dot(h, wt_ref[...])
    ps_ref[...] = _dot(h, ws_ref[...])


def _inproj(x2, g, wm, wg, wr, wt, ws, bsz, seq):
    n = bsz * seq
    tm = TM_PROJ
    nt = seq // tm
    row = lambda b, i: (b * nt + i, 0)
    const = lambda b, i: (0, 0)
    wspec = lambda w: pl.BlockSpec(w.shape, const, pipeline_mode=pl.Buffered(1))
    return pl.pallas_call(
        _inproj_kernel,
        grid=(bsz, nt),
        in_specs=[pl.BlockSpec((tm, D_MODEL), row), pl.BlockSpec((1, D_MODEL), const),
                  wspec(wm), wspec(wg), wspec(wr), wspec(wt), wspec(ws)],
        out_specs=[pl.BlockSpec((tm, 4 * GW), row), pl.BlockSpec((tm, LANES), row),
                   pl.BlockSpec((tm, 4 * GW), row), pl.BlockSpec((tm, 4 * GW), row),
                   pl.BlockSpec((tm, GW), lambda b, i: (i, b))],
        out_shape=[jax.ShapeDtypeStruct((n, 4 * GW), F32), jax.ShapeDtypeStruct((n, LANES), F32),
                   jax.ShapeDtypeStruct((n, 4 * GW), F32), jax.ShapeDtypeStruct((n, 4 * GW), F32),
                   jax.ShapeDtypeStruct((seq, bsz * GW), F32)],
        compiler_params=_cparams(("parallel", "parallel")),
        name="inproj",
    )(x2, g, wm, wg, wr, wt, ws)


def _mlstm_kernel(pm_ref, pg_ref, cw_ref, cb_ref, gb_ref, ng_ref, y_ref, xs_ref, ct_ref, n_ref, m_ref):
    MB, L = M_BLOCK, M_CHUNK
    nc = MB // L
    log2_l = L.bit_length() - 1

    @pl.when(pl.program_id(1) == 0)
    def _():
        xs_ref[0:SUBLANES, :] = jnp.zeros((SUBLANES, 2 * GW), F32)
        ct_ref[...] = jnp.zeros_like(ct_ref)
        n_ref[...] = jnp.zeros_like(n_ref)
        m_ref[...] = jnp.zeros_like(m_ref)

    xqk = pm_ref[:, 0:2 * GW]
    xs_ref[SUBLANES:SUBLANES + MB, :] = xqk
    conv = cb_ref[...]
    for j in range(CONV_WIDTH):
        conv = conv + xs_ref[pl.ds(SUBLANES - (CONV_WIDTH - 1) + j, MB), :] * cw_ref[j:j + 1, :]
    xs_ref[0:SUBLANES, :] = xqk[MB - SUBLANES:MB, :]
    qk = conv * _sigmoid(conv)
    q_t = qk[:, 0:GW].T
    k_all = qk[:, GW:2 * GW] * (DH ** -0.5)
    v_t = pm_ref[:, 2 * GW:3 * GW].T

    lane = lax.broadcasted_iota(jnp.int32, (1, LANES), 1)
    g2 = pg_ref[...] + gb_ref[...]
    lf = jnp.where((lane >= NH) & (lane < 2 * NH), -_softplus(-g2), 0.0)
    rb = lax.broadcasted_iota(jnp.int32, (MB, MB), 0)
    cb = lax.broadcasted_iota(jnp.int32, (MB, MB), 1)
    sel = jnp.where(((rb >> log2_l) == (cb >> log2_l)) & (rb >= cb), 1.0, 0.0)
    gcum = _sel_dot(sel, lf, 2)
    colmat = jnp.where(lane < NH, g2, gcum)
    dcol = colmat - pltpu.roll(colmat, LANES - NH, 1)
    rowmat = colmat.T
    g_rows = rowmat[NH:2 * NH]
    c_rows = rowmat[0:NH] - g_rows

    si = lax.broadcasted_iota(jnp.int32, (L, L), 0)
    ji = lax.broadcasted_iota(jnp.int32, (L, L), 1)
    causal = si <= ji
    blocks = [(c, h) for c in range(nc) for h in range(NH)]
    tsl = lambda c: slice(c * L, (c + 1) * L)
    hsl = lambda h: slice(h * DH, (h + 1) * DH)
    k_b = {b: k_all[tsl(b[0]), hsl(b[1])].astype(BF16) for b in blocks}
    q_b = {b: q_t[hsl(b[1]), tsl(b[0])].astype(BF16) for b in blocks}
    v_b = {b: v_t[hsl(b[1]), tsl(b[0])] for b in blocks}
    g_row = {b: g_rows[b[1]:b[1] + 1, tsl(b[0])] for b in blocks}

    m_loc, num_loc, den_loc = {}, {}, {}
    for b in blocks:
        c, h = b
        d_t = jnp.where(causal, g_row[b] + dcol[tsl(c), h:h + 1], -jnp.inf)
        m_loc[b] = jnp.max(d_t, axis=0, keepdims=True)
        s_t = _dot(k_b[b], q_b[b]) * jnp.exp(d_t - m_loc[b])
        num_loc[b] = _dot(v_b[b].astype(BF16), s_t.astype(BF16))
        den_loc[b] = jnp.sum(s_t, axis=0, keepdims=True)

    h_rows = []
    for c in range(nc):
        h_cols = []
        for h in range(NH):
            b = (c, h)
            ct = ct_ref[h]
            n8 = n_ref[h]
            m_prev = m_ref[h][:, 0:1]
            d_inter = g_row[b] + m_prev
            m_row = jnp.maximum(d_inter, m_loc[b])
            e_loc = jnp.exp(m_loc[b] - m_row)
            w_inter = jnp.exp(d_inter - m_row)
            num = e_loc * num_loc[b] + w_inter * _dot(ct.astype(BF16), q_b[b])
            den = e_loc * den_loc[b] + w_inter * _dot(n8.astype(BF16), q_b[b])[0:1]
            h_cols.append(num / jnp.maximum(jnp.abs(den), jnp.exp(-m_row)))

            g_last = g_row[b][:, L - 1:L]
            d_state = g_last + c_rows[h:h + 1, tsl(c)]
            m_new = jnp.maximum(g_last + m_prev, jnp.max(d_state, axis=1, keepdims=True))
            w_row = jnp.exp(d_state - m_new)
            cs = jnp.exp(g_last + m_prev - m_new)
            ct_ref[h] = cs * ct + _dot((v_b[b] * w_row).astype(BF16), k_b[b])
            n_ref[h] = cs * n8 + _dot(jnp.broadcast_to(w_row, (SUBLANES, L)).astype(BF16), k_b[b])
            m_ref[h] = jnp.broadcast_to(m_new, (1, LANES))
        h_rows.append(jnp.concatenate(h_cols, axis=0))
    hh = jnp.concatenate(h_rows, axis=1).T
    og = _sigmoid(pm_ref[:, 3 * GW:4 * GW])
    y_ref[...] = _head_norm(hh, HEAD_LN_EPS) * ng_ref[...] * og


def _mlstm(pm, pg, cw, cb, gb, ng, bsz, seq):
    MB = M_BLOCK
    nb = seq // MB
    row = lambda b, i: (b * nb + i, 0)
    const = lambda b, i: (0, 0)
    return pl.pallas_call(
        _mlstm_kernel,
        grid=(bsz, nb),
        in_specs=[pl.BlockSpec((MB, 4 * GW), row), pl.BlockSpec((MB, LANES), row),
                  pl.BlockSpec(cw.shape, const), pl.BlockSpec(cb.shape, const),
                  pl.BlockSpec(gb.shape, const), pl.BlockSpec(ng.shape, const)],
        out_specs=pl.BlockSpec((MB, GW), row),
        out_shape=jax.ShapeDtypeStruct((bsz * seq, GW), F32),
        scratch_shapes=[pltpu.VMEM((SUBLANES + MB, 2 * GW), F32), pltpu.VMEM((NH, DH, DH), F32),
                        pltpu.VMEM((NH, SUBLANES, DH), F32), pltpu.VMEM((NH, 1, LANES), F32)],
        compiler_params=_cparams(("parallel", "arbitrary")),
        name="mlstm",
    )(pm, pg, cw, cb, gb, ng)


def _to_blocks(x, rows):
    return jnp.stack([x[c * rows:(c + 1) * rows, h * DH:(h + 1) * DH]
                      for c in range(x.shape[0] // rows) for h in range(NH)], axis=0)


def _from_heads(x):
    return jnp.concatenate([x[h] for h in range(NH)], axis=-1)


def _bmm(a, b):
    return jnp.einsum('hij,hjk->hik', a.astype(BF16), b.astype(BF16), preferred_element_type=F32)


def _bmm_nt(a, b):
    return jnp.einsum('hik,hjk->hij', a.astype(BF16), b.astype(BF16), preferred_element_type=F32)


def _bmm_tn(a, b):
    return jnp.einsum('hki,hkj->hij', a.astype(BF16), b.astype(BF16), preferred_element_type=F32)


def _unit_lower_inverse(a, ri, ci, log2_n):
    same = lambda s: (ri >> s) == (ci >> s)
    eye = jnp.where(ri == ci, 1.0, 0.0).astype(F32)
    a8 = jnp.where(same(3), a, 0.0)
    a8_2 = _bmm(a8, a8)
    a8_4 = _bmm(a8_2, a8_2)
    inv = _bmm(_bmm(eye - a8, eye + a8_2), eye + a8_4)
    for s in range(3, log2_n):
        off = jnp.where(same(s + 1) & jnp.logical_not(same(s)), a, 0.0)
        inv = inv - _bmm(inv, _bmm(off, inv))
    return inv


def _rwkv_kernel(pr_ref, mu_ref, w0_ref, a0_ref, wc_ref, gup_ref, kk_ref, ka_ref, rk_ref, lg_ref, lb_ref,
                 y_ref, xs_ref, st_ref):
    RB, L = R_BLOCK, R_CHUNK
    log2_l = L.bit_length() - 1

    @pl.when(pl.program_id(1) == 0)
    def _():
        xs_ref[0:SUBLANES, :] = jnp.zeros((SUBLANES, 4 * GW), F32)
        st_ref[...] = jnp.zeros_like(st_ref)

    pr = pr_ref[...]
    xs_ref[SUBLANES:SUBLANES + RB, :] = pr
    shifted = xs_ref[pl.ds(SUBLANES - 1, RB), :]
    xs_ref[0:SUBLANES, :] = pr[RB - SUBLANES:RB, :]
    prm = pr + mu_ref[...] * (shifted - pr)

    r = prm[:, 0:GW]
    k = prm[:, GW:2 * GW]
    v = prm[:, 2 * GW:3 * GW]
    z = prm[:, 3 * GW:3 * GW + LANES]
    lane = lax.broadcasted_iota(jnp.int32, (1, LANES), 1)
    wa = _bdot(jnp.where(lane < DH, jnp.tanh(z), z), wc_ref[...])
    w_log = -_softplus(-(w0_ref[...] + wa[:, 0:GW])) - 0.5
    logw = -jnp.exp(w_log)
    a = _sigmoid(a0_ref[...] + wa[:, GW:2 * GW])
    g = _bdot(_sigmoid(prm[:, 3 * GW + LANES:4 * GW]), gup_ref[...])

    ones_bd = _same_head_matrix(1.0)
    kk = k * kk_ref[...]
    kk = kk / jnp.maximum(jnp.sqrt(_bdot(kk * kk, ones_bd)), 1e-12)
    k2 = k * (1.0 + (a - 1.0) * ka_ref[...])
    bonus = _bdot(r * k2 * rk_ref[...], ones_bd) * v
    b = kk * a

    ri = lax.broadcasted_iota(jnp.int32, (2 * RB, RB), 0)
    ci = lax.broadcasted_iota(jnp.int32, (2 * RB, RB), 1)
    rr = ri & (RB - 1)
    sel = jnp.where(((rr >> log2_l) == (ci >> log2_l)) & ((ri >= RB) | (rr >= ci)), 1.0, 0.0)
    cums = _sel_dot(sel, logw, 2)
    cum = cums[0:RB]
    tot = cums[RB:2 * RB]
    p_inv = jnp.exp(-cum)
    p_end = jnp.exp(tot - cum)
    nc = RB // L
    kkd = _to_blocks(kk * jnp.exp(cum - logw), L)
    rd = _to_blocks(r * jnp.exp(cum), L)
    bi = _to_blocks(b * p_inv, L)
    ki = _to_blocks(k2 * p_inv, L)
    be = _to_blocks(b * p_end, L)
    ke = _to_blocks(k2 * p_end, L)
    vh = _to_blocks(v, L)
    p_tot = _to_blocks(jnp.exp(tot), L)[:, 0:1, :]

    rl = lax.broadcasted_iota(jnp.int32, (L, L), 0)
    cl = lax.broadcasted_iota(jnp.int32, (L, L), 1)
    strict = rl > cl
    incl = rl >= cl

    kr = jnp.concatenate([kkd, rd], axis=1)
    xb = _bmm_nt(kr, bi)
    xk = _bmm_nt(kr, ki)
    a_b = jnp.where(strict, xb[:, 0:L], 0.0)
    b_b = jnp.where(incl, xb[:, L:2 * L], 0.0)
    a_k = jnp.where(strict, xk[:, 0:L], 0.0)
    b_k = jnp.where(incl, xk[:, L:2 * L], 0.0)
    t_inv = _unit_lower_inverse(a_b, rl, cl, log2_l)
    w_q = _bmm(t_inv, kkd)
    u_0 = _bmm(t_inv, _bmm(a_k, vh))
    q_eff = rd - _bmm(b_b, w_q)
    y_0 = _bmm(b_k, vh) - _bmm(b_b, u_0)
    m_t = jnp.where(rl == cl, p_tot, 0.0) - _bmm_tn(be, w_q)
    n_t = _bmm_tn(ke, vh) - _bmm_tn(be, u_0)

    st = st_ref[...]
    ys = []
    for c in range(nc):
        blk = slice(c * NH, (c + 1) * NH)
        ys.append(_from_heads(_bmm(q_eff[blk], st) + y_0[blk]))
        st = _bmm(m_t[blk], st) + n_t[blk]
    st_ref[...] = st

    y = _head_norm(jnp.concatenate(ys, axis=0), RWKV_LN_EPS) * lg_ref[...] + lb_ref[...]
    y_ref[...] = (y + bonus) * g


def _rwkv(pr, mu, w0, a0, wc, gup, kk, ka, rk, lg, lb, bsz, seq):
    RB = R_BLOCK
    nb = seq // RB
    row = lambda b, i: (b * nb + i, 0)
    const = lambda b, i: (0, 0)
    params = (mu, w0, a0, wc, gup, kk, ka, rk, lg, lb)
    return pl.pallas_call(
        _rwkv_kernel,
        grid=(bsz, nb),
        in_specs=[pl.BlockSpec((RB, 4 * GW), row)] + [pl.BlockSpec(p.shape, const) for p in params],
        out_specs=pl.BlockSpec((RB, GW), row),
        out_shape=jax.ShapeDtypeStruct((bsz * seq, GW), F32),
        scratch_shapes=[pltpu.VMEM((SUBLANES + RB, 4 * GW), F32), pltpu.VMEM((NH, DH, DH), F32)],
        compiler_params=_cparams(("parallel", "arbitrary")),
        name="rwkv",
    )(pr, *params)


def _ret_kernel(pt_ref, cos_ref, sa_ref, sb_ref, dm_ref, qd_ref, kd_ref, cd_ref, gg_ref, gb_ref,
                y_ref, r_ref):
    MB, L = M_BLOCK, M_CHUNK

    @pl.when(pl.program_id(1) == 0)
    def _():
        r_ref[...] = jnp.zeros_like(r_ref)

    cos = cos_ref[...]
    sa = sa_ref[...]
    sb = sb_ref[...]

    def rope(t):
        return t * cos + pltpu.roll(t, GW - DH // 2, 1) * sa + pltpu.roll(t, DH // 2, 1) * sb

    q_all = rope(pt_ref[:, 0:GW])
    k_all = rope(pt_ref[:, GW:2 * GW]) * (DH ** -0.5)
    v_all = pt_ref[:, 2 * GW:3 * GW]

    masks = _head_masks()
    same_head = _same_head_matrix(1.0)
    r_st = r_ref[...]
    os_ = []
    for c in range(MB // L):
        rows = slice(c * L, (c + 1) * L)
        q, k = q_all[rows], k_all[rows]
        kb = k.astype(BF16)
        vb = v_all[rows].astype(BF16)
        o = _bdot(q, r_st) * qd_ref[...]
        for h in range(NH):
            sc = _dot_nt((q * masks[h]).astype(BF16), kb) * dm_ref[h]
            o = o + masks[h] * _dot(sc.astype(BF16), vb)
        os_.append(o)
        r_st = r_st * cd_ref[...] + same_head * _dot_tn((k * kd_ref[...]).astype(BF16), vb)
    r_ref[...] = r_st
    gate = pt_ref[:, 3 * GW:4 * GW]
    y_ref[...] = gate * _sigmoid(gate) * (_head_norm(jnp.concatenate(os_, axis=0), HEAD_LN_EPS) * gg_ref[...]
                                          + gb_ref[...])


def _retention(pt, cos, sa, sb, dm, qd, kd, cd, gg, gb, bsz, seq):
    MB = M_BLOCK
    nb = seq // MB
    row = lambda b, i: (b * nb + i, 0)
    pos = lambda b, i: (i, 0)
    const2 = lambda b, i: (0, 0)
    const3 = lambda b, i: (0, 0, 0)
    return pl.pallas_call(
        _ret_kernel,
        grid=(bsz, nb),
        in_specs=[pl.BlockSpec((MB, 4 * GW), row),
                  pl.BlockSpec((MB, GW), pos), pl.BlockSpec((MB, GW), pos), pl.BlockSpec((MB, GW), pos),
                  pl.BlockSpec(dm.shape, const3), pl.BlockSpec(qd.shape, const2), pl.BlockSpec(kd.shape, const2),
                  pl.BlockSpec(cd.shape, const2), pl.BlockSpec(gg.shape, const2), pl.BlockSpec(gb.shape, const2)],
        out_specs=pl.BlockSpec((MB, GW), row),
        out_shape=jax.ShapeDtypeStruct((bsz * seq, GW), F32),
        scratch_shapes=[pltpu.VMEM((GW, GW), F32)],
        compiler_params=_cparams(("parallel", "arbitrary")),
        name="retention",
    )(pt, cos, sa, sb, dm, qd, kd, cd, gg, gb)


def _gelu_tanh(x):
    return 0.5 * x * (1.0 + jnp.tanh(0.7978845608028654 * (x + 0.044715 * (x * x * x))))


def _s5_kernel(u_ref, ar_ref, ai_ref, bm_ref, cm_ref, d_ref, wg_ref, bg_ref, y_ref, bu_ref, x_ref, *, nb):
    T = u_ref.shape[0] // nb

    @pl.when(pl.program_id(0) == 0)
    def _():
        x_ref[...] = jnp.zeros_like(x_ref)

    u = u_ref[...]
    bu_ref[...] = _dot(u.astype(BF16), bm_ref[...])
    ar = jnp.broadcast_to(ar_ref[...], (nb, S5_N))
    ai = jnp.broadcast_to(ai_ref[...], (nb, S5_N))

    def step(t, carry):
        xr, xi = carry
        r0 = pl.multiple_of(t * nb, nb)
        nr = ar * xr - ai * xi + bu_ref[pl.ds(r0, nb), 0:S5_N]
        ni = ar * xi + ai * xr + bu_ref[pl.ds(r0, nb), S5_N:2 * S5_N]
        bu_ref[pl.ds(r0, nb), 0:S5_N] = nr
        bu_ref[pl.ds(r0, nb), S5_N:2 * S5_N] = ni
        return nr, ni

    xr, xi = lax.fori_loop(0, T, step, (x_ref[:, 0:S5_N], x_ref[:, S5_N:2 * S5_N]), unroll=4)
    x_ref[:, 0:S5_N] = xr
    x_ref[:, S5_N:2 * S5_N] = xi

    y = _dot(bu_ref[...].astype(BF16), cm_ref[...])
    y = _gelu_tanh(y + d_ref[...] * u)
    z = _dot(y.astype(BF16), wg_ref[...]) + bg_ref[...]
    y_ref[...] = z[:, 0:GW] * _sigmoid(z[:, GW:2 * GW])


def _s5(ps_tm, ar, ai, bm, cm, d, wg, bg, bsz, seq):
    T = S5_T
    rows = T * bsz
    const = lambda i: (0, 0)
    params = (ar, ai, bm, cm, d, wg, bg)
    return pl.pallas_call(
        functools.partial(_s5_kernel, nb=bsz),
        grid=(seq // T,),
        in_specs=[pl.BlockSpec((rows, GW), lambda i: (i, 0))] + [pl.BlockSpec(p.shape, const) for p in params],
        out_specs=pl.BlockSpec((rows, GW), lambda i: (i, 0)),
        out_shape=jax.ShapeDtypeStruct((seq * bsz, GW), F32),
        scratch_shapes=[pltpu.VMEM((rows, 2 * S5_N), F32), pltpu.VMEM((bsz, 2 * S5_N), F32)],
        compiler_params=_cparams(("arbitrary",)),
        name="s5",
    )(ps_tm, *params)


def _outffn_kernel(x_ref, ym_ref, yr_ref, yt_ref, ys_ref, wo_ref, g1_ref, g2_ref, g3_ref, w1_ref, w2_ref, o_ref):
    y = _dot(ym_ref[...].astype(BF16), wo_ref[0:GW, :])
    y = y + _dot(yr_ref[...].astype(BF16), wo_ref[GW:2 * GW, :])
    y = y + _dot(yt_ref[...].astype(BF16), wo_ref[2 * GW:3 * GW, :])
    y = y + _dot(ys_ref[...].astype(BF16), wo_ref[3 * GW:4 * GW, :])
    x1 = x_ref[...] + _rms(y, g1_ref[...])
    h = _rms(x1, g2_ref[...]).astype(BF16)
    f = jnp.zeros_like(x1)
    for j in range(D_FF // D_MODEL):
        a = jnp.maximum(_dot(h, w1_ref[:, j * D_MODEL:(j + 1) * D_MODEL]), 0.0)
        f = f + _dot((a * a).astype(BF16), w2_ref[j * D_MODEL:(j + 1) * D_MODEL, :])
    o_ref[...] = x1 + _rms(f, g3_ref[...])


def _outffn(x2, ym, yr, yt, ys_tm, wo, g1, g2, g3, w1, w2, bsz, seq):
    tm = TM_PROJ
    nt = seq // tm
    row = lambda b, i: (b * nt + i, 0)
    const = lambda b, i: (0, 0)
    wspec = lambda w: pl.BlockSpec(w.shape, const, pipeline_mode=pl.Buffered(1))
    return pl.pallas_call(
        _outffn_kernel,
        grid=(bsz, nt),
        in_specs=[pl.BlockSpec((tm, D_MODEL), row),
                  pl.BlockSpec((tm, GW), row), pl.BlockSpec((tm, GW), row), pl.BlockSpec((tm, GW), row),
                  pl.BlockSpec((tm, GW), lambda b, i: (i, b)),
                  wspec(wo), pl.BlockSpec((1, D_MODEL), const), pl.BlockSpec((1, D_MODEL), const),
                  pl.BlockSpec((1, D_MODEL), const), wspec(w1), wspec(w2)],
        out_specs=pl.BlockSpec((tm, D_MODEL), row),
        out_shape=jax.ShapeDtypeStruct((bsz * seq, D_MODEL), F32),
        compiler_params=_cparams(("parallel", "parallel")),
        name="outffn",
    )(x2, ym, yr, yt, ys_tm, wo, g1, g2, g3, w1, w2)


def _rope_tables(seq):
    half = DH // 2
    inv = ROPE_BASE ** (-jnp.arange(half, dtype=F32) / half)
    ang = jnp.arange(seq, dtype=F32)[:, None] * inv[None, :]
    cos, sin = jnp.cos(ang), jnp.sin(ang)
    zero = jnp.zeros_like(sin)
    tile = lambda first, second: jnp.tile(jnp.concatenate([first, second], axis=-1), (1, NH))
    return tile(cos, cos), tile(-sin, zero), tile(zero, sin)


def _retention_tables():
    L = M_CHUNK
    log_gamma = jnp.log(1.0 - 2.0 ** (-5.0 - jnp.arange(NH, dtype=F32)))
    idx = jnp.arange(L, dtype=F32)
    diff = idx[:, None] - idx[None, :]
    causal = diff >= 0
    decay_mat = jnp.where(causal, jnp.exp(jnp.where(causal, diff, 0.0) * log_gamma[:, None, None]), 0.0)
    lanes = lambda t: jnp.repeat(t.T, DH, axis=1)
    q_decay = lanes(jnp.exp((idx + 1.0) * log_gamma[:, None]))
    k_decay = lanes(jnp.exp((L - 1.0 - idx) * log_gamma[:, None]))
    chunk_decay = jnp.repeat(jnp.exp(L * log_gamma), DH)[None, :]
    return decay_mat, q_decay, k_decay, chunk_decay


def _s5_tables(lam_re, lam_im, log_dt, b_re, b_im, c_re, c_im):
    G, P, C = S5_GROUPS, S5_STATE, S5_GROUP
    dt = jnp.exp(log_dt)[:, None]
    mag = jnp.exp(lam_re * dt)
    abar_re, abar_im = mag * jnp.cos(lam_im * dt), mag * jnp.sin(lam_im * dt)
    den = lam_re * lam_re + lam_im * lam_im
    num_re, num_im = abar_re - 1.0, abar_im
    f_re = (num_re * lam_re + num_im * lam_im) / den
    f_im = (num_im * lam_re - num_re * lam_im) / den
    bbar_re = f_re[..., None] * b_re - f_im[..., None] * b_im
    bbar_im = f_re[..., None] * b_im + f_im[..., None] * b_re
    eye = jnp.eye(G, dtype=F32)
    bd_in = lambda t: jnp.einsum('gpc,gh->gchp', t, eye).reshape(G * C, G * P)
    bd_out = lambda t: jnp.einsum('gcp,gh->gphc', t, eye).reshape(G * P, G * C)
    bmat = jnp.concatenate([bd_in(bbar_re), bd_in(bbar_im)], axis=1)
    cmat = jnp.concatenate([bd_out(c_re), -bd_out(c_im)], axis=0)
    return abar_re.reshape(1, G * P), abar_im.reshape(1, G * P), bmat.astype(BF16), cmat.astype(BF16)


def _pad_lanes(t, width=LANES):
    return jnp.pad(t, ((0, 0), (0, width - t.shape[-1])))


def kernel(x, norm_mix_pre, norm_mix_post, w_in, m_conv_w, m_conv_b, m_i_bias, m_f_bias, m_norm, r_mu, r_w0, r_w_up, r_a0, r_a_up, r_g_up, r_k_k, r_k_a, r_r_k, r_ln_g, r_ln_b, t_gn_g, t_gn_b, s_lam_re, s_lam_im, s_log_dt, s_b_re, s_b_im, s_c_re, s_c_im, s_d, s_w_glu, s_b_glu, w_out, norm_ffn_pre, norm_ffn_post, w_ff1, w_ff2):
    bsz, seq, _ = x.shape
    depth = w_in.shape[0]
    assert seq % TM_PROJ == 0 and seq % M_BLOCK == 0 and seq % R_BLOCK == 0 and seq % S5_T == 0
    assert bsz % SUBLANES == 0
    row = lambda t: t.reshape(1, -1)
    m_cols = 4 * GW + 2 * NH
    o_r = m_cols
    o_t = o_r + 4 * GW
    o_s = o_t + 4 * GW

    cos, sin_a, sin_b = _rope_tables(seq)
    dm, qd, kd, cd = _retention_tables()

    x2 = x.reshape(bsz * seq, D_MODEL)
    for l in range(depth):
        wl = w_in[l]
        wm = wl[:, 0:4 * GW].astype(BF16)
        wg = _pad_lanes(wl[:, 4 * GW:m_cols]).astype(BF16)
        wr = wl[:, o_r:o_t].astype(BF16)
        wt = wl[:, o_t:o_s].astype(BF16)
        ws = wl[:, o_s:].astype(BF16)
        pm, pg, pr, pt, ps_tm = _inproj(x2, row(norm_mix_pre[l]), wm, wg, wr, wt, ws, bsz, seq)

        gate_bias = _pad_lanes(jnp.concatenate([m_i_bias[l], m_f_bias[l]])[None, :])
        y_m = _mlstm(pm, pg, m_conv_w[l], row(m_conv_b[l]), gate_bias, row(m_norm[l]), bsz, seq)

        zeros = jnp.zeros((DH, GW), F32)
        w_comb = jnp.concatenate([jnp.concatenate([r_w_up[l], zeros], axis=1),
                                  jnp.concatenate([zeros, r_a_up[l]], axis=1)], axis=0)
        y_r = _rwkv(pr, row(r_mu[l]), row(r_w0[l]), row(r_a0[l]), w_comb, r_g_up[l], row(r_k_k[l]),
                    row(r_k_a[l]), row(r_r_k[l]), row(r_ln_g[l]), row(r_ln_b[l]), bsz, seq)

        y_t = _retention(pt, cos, sin_a, sin_b, dm, qd, kd, cd, row(t_gn_g[l]), row(t_gn_b[l]), bsz, seq)

        ar, ai, bmat, cmat = _s5_tables(s_lam_re[l], s_lam_im[l], s_log_dt[l], s_b_re[l], s_b_im[l],
                                        s_c_re[l], s_c_im[l])
        y_s_tm = _s5(ps_tm.reshape(seq * bsz, GW), ar, ai, bmat, cmat, row(s_d[l]),
                     s_w_glu[l].astype(BF16), row(s_b_glu[l]), bsz, seq)

        x2 = _outffn(x2, y_m, y_r, y_t, y_s_tm.reshape(seq, bsz * GW), w_out[l].astype(BF16),
                     row(norm_mix_post[l]), row(norm_ffn_pre[l]), row(norm_ffn_post[l]),
                     w_ff1[l].astype(BF16), w_ff2[l].astype(BF16), bsz, seq)
    return x2.reshape(bsz, seq, D_MODEL)
```

```python
import functools

import jax
import jax.numpy as jnp
from jax import lax
from jax.experimental import pallas as pl
from jax.experimental.pallas import tpu as pltpu

F32 = jnp.float32
BF16 = jnp.bfloat16

D_MODEL = 1024
GW = 256
NH = 4
DH = 64
HEAD_SHIFT = 6
D_FF = 4 * D_MODEL
S5_GROUPS = 16
S5_GROUP = 16
S5_STATE = 64
S5_N = S5_GROUPS * S5_STATE
CONV_WIDTH = 4
ROPE_BASE = 10000.0
RMS_EPS = 1e-6
HEAD_LN_EPS = 1e-5
RWKV_LN_EPS = 64e-5

SUBLANES = 8
LANES = 128
M_CHUNK = 128
M_BLOCK = 1024
R_CHUNK = 64
R_BLOCK = 1024
S5_T = 128
TM_PROJ = 512
VMEM_LIMIT = 56 * 1024 * 1024


def _cparams(sem):
    return pltpu.CompilerParams(dimension_semantics=sem, vmem_limit_bytes=VMEM_LIMIT)


def _dot(a, b, precision=None):
    return jnp.dot(a, b, preferred_element_type=F32, precision=precision)


def _dot_nt(a, b, precision=None):
    return lax.dot_general(a, b, (((1,), (1,)), ((), ())), preferred_element_type=F32, precision=precision)


def _dot_tn(a, b, precision=None):
    return lax.dot_general(a, b, (((0,), (0,)), ((), ())), preferred_element_type=F32, precision=precision)


_FORMS = {"nn": _dot, "nt": _dot_nt, "tn": _dot_tn}


def _bdot(a, b, form="nn"):
    return _FORMS[form](a.astype(BF16), b.astype(BF16))


def _split_bf16(a, terms):
    parts = []
    for _ in range(terms - 1):
        hi = a.astype(BF16)
        parts.append(hi)
        a = a - hi.astype(F32)
    parts.append(a.astype(BF16))
    return parts


def _sel_dot(sel, b, terms):
    sel = sel.astype(BF16)
    return sum(_dot(sel, p) for p in _split_bf16(b, terms))


def _sigmoid(x):
    return 1.0 / (1.0 + jnp.exp(-x))


def _softplus(x):
    return jnp.maximum(x, 0.0) + jnp.log(1.0 + jnp.exp(-jnp.abs(x)))


def _head_masks():
    lane = lax.broadcasted_iota(jnp.int32, (1, GW), 1)
    return [jnp.where((lane >> HEAD_SHIFT) == h, 1.0, 0.0).astype(F32) for h in range(NH)]


def _same_head_matrix(scale):
    r = lax.broadcasted_iota(jnp.int32, (GW, GW), 0)
    c = lax.broadcasted_iota(jnp.int32, (GW, GW), 1)
    return jnp.where((r >> HEAD_SHIFT) == (c >> HEAD_SHIFT), scale, 0.0).astype(F32)


def _head_norm(x, eps):
    avg = _same_head_matrix(1.0 / DH)
    mu = _bdot(x, avg)
    xc = x - mu
    var = _bdot(xc * xc, avg)
    return xc * lax.rsqrt(var + eps)


def _rms(x, g):
    return x * lax.rsqrt(jnp.mean(x * x, axis=-1, keepdims=True) + RMS_EPS) * g


def _inproj_kernel(x_ref, g_ref, wm_ref, wg_ref, wr_ref, wt_ref, ws_ref,
                   pm_ref, pg_ref, pr_ref, pt_ref, ps_ref):
    h = _rms(x_ref[...], g_ref[...]).astype(BF16)
    pm_ref[...] = _dot(h, wm_ref[...])
    pg_ref[...] = _dot(h, wg_ref[...])
    pr_ref[...] = _dot(h, wr_ref[...])
    pt_ref[...] = _dot(h, wt_ref[...])
    ps_ref[...] = _dot(h, ws_ref[...])


def _inproj(x2, g, wm, wg, wr, wt, ws, bsz, seq):
    n = bsz * seq
    tm = TM_PROJ
    nt = seq // tm
    row = lambda b, i: (b * nt + i, 0)
    const = lambda b, i: (0, 0)
    wspec = lambda w: pl.BlockSpec(w.shape, const, pipeline_mode=pl.Buffered(1))
    return pl.pallas_call(
        _inproj_kernel,
        grid=(bsz, nt),
        in_specs=[pl.BlockSpec((tm, D_MODEL), row), pl.BlockSpec((1, D_MODEL), const),
                  wspec(wm), wspec(wg), wspec(wr), wspec(wt), wspec(ws)],
        out_specs=[pl.BlockSpec((tm, 4 * GW), row), pl.BlockSpec((tm, LANES), row),
                   pl.BlockSpec((tm, 4 * GW), row), pl.BlockSpec((tm, 4 * GW), row),
                   pl.BlockSpec((tm, GW), lambda b, i: (i, b))],
        out_shape=[jax.ShapeDtypeStruct((n, 4 * GW), F32), jax.ShapeDtypeStruct((n, LANES), F32),
                   jax.ShapeDtypeStruct((n, 4 * GW), F32), jax.ShapeDtypeStruct((n, 4 * GW), F32),
                   jax.ShapeDtypeStruct((seq, bsz * GW), F32)],
        compiler_params=_cparams(("parallel", "parallel")),
        name="inproj",
    )(x2, g, wm, wg, wr, wt, ws)


def _mlstm_kernel(pm_ref, pg_ref, cw_ref, cb_ref, gb_ref, ng_ref, y_ref, xs_ref, ct_ref, n_ref, m_ref):
    MB, L = M_BLOCK, M_CHUNK
    nc = MB // L

    @pl.when(pl.program_id(1) == 0)
    def _():
        xs_ref[0:SUBLANES, :] = jnp.zeros((SUBLANES, 2 * GW), F32)
        ct_ref[...] = jnp.zeros_like(ct_ref)
        n_ref[...] = jnp.zeros_like(n_ref)
        m_ref[...] = jnp.zeros_like(m_ref)

    xqk = pm_ref[:, 0:2 * GW]
    xs_ref[SUBLANES:SUBLANES + MB, :] = xqk
    conv = cb_ref[...]
    for j in range(CONV_WIDTH):
        conv = conv + xs_ref[pl.ds(SUBLANES - (CONV_WIDTH - 1) + j, MB), :] * cw_ref[j:j + 1, :]
    xs_ref[0:SUBLANES, :] = xqk[MB - SUBLANES:MB, :]
    qk = conv * _sigmoid(conv)
    q_t = qk[:, 0:GW].T
    k_all = qk[:, GW:2 * GW] * (DH ** -0.5)
    v_t = pm_ref[:, 2 * GW:3 * GW].T

    lane = lax.broadcasted_iota(jnp.int32, (1, LANES), 1)
    g2 = pg_ref[...] + gb_ref[...]
    lf = jnp.where((lane >= NH) & (lane < 2 * NH), -_softplus(-g2), 0.0)
    ri = lax.broadcasted_iota(jnp.int32, (L, L), 0)
    ci = lax.broadcasted_iota(jnp.int32, (L, L), 1)
    tril = jnp.where(ri >= ci, 1.0, 0.0)
    gcum = jnp.concatenate([_sel_dot(tril, lf[c * L:(c + 1) * L], 2) for c in range(nc)], axis=0)
    colmat = jnp.where(lane < NH, g2, gcum)
    dcol = colmat - pltpu.roll(colmat, LANES - NH, 1)
    rowmat = colmat.T
    g_rows = rowmat[NH:2 * NH]
    c_rows = rowmat[0:NH] - g_rows

    causal = ri <= ci
    blocks = [(c, h) for c in range(nc) for h in range(NH)]
    tsl = lambda c: slice(c * L, (c + 1) * L)
    hsl = lambda h: slice(h * DH, (h + 1) * DH)
    k_b = {b: k_all[tsl(b[0]), hsl(b[1])].astype(BF16) for b in blocks}
    q_b = {b: q_t[hsl(b[1]), tsl(b[0])].astype(BF16) for b in blocks}
    v_b = {b: v_t[hsl(b[1]), tsl(b[0])] for b in blocks}
    g_row = {b: g_rows[b[1]:b[1] + 1, tsl(b[0])] for b in blocks}

    m_loc, num_loc, den_loc = {}, {}, {}
    for b in blocks:
        c, h = b
        d_t = jnp.where(causal, g_row[b] + dcol[tsl(c), h:h + 1], -jnp.inf)
        m_loc[b] = jnp.max(d_t, axis=0, keepdims=True)
        s_t = _dot(k_b[b], q_b[b]) * jnp.exp(d_t - m_loc[b])
        num_loc[b] = _dot(v_b[b].astype(BF16), s_t.astype(BF16))
        den_loc[b] = jnp.sum(s_t, axis=0, keepdims=True)

    h_rows = []
    for c in range(nc):
        h_cols = []
        for h in range(NH):
            b = (c, h)
            ct = ct_ref[h]
            n8 = n_ref[h]
            m_prev = m_ref[h][:, 0:1]
            d_inter = g_row[b] + m_prev
            m_row = jnp.maximum(d_inter, m_loc[b])
            e_loc = jnp.exp(m_loc[b] - m_row)
            w_inter = jnp.exp(d_inter - m_row)
            num = e_loc * num_loc[b] + w_inter * _dot(ct.astype(BF16), q_b[b])
            den = e_loc * den_loc[b] + w_inter * _dot(n8.astype(BF16), q_b[b])[0:1]
            h_cols.append(num / jnp.maximum(jnp.abs(den), jnp.exp(-m_row)))

            g_last = g_row[b][:, L - 1:L]
            d_state = g_last + c_rows[h:h + 1, tsl(c)]
            m_new = jnp.maximum(g_last + m_prev, jnp.max(d_state, axis=1, keepdims=True))
            w_row = jnp.exp(d_state - m_new)
            cs = jnp.exp(g_last + m_prev - m_new)
            ct_ref[h] = cs * ct + _dot((v_b[b] * w_row).astype(BF16), k_b[b])
            n_ref[h] = cs * n8 + _dot(jnp.broadcast_to(w_row, (SUBLANES, L)).astype(BF16), k_b[b])
            m_ref[h] = jnp.broadcast_to(m_new, (1, LANES))
        h_rows.append(jnp.concatenate(h_cols, axis=0))
    hh = jnp.concatenate(h_rows, axis=1).T
    og = _sigmoid(pm_ref[:, 3 * GW:4 * GW])
    y_ref[...] = _head_norm(hh, HEAD_LN_EPS) * ng_ref[...] * og


def _mlstm(pm, pg, cw, cb, gb, ng, bsz, seq):
    MB = M_BLOCK
    nb = seq // MB
    row = lambda b, i: (b * nb + i, 0)
    const = lambda b, i: (0, 0)
    return pl.pallas_call(
        _mlstm_kernel,
        grid=(bsz, nb),
        in_specs=[pl.BlockSpec((MB, 4 * GW), row), pl.BlockSpec((MB, LANES), row),
                  pl.BlockSpec(cw.shape, const), pl.BlockSpec(cb.shape, const),
                  pl.BlockSpec(gb.shape, const), pl.BlockSpec(ng.shape, const)],
        out_specs=pl.BlockSpec((MB, GW), row),
        out_shape=jax.ShapeDtypeStruct((bsz * seq, GW), F32),
        scratch_shapes=[pltpu.VMEM((SUBLANES + MB, 2 * GW), F32), pltpu.VMEM((NH, DH, DH), F32),
                        pltpu.VMEM((NH, SUBLANES, DH), F32), pltpu.VMEM((NH, 1, LANES), F32)],
        compiler_params=_cparams(("parallel", "arbitrary")),
        name="mlstm",
    )(pm, pg, cw, cb, gb, ng)


def _to_blocks(x, rows):
    return jnp.stack([x[c * rows:(c + 1) * rows, h * DH:(h + 1) * DH]
                      for c in range(x.shape[0] // rows) for h in range(NH)], axis=0)


def _from_heads(x):
    return jnp.concatenate([x[h] for h in range(NH)], axis=-1)


def _bmm(a, b):
    return jnp.einsum('hij,hjk->hik', a.astype(BF16), b.astype(BF16), preferred_element_type=F32)


def _bmm_nt(a, b):
    return jnp.einsum('hik,hjk->hij', a.astype(BF16), b.astype(BF16), preferred_element_type=F32)


def _bmm_tn(a, b):
    return jnp.einsum('hki,hkj->hij', a.astype(BF16), b.astype(BF16), preferred_element_type=F32)


def _unit_lower_inverse(a, ri, ci, log2_n):
    same = lambda s: (ri >> s) == (ci >> s)
    eye = jnp.where(ri == ci, 1.0, 0.0).astype(F32)
    a8 = jnp.where(same(3), a, 0.0)
    a8_2 = _bmm(a8, a8)
    a8_4 = _bmm(a8_2, a8_2)
    inv = _bmm(_bmm(eye - a8, eye + a8_2), eye + a8_4)
    for s in range(3, log2_n):
        off = jnp.where(same(s + 1) & jnp.logical_not(same(s)), a, 0.0)
        inv = inv - _bmm(inv, _bmm(off, inv))
    return inv


def _rwkv_kernel(pr_ref, mu_ref, w0_ref, a0_ref, wc_ref, gup_ref, kk_ref, ka_ref, rk_ref, lg_ref, lb_ref,
                 y_ref, xs_ref, st_ref):
    RB, L = R_BLOCK, R_CHUNK
    log2_l = L.bit_length() - 1

    @pl.when(pl.program_id(1) == 0)
    def _():
        xs_ref[0:SUBLANES, :] = jnp.zeros((SUBLANES, 4 * GW), F32)
        st_ref[...] = jnp.zeros_like(st_ref)

    pr = pr_ref[...]
    xs_ref[SUBLANES:SUBLANES + RB, :] = pr
    shifted = xs_ref[pl.ds(SUBLANES - 1, RB), :]
    xs_ref[0:SUBLANES, :] = pr[RB - SUBLANES:RB, :]
    prm = pr + mu_ref[...] * (shifted - pr)

    r = prm[:, 0:GW]
    k = prm[:, GW:2 * GW]
    v = prm[:, 2 * GW:3 * GW]
    z = prm[:, 3 * GW:3 * GW + LANES]
    lane = lax.broadcasted_iota(jnp.int32, (1, LANES), 1)
    wa = _bdot(jnp.where(lane < DH, jnp.tanh(z), z), wc_ref[...])
    w_log = -_softplus(-(w0_ref[...] + wa[:, 0:GW])) - 0.5
    logw = -jnp.exp(w_log)
    a = _sigmoid(a0_ref[...] + wa[:, GW:2 * GW])
    g = _bdot(_sigmoid(prm[:, 3 * GW + LANES:4 * GW]), gup_ref[...])

    ones_bd = _same_head_matrix(1.0)
    kk = k * kk_ref[...]
    kk = kk / jnp.maximum(jnp.sqrt(_bdot(kk * kk, ones_bd)), 1e-12)
    k2 = k * (1.0 + (a - 1.0) * ka_ref[...])
    bonus = _bdot(r * k2 * rk_ref[...], ones_bd) * v
    b = kk * a

    nc = RB // L
    rl = lax.broadcasted_iota(jnp.int32, (L, L), 0)
    cl = lax.broadcasted_iota(jnp.int32, (L, L), 1)
    tril = jnp.where(rl >= cl, 1.0, 0.0)
    cum_c = [_sel_dot(tril, logw[c * L:(c + 1) * L], 2) for c in range(nc)]
    cum = jnp.concatenate(cum_c, axis=0)
    tot_c = [cc[L - 1:L] for cc in cum_c]
    tot = jnp.concatenate([jnp.broadcast_to(t, (L, GW)) for t in tot_c], axis=0)
    p_inv = jnp.exp(-cum)
    p_end = jnp.exp(tot - cum)
    kkd = _to_blocks(kk * jnp.exp(cum - logw), L)
    rd = _to_blocks(r * jnp.exp(cum), L)
    bi = _to_blocks(b * p_inv, L)
    ki = _to_blocks(k2 * p_inv, L)
    be = _to_blocks(b * p_end, L)
    ke = _to_blocks(k2 * p_end, L)
    vh = _to_blocks(v, L)
    p_tot = _to_blocks(jnp.concatenate([jnp.exp(t) for t in tot_c], axis=0), 1)

    strict = rl > cl
    incl = rl >= cl

    kr = jnp.concatenate([kkd, rd], axis=1)
    xb = _bmm_nt(kr, bi)
    xk = _bmm_nt(kr, ki)
    a_b = jnp.where(strict, xb[:, 0:L], 0.0)
    b_b = jnp.where(incl, xb[:, L:2 * L], 0.0)
    a_k = jnp.where(strict, xk[:, 0:L], 0.0)
    b_k = jnp.where(incl, xk[:, L:2 * L], 0.0)
    t_inv = _unit_lower_inverse(a_b, rl, cl, log2_l)
    w_q = _bmm(t_inv, kkd)
    u_0 = _bmm(t_inv, _bmm(a_k, vh))
    q_eff = rd - _bmm(b_b, w_q)
    y_0 = _bmm(b_k, vh) - _bmm(b_b, u_0)
    m_t = jnp.where(rl == cl, p_tot, 0.0) - _bmm_tn(be, w_q)
    n_t = _bmm_tn(ke, vh) - _bmm_tn(be, u_0)

    st = st_ref[...]
    ys = []
    for c in range(nc):
        blk = slice(c * NH, (c + 1) * NH)
        ys.append(_from_heads(_bmm(q_eff[blk], st) + y_0[blk]))
        st = _bmm(m_t[blk], st) + n_t[blk]
    st_ref[...] = st

    y = _head_norm(jnp.concatenate(ys, axis=0), RWKV_LN_EPS) * lg_ref[...] + lb_ref[...]
    y_ref[...] = (y + bonus) * g


def _rwkv(pr, mu, w0, a0, wc, gup, kk, ka, rk, lg, lb, bsz, seq):
    RB = R_BLOCK
    nb = seq // RB
    row = lambda b, i: (b * nb + i, 0)
    const = lambda b, i: (0, 0)
    params = (mu, w0, a0, wc, gup, kk, ka, rk, lg, lb)
    return pl.pallas_call(
        _rwkv_kernel,
        grid=(bsz, nb),
        in_specs=[pl.BlockSpec((RB, 4 * GW), row)] + [pl.BlockSpec(p.shape, const) for p in params],
        out_specs=pl.BlockSpec((RB, GW), row),
        out_shape=jax.ShapeDtypeStruct((bsz * seq, GW), F32),
        scratch_shapes=[pltpu.VMEM((SUBLANES + RB, 4 * GW), F32), pltpu.VMEM((NH, DH, DH), F32)],
        compiler_params=_cparams(("parallel", "arbitrary")),
        name="rwkv",
    )(pr, *params)


def _ret_kernel(pt_ref, cos_ref, sa_ref, sb_ref, dm_ref, qd_ref, kd_ref, cd_ref, gg_ref, gb_ref,
                y_ref, r_ref):
    MB, L = M_BLOCK, M_CHUNK

    @pl.when(pl.program_id(1) == 0)
    def _():
        r_ref[...] = jnp.zeros_like(r_ref)

    cos = cos_ref[...]
    sa = sa_ref[...]
    sb = sb_ref[...]

    def rope(t):
        return t * cos + pltpu.roll(t, GW - DH // 2, 1) * sa + pltpu.roll(t, DH // 2, 1) * sb

    q_all = rope(pt_ref[:, 0:GW])
    k_all = rope(pt_ref[:, GW:2 * GW]) * (DH ** -0.5)
    v_all = pt_ref[:, 2 * GW:3 * GW]

    masks = _head_masks()
    same_head = _same_head_matrix(1.0)
    r_st = r_ref[...]
    os_ = []
    for c in range(MB // L):
        rows = slice(c * L, (c + 1) * L)
        q, k = q_all[rows], k_all[rows]
        kb = k.astype(BF16)
        vb = v_all[rows].astype(BF16)
        o = _bdot(q, r_st) * qd_ref[...]
        for h in range(NH):
            sc = _dot_nt((q * masks[h]).astype(BF16), kb) * dm_ref[h]
            o = o + masks[h] * _dot(sc.astype(BF16), vb)
        os_.append(o)
        r_st = r_st * cd_ref[...] + same_head * _dot_tn((k * kd_ref[...]).astype(BF16), vb)
    r_ref[...] = r_st
    gate = pt_ref[:, 3 * GW:4 * GW]
    y_ref[...] = gate * _sigmoid(gate) * (_head_norm(jnp.concatenate(os_, axis=0), HEAD_LN_EPS) * gg_ref[...]
                                          + gb_ref[...])


def _retention(pt, cos, sa, sb, dm, qd, kd, cd, gg, gb, bsz, seq):
    MB = M_BLOCK
    nb = seq // MB
    row = lambda b, i: (b * nb + i, 0)
    pos = lambda b, i: (i, 0)
    const2 = lambda b, i: (0, 0)
    const3 = lambda b, i: (0, 0, 0)
    return pl.pallas_call(
        _ret_kernel,
        grid=(bsz, nb),
        in_specs=[pl.BlockSpec((MB, 4 * GW), row),
                  pl.BlockSpec((MB, GW), pos), pl.BlockSpec((MB, GW), pos), pl.BlockSpec((MB, GW), pos),
                  pl.BlockSpec(dm.shape, const3), pl.BlockSpec(qd.shape, const2), pl.BlockSpec(kd.shape, const2),
                  pl.BlockSpec(cd.shape, const2), pl.BlockSpec(gg.shape, const2), pl.BlockSpec(gb.shape, const2)],
        out_specs=pl.BlockSpec((MB, GW), row),
        out_shape=jax.ShapeDtypeStruct((bsz * seq, GW), F32),
        scratch_shapes=[pltpu.VMEM((GW, GW), F32)],
        compiler_params=_cparams(("parallel", "arbitrary")),
        name="retention",
    )(pt, cos, sa, sb, dm, qd, kd, cd, gg, gb)


def _gelu_tanh(x):
    return 0.5 * x * (1.0 + jnp.tanh(0.7978845608028654 * (x + 0.044715 * (x * x * x))))


def _s5_kernel(u_ref, ar_ref, ai_ref, bm_ref, cm_ref, d_ref, wg_ref, bg_ref, y_ref, bu_ref, x_ref, *, nb):
    T = u_ref.shape[0] // nb

    @pl.when(pl.program_id(0) == 0)
    def _():
        x_ref[...] = jnp.zeros_like(x_ref)

    u = u_ref[...]
    bu_ref[...] = _dot(u.astype(BF16), bm_ref[...])
    ar = jnp.broadcast_to(ar_ref[...], (nb, S5_N))
    ai = jnp.broadcast_to(ai_ref[...], (nb, S5_N))

    def step(t, carry):
        xr, xi = carry
        r0 = pl.multiple_of(t * nb, nb)
        nr = ar * xr - ai * xi + bu_ref[pl.ds(r0, nb), 0:S5_N]
        ni = ar * xi + ai * xr + bu_ref[pl.ds(r0, nb), S5_N:2 * S5_N]
        bu_ref[pl.ds(r0, nb), 0:S5_N] = nr
        bu_ref[pl.ds(r0, nb), S5_N:2 * S5_N] = ni
        return nr, ni

    xr, xi = lax.fori_loop(0, T, step, (x_ref[:, 0:S5_N], x_ref[:, S5_N:2 * S5_N]), unroll=4)
    x_ref[:, 0:S5_N] = xr
    x_ref[:, S5_N:2 * S5_N] = xi

    y = _dot(bu_ref[...].astype(BF16), cm_ref[...])
    y = _gelu_tanh(y + d_ref[...] * u)
    z = _dot(y.astype(BF16), wg_ref[...]) + bg_ref[...]
    y_ref[...] = z[:, 0:GW] * _sigmoid(z[:, GW:2 * GW])


def _s5(ps_tm, ar, ai, bm, cm, d, wg, bg, bsz, seq):
    T = S5_T
    rows = T * bsz
    const = lambda i: (0, 0)
    params = (ar, ai, bm, cm, d, wg, bg)
    return pl.pallas_call(
        functools.partial(_s5_kernel, nb=bsz),
        grid=(seq // T,),
        in_specs=[pl.BlockSpec((rows, GW), lambda i: (i, 0))] + [pl.BlockSpec(p.shape, const) for p in params],
        out_specs=pl.BlockSpec((rows, GW), lambda i: (i, 0)),
        out_shape=jax.ShapeDtypeStruct((seq * bsz, GW), F32),
        scratch_shapes=[pltpu.VMEM((rows, 2 * S5_N), F32), pltpu.VMEM((bsz, 2 * S5_N), F32)],
        compiler_params=_cparams(("arbitrary",)),
        name="s5",
    )(ps_tm, *params)


def _outffn_kernel(x_ref, ym_ref, yr_ref, yt_ref, ys_ref, wo_ref, g1_ref, g2_ref, g3_ref, w1_ref, w2_ref, o_ref):
    y = _dot(ym_ref[...].astype(BF16), wo_ref[0:GW, :])
    y = y + _dot(yr_ref[...].astype(BF16), wo_ref[GW:2 * GW, :])
    y = y + _dot(yt_ref[...].astype(BF16), wo_ref[2 * GW:3 * GW, :])
    y = y + _dot(ys_ref[...].astype(BF16), wo_ref[3 * GW:4 * GW, :])
    x1 = x_ref[...] + _rms(y, g1_ref[...])
    h = _rms(x1, g2_ref[...]).astype(BF16)
    f = jnp.zeros_like(x1)
    for j in range(D_FF // D_MODEL):
        a = jnp.maximum(_dot(h, w1_ref[:, j * D_MODEL:(j + 1) * D_MODEL]), 0.0)
        f = f + _dot((a * a).astype(BF16), w2_ref[j * D_MODEL:(j + 1) * D_MODEL, :])
    o_ref[...] = x1 + _rms(f, g3_ref[...])


def _outffn(x2, ym, yr, yt, ys_tm, wo, g1, g2, g3, w1, w2, bsz, seq):
    tm = TM_PROJ
    nt = seq // tm
    row = lambda b, i: (b * nt + i, 0)
    const = lambda b, i: (0, 0)
    wspec = lambda w: pl.BlockSpec(w.shape, const, pipeline_mode=pl.Buffered(1))
    return pl.pallas_call(
        _outffn_kernel,
        grid=(bsz, nt),
        in_specs=[pl.BlockSpec((tm, D_MODEL), row),
                  pl.BlockSpec((tm, GW), row), pl.BlockSpec((tm, GW), row), pl.BlockSpec((tm, GW), row),
                  pl.BlockSpec((tm, GW), lambda b, i: (i, b)),
                  wspec(wo), pl.BlockSpec((1, D_MODEL), const), pl.BlockSpec((1, D_MODEL), const),
                  pl.BlockSpec((1, D_MODEL), const), wspec(w1), wspec(w2)],
        out_specs=pl.BlockSpec((tm, D_MODEL), row),
        out_shape=jax.ShapeDtypeStruct((bsz * seq, D_MODEL), F32),
        compiler_params=_cparams(("parallel", "parallel")),
        name="outffn",
    )(x2, ym, yr, yt, ys_tm, wo, g1, g2, g3, w1, w2)


def _rope_tables(seq):
    half = DH // 2
    inv = ROPE_BASE ** (-jnp.arange(half, dtype=F32) / half)
    ang = jnp.arange(seq, dtype=F32)[:, None] * inv[None, :]
    cos, sin = jnp.cos(ang), jnp.sin(ang)
    zero = jnp.zeros_like(sin)
    tile = lambda first, second: jnp.tile(jnp.concatenate([first, second], axis=-1), (1, NH))
    return tile(cos, cos), tile(-sin, zero), tile(zero, sin)


def _retention_tables():
    L = M_CHUNK
    log_gamma = jnp.log(1.0 - 2.0 ** (-5.0 - jnp.arange(NH, dtype=F32)))
    idx = jnp.arange(L, dtype=F32)
    diff = idx[:, None] - idx[None, :]
    causal = diff >= 0
    decay_mat = jnp.where(causal, jnp.exp(jnp.where(causal, diff, 0.0) * log_gamma[:, None, None]), 0.0)
    lanes = lambda t: jnp.repeat(t.T, DH, axis=1)
    q_decay = lanes(jnp.exp((idx + 1.0) * log_gamma[:, None]))
    k_decay = lanes(jnp.exp((L - 1.0 - idx) * log_gamma[:, None]))
    chunk_decay = jnp.repeat(jnp.exp(L * log_gamma), DH)[None, :]
    return decay_mat, q_decay, k_decay, chunk_decay


def _s5_tables(lam_re, lam_im, log_dt, b_re, b_im, c_re, c_im):
    G, P, C = S5_GROUPS, S5_STATE, S5_GROUP
    dt = jnp.exp(log_dt)[:, None]
    mag = jnp.exp(lam_re * dt)
    abar_re, abar_im = mag * jnp.cos(lam_im * dt), mag * jnp.sin(lam_im * dt)
    den = lam_re * lam_re + lam_im * lam_im
    num_re, num_im = abar_re - 1.0, abar_im
    f_re = (num_re * lam_re + num_im * lam_im) / den
    f_im = (num_im * lam_re - num_re * lam_im) / den
    bbar_re = f_re[..., None] * b_re - f_im[..., None] * b_im
    bbar_im = f_re[..., None] * b_im + f_im[..., None] * b_re
    eye = jnp.eye(G, dtype=F32)
    bd_in = lambda t: jnp.einsum('gpc,gh->gchp', t, eye).reshape(G * C, G * P)
    bd_out = lambda t: jnp.einsum('gcp,gh->gphc', t, eye).reshape(G * P, G * C)
    bmat = jnp.concatenate([bd_in(bbar_re), bd_in(bbar_im)], axis=1)
    cmat = jnp.concatenate([bd_out(c_re), -bd_out(c_im)], axis=0)
    return abar_re.reshape(1, G * P), abar_im.reshape(1, G * P), bmat.astype(BF16), cmat.astype(BF16)


def _pad_lanes(t, width=LANES):
    return jnp.pad(t, ((0, 0), (0, width - t.shape[-1])))


def kernel(x, norm_mix_pre, norm_mix_post, w_in, m_conv_w, m_conv_b, m_i_bias, m_f_bias, m_norm, r_mu, r_w0, r_w_up, r_a0, r_a_up, r_g_up, r_k_k, r_k_a, r_r_k, r_ln_g, r_ln_b, t_gn_g, t_gn_b, s_lam_re, s_lam_im, s_log_dt, s_b_re, s_b_im, s_c_re, s_c_im, s_d, s_w_glu, s_b_glu, w_out, norm_ffn_pre, norm_ffn_post, w_ff1, w_ff2):
    bsz, seq, _ = x.shape
    depth = w_in.shape[0]
    assert seq % TM_PROJ == 0 and seq % M_BLOCK == 0 and seq % R_BLOCK == 0 and seq % S5_T == 0
    assert bsz % SUBLANES == 0
    row = lambda t: t.reshape(1, -1)
    m_cols = 4 * GW + 2 * NH
    o_r = m_cols
    o_t = o_r + 4 * GW
    o_s = o_t + 4 * GW

    cos, sin_a, sin_b = _rope_tables(seq)
    dm, qd, kd, cd = _retention_tables()

    x2 = x.reshape(bsz * seq, D_MODEL)
    for l in range(depth):
        wl = w_in[l]
        wm = wl[:, 0:4 * GW].astype(BF16)
        wg = _pad_lanes(wl[:, 4 * GW:m_cols]).astype(BF16)
        wr = wl[:, o_r:o_t].astype(BF16)
        wt = wl[:, o_t:o_s].astype(BF16)
        ws = wl[:, o_s:].astype(BF16)
        pm, pg, pr, pt, ps_tm = _inproj(x2, row(norm_mix_pre[l]), wm, wg, wr, wt, ws, bsz, seq)

        gate_bias = _pad_lanes(jnp.concatenate([m_i_bias[l], m_f_bias[l]])[None, :])
        y_m = _mlstm(pm, pg, m_conv_w[l], row(m_conv_b[l]), gate_bias, row(m_norm[l]), bsz, seq)

        zeros = jnp.zeros((DH, GW), F32)
        w_comb = jnp.concatenate([jnp.concatenate([r_w_up[l], zeros], axis=1),
                                  jnp.concatenate([zeros, r_a_up[l]], axis=1)], axis=0)
        y_r = _rwkv(pr, row(r_mu[l]), row(r_w0[l]), row(r_a0[l]), w_comb, r_g_up[l], row(r_k_k[l]),
                    row(r_k_a[l]), row(r_r_k[l]), row(r_ln_g[l]), row(r_ln_b[l]), bsz, seq)

        y_t = _retention(pt, cos, sin_a, sin_b, dm, qd, kd, cd, row(t_gn_g[l]), row(t_gn_b[l]), bsz, seq)

        ar, ai, bmat, cmat = _s5_tables(s_lam_re[l], s_lam_im[l], s_log_dt[l], s_b_re[l], s_b_im[l],
                                        s_c_re[l], s_c_im[l])
        y_s_tm = _s5(ps_tm.reshape(seq * bsz, GW), ar, ai, bmat, cmat, row(s_d[l]),
                     s_w_glu[l].astype(BF16), row(s_b_glu[l]), bsz, seq)

        x2 = _outffn(x2, y_m, y_r, y_t, y_s_tm.reshape(seq, bsz * GW), w_out[l].astype(BF16),
                     row(norm_mix_post[l]), row(norm_ffn_pre[l]), row(norm_ffn_post[l]),
                     w_ff1[l].astype(BF16), w_ff2[l].astype(BF16), bsz, seq)
    return x2.reshape(bsz, seq, D_MODEL)
```

```python
import jax
import jax.numpy as jnp
from jax import lax
from jax.experimental import pallas as pl
from jax.experimental.pallas import tpu as pltpu

F32 = jnp.float32
BF16 = jnp.bfloat16

D_MODEL = 1024
GW = 256
NH = 4
DH = 64
HEAD_SHIFT = 6
D_FF = 4 * D_MODEL
S5_GROUPS = 16
S5_GROUP = 16
S5_STATE = 64
S5_N = S5_GROUPS * S5_STATE
CONV_WIDTH = 4
ROPE_BASE = 10000.0
RMS_EPS = 1e-6
HEAD_LN_EPS = 1e-5
RWKV_LN_EPS = 64e-5

SUBLANES = 8
LANES = 128
M_CHUNK = 128
M_BLOCK = 1024
R_CHUNK = 64
R_BLOCK = 1024
S5_T = 128
TM_PROJ = 512
VMEM_LIMIT = 56 * 1024 * 1024


def _cparams(sem):
    return pltpu.CompilerParams(dimension_semantics=sem, vmem_limit_bytes=VMEM_LIMIT)


def _dot(a, b, precision=None):
    return jnp.dot(a, b, preferred_element_type=F32, precision=precision)


def _dot_nt(a, b, precision=None):
    return lax.dot_general(a, b, (((1,), (1,)), ((), ())), preferred_element_type=F32, precision=precision)


def _dot_tn(a, b, precision=None):
    return lax.dot_general(a, b, (((0,), (0,)), ((), ())), preferred_element_type=F32, precision=precision)


_FORMS = {"nn": _dot, "nt": _dot_nt, "tn": _dot_tn}


def _bdot(a, b, form="nn"):
    return _FORMS[form](a.astype(BF16), b.astype(BF16))


def _split_bf16(a, terms):
    parts = []
    for _ in range(terms - 1):
        hi = a.astype(BF16)
        parts.append(hi)
        a = a - hi.astype(F32)
    parts.append(a.astype(BF16))
    return parts


def _sel_dot(sel, b, terms):
    sel = sel.astype(BF16)
    return sum(_dot(sel, p) for p in _split_bf16(b, terms))


def _sigmoid(x):
    return 1.0 / (1.0 + jnp.exp(-x))


def _softplus(x):
    return jnp.maximum(x, 0.0) + jnp.log(1.0 + jnp.exp(-jnp.abs(x)))


def _same_head_matrix(scale):
    r = lax.broadcasted_iota(jnp.int32, (GW, GW), 0)
    c = lax.broadcasted_iota(jnp.int32, (GW, GW), 1)
    return jnp.where((r >> HEAD_SHIFT) == (c >> HEAD_SHIFT), scale, 0.0).astype(F32)


def _head_norm(x, eps):
    avg = _same_head_matrix(1.0 / DH)
    mu = _bdot(x, avg)
    xc = x - mu
    var = _bdot(xc * xc, avg)
    return xc * lax.rsqrt(var + eps)


def _rms(x, g):
    return x * lax.rsqrt(jnp.mean(x * x, axis=-1, keepdims=True) + RMS_EPS) * g


def _inproj_kernel(x_ref, g_ref, wm_ref, wg_ref, wr_ref, wt_ref, ws_ref,
                   pm_ref, pg_ref, pr_ref, pt_ref, ps_ref):
    h = _rms(x_ref[...], g_ref[...]).astype(BF16)
    pm_ref[...] = _dot(h, wm_ref[...])
    pg_ref[...] = _dot(h, wg_ref[...])
    pr_ref[...] = _dot(h, wr_ref[...])
    pt_ref[...] = _dot(h, wt_ref[...])
    ps_ref[...] = _dot(h, ws_ref[...])


def _inproj(x2, g, wm, wg, wr, wt, ws, bsz, seq):
    n = bsz * seq
    tm = TM_PROJ
    nt = seq // tm
    row = lambda b, i: (b * nt + i, 0)
    const = lambda b, i: (0, 0)
    wspec = lambda w: pl.BlockSpec(w.shape, const, pipeline_mode=pl.Buffered(1))
    return pl.pallas_call(
        _inproj_kernel,
        grid=(bsz, nt),
        in_specs=[pl.BlockSpec((tm, D_MODEL), row), pl.BlockSpec((1, D_MODEL), const),
                  wspec(wm), wspec(wg), wspec(wr), wspec(wt), wspec(ws)],
        out_specs=[pl.BlockSpec((tm, 4 * GW), row), pl.BlockSpec((tm, LANES), row),
                   pl.BlockSpec((tm, 4 * GW), row), pl.BlockSpec((tm, 4 * GW), row),
                   pl.BlockSpec((tm, GW), row)],
        out_shape=[jax.ShapeDtypeStruct((n, 4 * GW), F32), jax.ShapeDtypeStruct((n, LANES), F32),
                   jax.ShapeDtypeStruct((n, 4 * GW), F32), jax.ShapeDtypeStruct((n, 4 * GW), F32),
                   jax.ShapeDtypeStruct((n, GW), F32)],
        compiler_params=_cparams(("parallel", "parallel")),
        name="inproj",
    )(x2, g, wm, wg, wr, wt, ws)


def _mlstm_kernel(pm_ref, pg_ref, cw_ref, cb_ref, gb_ref, ng_ref, y_ref, xs_ref, ct_ref, n_ref, m_ref):
    MB, L = M_BLOCK, M_CHUNK
    nc = MB // L

    @pl.when(pl.program_id(1) == 0)
    def _():
        xs_ref[0:SUBLANES, :] = jnp.zeros((SUBLANES, 2 * GW), F32)
        ct_ref[...] = jnp.zeros_like(ct_ref)
        n_ref[...] = jnp.zeros_like(n_ref)
        m_ref[...] = jnp.zeros_like(m_ref)

    xqk = pm_ref[:, 0:2 * GW]
    xs_ref[SUBLANES:SUBLANES + MB, :] = xqk
    conv = cb_ref[...]
    for j in range(CONV_WIDTH):
        conv = conv + xs_ref[pl.ds(SUBLANES - (CONV_WIDTH - 1) + j, MB), :] * cw_ref[j:j + 1, :]
    xs_ref[0:SUBLANES, :] = xqk[MB - SUBLANES:MB, :]
    qk = conv * _sigmoid(conv)
    q_t = qk[:, 0:GW].T
    k_all = qk[:, GW:2 * GW] * (DH ** -0.5)
    v_t = pm_ref[:, 2 * GW:3 * GW].T

    lane = lax.broadcasted_iota(jnp.int32, (1, LANES), 1)
    g2 = pg_ref[...] + gb_ref[...]
    lf = jnp.where((lane >= NH) & (lane < 2 * NH), -_softplus(-g2), 0.0)
    ri = lax.broadcasted_iota(jnp.int32, (L, L), 0)
    ci = lax.broadcasted_iota(jnp.int32, (L, L), 1)
    tril = jnp.where(ri >= ci, 1.0, 0.0)
    gcum = jnp.concatenate([_sel_dot(tril, lf[c * L:(c + 1) * L], 2) for c in range(nc)], axis=0)
    colmat = jnp.where(lane < NH, g2, gcum)
    dcol = colmat - pltpu.roll(colmat, LANES - NH, 1)
    rowmat = colmat.T
    g_rows = rowmat[NH:2 * NH]
    c_rows = rowmat[0:NH] - g_rows

    causal = ri <= ci
    blocks = [(c, h) for c in range(nc) for h in range(NH)]
    tsl = lambda c: slice(c * L, (c + 1) * L)
    hsl = lambda h: slice(h * DH, (h + 1) * DH)
    k_b = {b: k_all[tsl(b[0]), hsl(b[1])].astype(BF16) for b in blocks}
    q_b = {b: q_t[hsl(b[1]), tsl(b[0])].astype(BF16) for b in blocks}
    v_b = {b: v_t[hsl(b[1]), tsl(b[0])] for b in blocks}
    g_row = {b: g_rows[b[1]:b[1] + 1, tsl(b[0])] for b in blocks}

    m_loc, num_loc, den_loc = {}, {}, {}
    for b in blocks:
        c, h = b
        d_t = jnp.where(causal, g_row[b] + dcol[tsl(c), h:h + 1], -jnp.inf)
        m_loc[b] = jnp.max(d_t, axis=0, keepdims=True)
        s_t = _dot(k_b[b], q_b[b]) * jnp.exp(d_t - m_loc[b])
        num_loc[b] = _dot(v_b[b].astype(BF16), s_t.astype(BF16))
        den_loc[b] = jnp.sum(s_t, axis=0, keepdims=True)

    ct = [ct_ref[h] for h in range(NH)]
    n8 = [n_ref[h] for h in range(NH)]
    m_st = [m_ref[h][:, 0:1] for h in range(NH)]
    h_rows = []
    for c in range(nc):
        h_cols = []
        for h in range(NH):
            b = (c, h)
            d_inter = g_row[b] + m_st[h]
            m_row = jnp.maximum(d_inter, m_loc[b])
            e_loc = jnp.exp(m_loc[b] - m_row)
            w_inter = jnp.exp(d_inter - m_row)
            num = e_loc * num_loc[b] + w_inter * _dot(ct[h].astype(BF16), q_b[b])
            den = e_loc * den_loc[b] + w_inter * _dot(n8[h].astype(BF16), q_b[b])[0:1]
            h_cols.append(num / jnp.maximum(jnp.abs(den), jnp.exp(-m_row)))

            g_last = g_row[b][:, L - 1:L]
            d_state = g_last + c_rows[h:h + 1, tsl(c)]
            m_new = jnp.maximum(g_last + m_st[h], jnp.max(d_state, axis=1, keepdims=True))
            w_row = jnp.exp(d_state - m_new)
            cs = jnp.exp(g_last + m_st[h] - m_new)
            ct[h] = cs * ct[h] + _dot((v_b[b] * w_row).astype(BF16), k_b[b])
            n8[h] = cs * n8[h] + _dot(jnp.broadcast_to(w_row, (SUBLANES, L)).astype(BF16), k_b[b])
            m_st[h] = m_new
        h_rows.append(jnp.concatenate(h_cols, axis=0))
    for h in range(NH):
        ct_ref[h] = ct[h]
        n_ref[h] = n8[h]
        m_ref[h] = jnp.broadcast_to(m_st[h], (1, LANES))
    hh = jnp.concatenate(h_rows, axis=1).T
    og = _sigmoid(pm_ref[:, 3 * GW:4 * GW])
    y_ref[...] = _head_norm(hh, HEAD_LN_EPS) * ng_ref[...] * og


def _mlstm(pm, pg, cw, cb, gb, ng, bsz, seq):
    MB = M_BLOCK
    nb = seq // MB
    row = lambda b, i: (b * nb + i, 0)
    const = lambda b, i: (0, 0)
    return pl.pallas_call(
        _mlstm_kernel,
        grid=(bsz, nb),
        in_specs=[pl.BlockSpec((MB, 4 * GW), row), pl.BlockSpec((MB, LANES), row),
                  pl.BlockSpec(cw.shape, const), pl.BlockSpec(cb.shape, const),
                  pl.BlockSpec(gb.shape, const), pl.BlockSpec(ng.shape, const)],
        out_specs=pl.BlockSpec((MB, GW), row),
        out_shape=jax.ShapeDtypeStruct((bsz * seq, GW), F32),
        scratch_shapes=[pltpu.VMEM((SUBLANES + MB, 2 * GW), F32), pltpu.VMEM((NH, DH, DH), F32),
                        pltpu.VMEM((NH, SUBLANES, DH), F32), pltpu.VMEM((NH, 1, LANES), F32)],
        compiler_params=_cparams(("parallel", "arbitrary")),
        name="mlstm",
    )(pm, pg, cw, cb, gb, ng)


def _to_blocks(x, rows):
    return jnp.stack([x[c * rows:(c + 1) * rows, h * DH:(h + 1) * DH]
                      for c in range(x.shape[0] // rows) for h in range(NH)], axis=0)


def _from_heads(x):
    return jnp.concatenate([x[h] for h in range(NH)], axis=-1)


def _bmm(a, b):
    return jnp.einsum('hij,hjk->hik', a.astype(BF16), b.astype(BF16), preferred_element_type=F32)


def _bmm_nt(a, b):
    return jnp.einsum('hik,hjk->hij', a.astype(BF16), b.astype(BF16), preferred_element_type=F32)


def _bmm_tn(a, b):
    return jnp.einsum('hki,hkj->hij', a.astype(BF16), b.astype(BF16), preferred_element_type=F32)


def _unit_lower_inverse(a, ri, ci, log2_n):
    same = lambda s: (ri >> s) == (ci >> s)
    eye = jnp.where(ri == ci, 1.0, 0.0).astype(F32)
    a8 = jnp.where(same(3), a, 0.0)
    a8_2 = _bmm(a8, a8)
    a8_4 = _bmm(a8_2, a8_2)
    inv = _bmm(_bmm(eye - a8, eye + a8_2), eye + a8_4)
    for s in range(3, log2_n):
        off = jnp.where(same(s + 1) & jnp.logical_not(same(s)), a, 0.0)
        inv = inv - _bmm(inv, _bmm(off, inv))
    return inv


def _rwkv_kernel(pr_ref, mu_ref, w0_ref, a0_ref, wc_ref, gup_ref, kk_ref, ka_ref, rk_ref, lg_ref, lb_ref,
                 y_ref, xs_ref, st_ref):
    RB, L = R_BLOCK, R_CHUNK
    log2_l = L.bit_length() - 1

    @pl.when(pl.program_id(1) == 0)
    def _():
        xs_ref[0:SUBLANES, :] = jnp.zeros((SUBLANES, 4 * GW), F32)
        st_ref[...] = jnp.zeros_like(st_ref)

    pr = pr_ref[...]
    xs_ref[SUBLANES:SUBLANES + RB, :] = pr
    shifted = xs_ref[pl.ds(SUBLANES - 1, RB), :]
    xs_ref[0:SUBLANES, :] = pr[RB - SUBLANES:RB, :]
    prm = pr + mu_ref[...] * (shifted - pr)

    r = prm[:, 0:GW]
    k = prm[:, GW:2 * GW]
    v = prm[:, 2 * GW:3 * GW]
    z = prm[:, 3 * GW:3 * GW + LANES]
    lane = lax.broadcasted_iota(jnp.int32, (1, LANES), 1)
    wa = _bdot(jnp.where(lane < DH, jnp.tanh(z), z), wc_ref[...])
    w_log = -_softplus(-(w0_ref[...] + wa[:, 0:GW])) - 0.5
    logw = -jnp.exp(w_log)
    a = _sigmoid(a0_ref[...] + wa[:, GW:2 * GW])
    g = _bdot(_sigmoid(prm[:, 3 * GW + LANES:4 * GW]), gup_ref[...])

    ones_bd = _same_head_matrix(1.0)
    kk = k * kk_ref[...]
    kk = kk / jnp.maximum(jnp.sqrt(_bdot(kk * kk, ones_bd)), 1e-12)
    k2 = k * (1.0 + (a - 1.0) * ka_ref[...])
    bonus = _bdot(r * k2 * rk_ref[...], ones_bd) * v
    b = kk * a

    nc = RB // L
    rl = lax.broadcasted_iota(jnp.int32, (L, L), 0)
    cl = lax.broadcasted_iota(jnp.int32, (L, L), 1)
    tril = jnp.where(rl >= cl, 1.0, 0.0)
    cum_c = [_sel_dot(tril, logw[c * L:(c + 1) * L], 2) for c in range(nc)]
    cum = jnp.concatenate(cum_c, axis=0)
    tot_c = [cc[L - 1:L] for cc in cum_c]
    tot = jnp.concatenate([jnp.broadcast_to(t, (L, GW)) for t in tot_c], axis=0)
    p_inv = jnp.exp(-cum)
    p_end = jnp.exp(tot - cum)
    kkd = _to_blocks(kk * jnp.exp(cum - logw), L).astype(BF16)
    rd = _to_blocks(r * jnp.exp(cum), L)
    bi = _to_blocks(b * p_inv, L).astype(BF16)
    ki = _to_blocks(k2 * p_inv, L).astype(BF16)
    be = _to_blocks(b * p_end, L).astype(BF16)
    ke = _to_blocks(k2 * p_end, L).astype(BF16)
    vh = _to_blocks(v, L).astype(BF16)
    p_tot = _to_blocks(jnp.concatenate([jnp.exp(t) for t in tot_c], axis=0), 1)

    strict = rl > cl
    incl = rl >= cl

    kr = jnp.concatenate([kkd, rd.astype(BF16)], axis=1)
    xb = _bmm_nt(kr, bi)
    xk = _bmm_nt(kr, ki)
    a_b = jnp.where(strict, xb[:, 0:L], 0.0)
    b_b = jnp.where(incl, xb[:, L:2 * L], 0.0)
    a_k = jnp.where(strict, xk[:, 0:L], 0.0)
    b_k = jnp.where(incl, xk[:, L:2 * L], 0.0)
    t_inv = _unit_lower_inverse(a_b, rl, cl, log2_l)
    w_q = _bmm(t_inv, kkd)
    u_0 = _bmm(t_inv, _bmm(a_k, vh))
    q_eff = rd - _bmm(b_b, w_q)
    y_0 = _bmm(b_k, vh) - _bmm(b_b, u_0)
    m_t = jnp.where(rl == cl, p_tot, 0.0) - _bmm_tn(be, w_q)
    n_t = _bmm_tn(ke, vh) - _bmm_tn(be, u_0)

    st = st_ref[...]
    ys = []
    for c in range(nc):
        blk = slice(c * NH, (c + 1) * NH)
        ys.append(_from_heads(_bmm(q_eff[blk], st) + y_0[blk]))
        st = _bmm(m_t[blk], st) + n_t[blk]
    st_ref[...] = st

    y = _head_norm(jnp.concatenate(ys, axis=0), RWKV_LN_EPS) * lg_ref[...] + lb_ref[...]
    y_ref[...] = (y + bonus) * g


def _rwkv(pr, mu, w0, a0, wc, gup, kk, ka, rk, lg, lb, bsz, seq):
    RB = R_BLOCK
    nb = seq // RB
    row = lambda b, i: (b * nb + i, 0)
    const = lambda b, i: (0, 0)
    params = (mu, w0, a0, wc, gup, kk, ka, rk, lg, lb)
    return pl.pallas_call(
        _rwkv_kernel,
        grid=(bsz, nb),
        in_specs=[pl.BlockSpec((RB, 4 * GW), row)] + [pl.BlockSpec(p.shape, const) for p in params],
        out_specs=pl.BlockSpec((RB, GW), row),
        out_shape=jax.ShapeDtypeStruct((bsz * seq, GW), F32),
        scratch_shapes=[pltpu.VMEM((SUBLANES + RB, 4 * GW), F32), pltpu.VMEM((NH, DH, DH), F32)],
        compiler_params=_cparams(("parallel", "arbitrary")),
        name="rwkv",
    )(pr, *params)


def _ret_kernel(pt_ref, cos_ref, sa_ref, sb_ref, dm_ref, qd_ref, kd_ref, cd_ref, gg_ref, gb_ref,
                y_ref, rt_ref):
    MB, L = M_BLOCK, M_CHUNK
    nc = MB // L

    @pl.when(pl.program_id(1) == 0)
    def _():
        rt_ref[...] = jnp.zeros_like(rt_ref)

    cos = cos_ref[...]
    sa = sa_ref[...]
    sb = sb_ref[...]

    def rope(t):
        return t * cos + pltpu.roll(t, GW - DH // 2, 1) * sa + pltpu.roll(t, DH // 2, 1) * sb

    q_t = rope(pt_ref[:, 0:GW]).T
    k_all = rope(pt_ref[:, GW:2 * GW]) * (DH ** -0.5)
    v_t = pt_ref[:, 2 * GW:3 * GW].T

    blocks = [(c, h) for c in range(nc) for h in range(NH)]
    q_b, o_loc, r_add = {}, {}, {}
    for b in blocks:
        c, h = b
        tsl, hsl = slice(c * L, (c + 1) * L), slice(h * DH, (h + 1) * DH)
        k_b = k_all[tsl, hsl].astype(BF16)
        q_b[b] = q_t[hsl, tsl].astype(BF16)
        v_b = v_t[hsl, tsl]
        s_t = _dot(k_b, q_b[b]) * dm_ref[h]
        o_loc[b] = _dot(v_b.astype(BF16), s_t.astype(BF16))
        r_add[b] = _dot((v_b * kd_ref[h:h + 1, :]).astype(BF16), k_b)

    rt = [rt_ref[h] for h in range(NH)]
    o_rows = []
    for c in range(nc):
        o_cols = []
        for h in range(NH):
            b = (c, h)
            o_cols.append(o_loc[b] + _dot(rt[h].astype(BF16), q_b[b]) * qd_ref[h:h + 1, :])
            rt[h] = rt[h] * cd_ref[h] + r_add[b]
        o_rows.append(jnp.concatenate(o_cols, axis=0))
    for h in range(NH):
        rt_ref[h] = rt[h]
    o = jnp.concatenate(o_rows, axis=1).T
    gate = pt_ref[:, 3 * GW:4 * GW]
    y_ref[...] = gate * _sigmoid(gate) * (_head_norm(o, HEAD_LN_EPS) * gg_ref[...] + gb_ref[...])


def _retention(pt, cos, sa, sb, dm, qd, kd, cd, gg, gb, bsz, seq):
    MB = M_BLOCK
    nb = seq // MB
    row = lambda b, i: (b * nb + i, 0)
    pos = lambda b, i: (i, 0)
    const2 = lambda b, i: (0, 0)
    const3 = lambda b, i: (0, 0, 0)
    return pl.pallas_call(
        _ret_kernel,
        grid=(bsz, nb),
        in_specs=[pl.BlockSpec((MB, 4 * GW), row),
                  pl.BlockSpec((MB, GW), pos), pl.BlockSpec((MB, GW), pos), pl.BlockSpec((MB, GW), pos),
                  pl.BlockSpec(dm.shape, const3), pl.BlockSpec(qd.shape, const2), pl.BlockSpec(kd.shape, const2),
                  pl.BlockSpec(cd.shape, const3), pl.BlockSpec(gg.shape, const2), pl.BlockSpec(gb.shape, const2)],
        out_specs=pl.BlockSpec((MB, GW), row),
        out_shape=jax.ShapeDtypeStruct((bsz * seq, GW), F32),
        scratch_shapes=[pltpu.VMEM((NH, DH, DH), F32)],
        compiler_params=_cparams(("parallel", "arbitrary")),
        name="retention",
    )(pt, cos, sa, sb, dm, qd, kd, cd, gg, gb)


def _gelu_tanh(x):
    return 0.5 * x * (1.0 + jnp.tanh(0.7978845608028654 * (x + 0.044715 * (x * x * x))))


def _s5_kernel(u_ref, ar_ref, ai_ref, bm_ref, cm_ref, d_ref, wg_ref, bg_ref, y_ref, bu_ref, x_ref):
    nb, T, _ = u_ref.shape

    @pl.when(pl.program_id(0) == 0)
    def _():
        x_ref[...] = jnp.zeros_like(x_ref)

    u = jnp.swapaxes(u_ref[...], 0, 1).reshape(T * nb, GW)
    bu_ref[...] = _dot(u.astype(BF16), bm_ref[...])
    ar = jnp.broadcast_to(ar_ref[...], (nb, S5_N))
    ai = jnp.broadcast_to(ai_ref[...], (nb, S5_N))

    def step(t, carry):
        xr, xi = carry
        r0 = pl.multiple_of(t * nb, nb)
        nr = ar * xr - ai * xi + bu_ref[pl.ds(r0, nb), 0:S5_N]
        ni = ar * xi + ai * xr + bu_ref[pl.ds(r0, nb), S5_N:2 * S5_N]
        bu_ref[pl.ds(r0, nb), 0:S5_N] = nr
        bu_ref[pl.ds(r0, nb), S5_N:2 * S5_N] = ni
        return nr, ni

    xr, xi = lax.fori_loop(0, T, step, (x_ref[:, 0:S5_N], x_ref[:, S5_N:2 * S5_N]), unroll=True)
    x_ref[:, 0:S5_N] = xr
    x_ref[:, S5_N:2 * S5_N] = xi

    y = _dot(bu_ref[...].astype(BF16), cm_ref[...])
    y = _gelu_tanh(y + d_ref[...] * u)
    z = _dot(y.astype(BF16), wg_ref[...]) + bg_ref[...]
    out = z[:, 0:GW] * _sigmoid(z[:, GW:2 * GW])
    y_ref[...] = jnp.swapaxes(out.reshape(T, nb, GW), 0, 1)


def _s5(ps, ar, ai, bm, cm, d, wg, bg, bsz, seq):
    T = S5_T
    const = lambda i: (0, 0)
    blk = pl.BlockSpec((bsz, T, GW), lambda i: (0, i, 0))
    params = (ar, ai, bm, cm, d, wg, bg)
    return pl.pallas_call(
        _s5_kernel,
        grid=(seq // T,),
        in_specs=[blk] + [pl.BlockSpec(p.shape, const) for p in params],
        out_specs=blk,
        out_shape=jax.ShapeDtypeStruct((bsz, seq, GW), F32),
        scratch_shapes=[pltpu.VMEM((T * bsz, 2 * S5_N), F32), pltpu.VMEM((bsz, 2 * S5_N), F32)],
        compiler_params=_cparams(("arbitrary",)),
        name="s5",
    )(ps.reshape(bsz, seq, GW), *params).reshape(bsz * seq, GW)


def _outffn_kernel(x_ref, ym_ref, yr_ref, yt_ref, ys_ref, wo_ref, g1_ref, g2_ref, g3_ref, w1_ref, w2_ref, o_ref):
    y = _dot(ym_ref[...].astype(BF16), wo_ref[0:GW, :])
    y = y + _dot(yr_ref[...].astype(BF16), wo_ref[GW:2 * GW, :])
    y = y + _dot(yt_ref[...].astype(BF16), wo_ref[2 * GW:3 * GW, :])
    y = y + _dot(ys_ref[...].astype(BF16), wo_ref[3 * GW:4 * GW, :])
    x1 = x_ref[...] + _rms(y, g1_ref[...])
    h = _rms(x1, g2_ref[...]).astype(BF16)
    f = jnp.zeros_like(x1)
    for j in range(D_FF // D_MODEL):
        a = jnp.maximum(_dot(h, w1_ref[:, j * D_MODEL:(j + 1) * D_MODEL]), 0.0)
        f = f + _dot((a * a).astype(BF16), w2_ref[j * D_MODEL:(j + 1) * D_MODEL, :])
    o_ref[...] = x1 + _rms(f, g3_ref[...])


def _outffn(x2, ym, yr, yt, ys, wo, g1, g2, g3, w1, w2, bsz, seq):
    tm = TM_PROJ
    nt = seq // tm
    row = lambda b, i: (b * nt + i, 0)
    const = lambda b, i: (0, 0)
    wspec = lambda w: pl.BlockSpec(w.shape, const, pipeline_mode=pl.Buffered(1))
    return pl.pallas_call(
        _outffn_kernel,
        grid=(bsz, nt),
        in_specs=[pl.BlockSpec((tm, D_MODEL), row),
                  pl.BlockSpec((tm, GW), row), pl.BlockSpec((tm, GW), row), pl.BlockSpec((tm, GW), row),
                  pl.BlockSpec((tm, GW), row),
                  wspec(wo), pl.BlockSpec((1, D_MODEL), const), pl.BlockSpec((1, D_MODEL), const),
                  pl.BlockSpec((1, D_MODEL), const), wspec(w1), wspec(w2)],
        out_specs=pl.BlockSpec((tm, D_MODEL), row),
        out_shape=jax.ShapeDtypeStruct((bsz * seq, D_MODEL), F32),
        compiler_params=_cparams(("parallel", "parallel")),
        name="outffn",
    )(x2, ym, yr, yt, ys, wo, g1, g2, g3, w1, w2)


def _rope_tables(seq):
    half = DH // 2
    inv = ROPE_BASE ** (-jnp.arange(half, dtype=F32) / half)
    ang = jnp.arange(seq, dtype=F32)[:, None] * inv[None, :]
    cos, sin = jnp.cos(ang), jnp.sin(ang)
    zero = jnp.zeros_like(sin)
    tile = lambda first, second: jnp.tile(jnp.concatenate([first, second], axis=-1), (1, NH))
    return tile(cos, cos), tile(-sin, zero), tile(zero, sin)


def _retention_tables():
    L = M_CHUNK
    log_gamma = jnp.log(1.0 - 2.0 ** (-5.0 - jnp.arange(NH, dtype=F32)))
    idx = jnp.arange(L, dtype=F32)
    diff = idx[:, None] - idx[None, :]
    causal = diff >= 0
    decay_mat = jnp.where(causal, jnp.exp(jnp.where(causal, diff, 0.0) * log_gamma[:, None, None]), 0.0)
    decay_t = jnp.swapaxes(decay_mat, 1, 2)
    q_decay = jnp.exp((idx + 1.0) * log_gamma[:, None])
    k_decay = jnp.exp((L - 1.0 - idx) * log_gamma[:, None])
    chunk_decay = jnp.broadcast_to(jnp.exp(L * log_gamma)[:, None, None], (NH, 1, DH))
    return decay_t, q_decay, k_decay, chunk_decay


def _s5_tables(lam_re, lam_im, log_dt, b_re, b_im, c_re, c_im):
    G, P, C = S5_GROUPS, S5_STATE, S5_GROUP
    dt = jnp.exp(log_dt)[:, None]
    mag = jnp.exp(lam_re * dt)
    abar_re, abar_im = mag * jnp.cos(lam_im * dt), mag * jnp.sin(lam_im * dt)
    den = lam_re * lam_re + lam_im * lam_im
    num_re, num_im = abar_re - 1.0, abar_im
    f_re = (num_re * lam_re + num_im * lam_im) / den
    f_im = (num_im * lam_re - num_re * lam_im) / den
    bbar_re = f_re[..., None] * b_re - f_im[..., None] * b_im
    bbar_im = f_re[..., None] * b_im + f_im[..., None] * b_re
    eye = jnp.eye(G, dtype=F32)
    bd_in = lambda t: jnp.einsum('gpc,gh->gchp', t, eye).reshape(G * C, G * P)
    bd_out = lambda t: jnp.einsum('gcp,gh->gphc', t, eye).reshape(G * P, G * C)
    bmat = jnp.concatenate([bd_in(bbar_re), bd_in(bbar_im)], axis=1)
    cmat = jnp.concatenate([bd_out(c_re), -bd_out(c_im)], axis=0)
    return abar_re.reshape(1, G * P), abar_im.reshape(1, G * P), bmat.astype(BF16), cmat.astype(BF16)


def _pad_lanes(t, width=LANES):
    return jnp.pad(t, ((0, 0), (0, width - t.shape[-1])))


def kernel(x, norm_mix_pre, norm_mix_post, w_in, m_conv_w, m_conv_b, m_i_bias, m_f_bias, m_norm, r_mu, r_w0, r_w_up, r_a0, r_a_up, r_g_up, r_k_k, r_k_a, r_r_k, r_ln_g, r_ln_b, t_gn_g, t_gn_b, s_lam_re, s_lam_im, s_log_dt, s_b_re, s_b_im, s_c_re, s_c_im, s_d, s_w_glu, s_b_glu, w_out, norm_ffn_pre, norm_ffn_post, w_ff1, w_ff2):
    bsz, seq, _ = x.shape
    depth = w_in.shape[0]
    assert seq % TM_PROJ == 0 and seq % M_BLOCK == 0 and seq % R_BLOCK == 0 and seq % S5_T == 0
    assert bsz % SUBLANES == 0
    row = lambda t: t.reshape(1, -1)
    m_cols = 4 * GW + 2 * NH
    o_r = m_cols
    o_t = o_r + 4 * GW
    o_s = o_t + 4 * GW

    cos, sin_a, sin_b = _rope_tables(seq)
    dm, qd, kd, cd = _retention_tables()

    x2 = x.reshape(bsz * seq, D_MODEL)
    for l in range(depth):
        wl = w_in[l]
        wm = wl[:, 0:4 * GW].astype(BF16)
        wg = _pad_lanes(wl[:, 4 * GW:m_cols]).astype(BF16)
        wr = wl[:, o_r:o_t].astype(BF16)
        wt = wl[:, o_t:o_s].astype(BF16)
        ws = wl[:, o_s:].astype(BF16)
        pm, pg, pr, pt, ps = _inproj(x2, row(norm_mix_pre[l]), wm, wg, wr, wt, ws, bsz, seq)

        gate_bias = _pad_lanes(jnp.concatenate([m_i_bias[l], m_f_bias[l]])[None, :])
        y_m = _mlstm(pm, pg, m_conv_w[l], row(m_conv_b[l]), gate_bias, row(m_norm[l]), bsz, seq)

        zeros = jnp.zeros((DH, GW), F32)
        w_comb = jnp.concatenate([jnp.concatenate([r_w_up[l], zeros], axis=1),
                                  jnp.concatenate([zeros, r_a_up[l]], axis=1)], axis=0)
        y_r = _rwkv(pr, row(r_mu[l]), row(r_w0[l]), row(r_a0[l]), w_comb, r_g_up[l], row(r_k_k[l]),
                    row(r_k_a[l]), row(r_r_k[l]), row(r_ln_g[l]), row(r_ln_b[l]), bsz, seq)

        y_t = _retention(pt, cos, sin_a, sin_b, dm, qd, kd, cd, row(t_gn_g[l]), row(t_gn_b[l]), bsz, seq)

        ar, ai, bmat, cmat = _s5_tables(s_lam_re[l], s_lam_im[l], s_log_dt[l], s_b_re[l], s_b_im[l],
                                        s_c_re[l], s_c_im[l])
        y_s = _s5(ps, ar, ai, bmat, cmat, row(s_d[l]), s_w_glu[l].astype(BF16), row(s_b_glu[l]), bsz, seq)

        x2 = _outffn(x2, y_m, y_r, y_t, y_s, w_out[l].astype(BF16),
                     row(norm_mix_post[l]), row(norm_ffn_pre[l]), row(norm_ffn_post[l]),
                     w_ff1[l].astype(BF16), w_ff2[l].astype(BF16), bsz, seq)
    return x2.reshape(bsz, seq, D_MODEL)
```

```python
import jax
import jax.numpy as jnp
from jax import lax
from jax.experimental import pallas as pl
from jax.experimental.pallas import tpu as pltpu

F32 = jnp.float32
BF16 = jnp.bfloat16

D_MODEL = 1024
GW = 256
NH = 4
DH = 64
HEAD_SHIFT = 6
D_FF = 4 * D_MODEL
S5_GROUPS = 16
S5_GROUP = 16
S5_STATE = 64
S5_N = S5_GROUPS * S5_STATE
CONV_WIDTH = 4
ROPE_BASE = 10000.0
RMS_EPS = 1e-6
HEAD_LN_EPS = 1e-5
RWKV_LN_EPS = 64e-5

SUBLANES = 8
LANES = 128
M_CHUNK = 128
M_BLOCK = 1024
R_CHUNK = 64
R_BLOCK = 1024
S5_T = 128
TM_PROJ = 512
VMEM_LIMIT = 56 * 1024 * 1024


def _cparams(sem):
    return pltpu.CompilerParams(dimension_semantics=sem, vmem_limit_bytes=VMEM_LIMIT)


def _dot(a, b, precision=None):
    return jnp.dot(a, b, preferred_element_type=F32, precision=precision)


def _dot_nt(a, b, precision=None):
    return lax.dot_general(a, b, (((1,), (1,)), ((), ())), preferred_element_type=F32, precision=precision)


def _dot_tn(a, b, precision=None):
    return lax.dot_general(a, b, (((0,), (0,)), ((), ())), preferred_element_type=F32, precision=precision)


_FORMS = {"nn": _dot, "nt": _dot_nt, "tn": _dot_tn}


def _bdot(a, b, form="nn"):
    return _FORMS[form](a.astype(BF16), b.astype(BF16))


def _split_bf16(a, terms):
    parts = []
    for _ in range(terms - 1):
        hi = a.astype(BF16)
        parts.append(hi)
        a = a - hi.astype(F32)
    parts.append(a.astype(BF16))
    return parts


def _sel_dot(sel, b, terms):
    sel = sel.astype(BF16)
    return sum(_dot(sel, p) for p in _split_bf16(b, terms))


def _sigmoid(x):
    return 1.0 / (1.0 + jnp.exp(-x))


def _softplus(x):
    return jnp.maximum(x, 0.0) + jnp.log(1.0 + jnp.exp(-jnp.abs(x)))


def _same_head_matrix(scale):
    r = lax.broadcasted_iota(jnp.int32, (GW, GW), 0)
    c = lax.broadcasted_iota(jnp.int32, (GW, GW), 1)
    return jnp.where((r >> HEAD_SHIFT) == (c >> HEAD_SHIFT), scale, 0.0).astype(F32)


def _head_norm(x, eps):
    avg = _same_head_matrix(1.0 / DH)
    mu = _bdot(x, avg)
    xc = x - mu
    var = _bdot(xc * xc, avg)
    return xc * lax.rsqrt(var + eps)


def _rms(x, g):
    return x * lax.rsqrt(jnp.mean(x * x, axis=-1, keepdims=True) + RMS_EPS) * g


def _inproj_kernel(x_ref, g_ref, wm_ref, wg_ref, wr_ref, wt_ref, ws_ref,
                   pm_ref, pg_ref, pr_ref, pt_ref, ps_ref):
    h = _rms(x_ref[...], g_ref[...]).astype(BF16)
    pm_ref[...] = _dot(h, wm_ref[...])
    pg_ref[...] = _dot(h, wg_ref[...])
    pr_ref[...] = _dot(h, wr_ref[...])
    pt_ref[...] = _dot(h, wt_ref[...])
    ps_ref[...] = _dot(h, ws_ref[...])


def _inproj(x2, g, wm, wg, wr, wt, ws, bsz, seq):
    n = bsz * seq
    tm = TM_PROJ
    nt = seq // tm
    row = lambda b, i: (b * nt + i, 0)
    const = lambda b, i: (0, 0)
    wspec = lambda w: pl.BlockSpec(w.shape, const, pipeline_mode=pl.Buffered(1))
    return pl.pallas_call(
        _inproj_kernel,
        grid=(bsz, nt),
        in_specs=[pl.BlockSpec((tm, D_MODEL), row), pl.BlockSpec((1, D_MODEL), const),
                  wspec(wm), wspec(wg), wspec(wr), wspec(wt), wspec(ws)],
        out_specs=[pl.BlockSpec((tm, 4 * GW), row), pl.BlockSpec((tm, LANES), row),
                   pl.BlockSpec((tm, 4 * GW), row), pl.BlockSpec((tm, 4 * GW), row),
                   pl.BlockSpec((tm, GW), row)],
        out_shape=[jax.ShapeDtypeStruct((n, 4 * GW), F32), jax.ShapeDtypeStruct((n, LANES), F32),
                   jax.ShapeDtypeStruct((n, 4 * GW), F32), jax.ShapeDtypeStruct((n, 4 * GW), F32),
                   jax.ShapeDtypeStruct((n, GW), F32)],
        compiler_params=_cparams(("parallel", "parallel")),
        name="inproj",
    )(x2, g, wm, wg, wr, wt, ws)


def _mlstm_kernel(pm_ref, pg_ref, cw_ref, cb_ref, gb_ref, ng_ref, y_ref, xs_ref, ct_ref, n_ref, m_ref):
    MB, L = M_BLOCK, M_CHUNK
    nc = MB // L

    @pl.when(pl.program_id(1) == 0)
    def _():
        xs_ref[0:SUBLANES, :] = jnp.zeros((SUBLANES, 2 * GW), F32)
        ct_ref[...] = jnp.zeros_like(ct_ref)
        n_ref[...] = jnp.zeros_like(n_ref)
        m_ref[...] = jnp.zeros_like(m_ref)

    xqk = pm_ref[:, 0:2 * GW]
    xs_ref[SUBLANES:SUBLANES + MB, :] = xqk
    conv = cb_ref[...]
    for j in range(CONV_WIDTH):
        conv = conv + xs_ref[pl.ds(SUBLANES - (CONV_WIDTH - 1) + j, MB), :] * cw_ref[j:j + 1, :]
    xs_ref[0:SUBLANES, :] = xqk[MB - SUBLANES:MB, :]
    qk = conv * _sigmoid(conv)
    q_t = qk[:, 0:GW].T
    k_all = qk[:, GW:2 * GW] * (DH ** -0.5)
    v_t = pm_ref[:, 2 * GW:3 * GW].T

    lane = lax.broadcasted_iota(jnp.int32, (1, LANES), 1)
    g2 = pg_ref[...] + gb_ref[...]
    lf = jnp.where((lane >= NH) & (lane < 2 * NH), -_softplus(-g2), 0.0)
    ri = lax.broadcasted_iota(jnp.int32, (L, L), 0)
    ci = lax.broadcasted_iota(jnp.int32, (L, L), 1)
    tril = jnp.where(ri >= ci, 1.0, 0.0)
    gcum = jnp.concatenate([_sel_dot(tril, lf[c * L:(c + 1) * L], 2) for c in range(nc)], axis=0)
    colmat = jnp.where(lane < NH, g2, gcum)
    dcol = colmat - pltpu.roll(colmat, LANES - NH, 1)
    rowmat = colmat.T
    g_rows = rowmat[NH:2 * NH]
    c_rows = rowmat[0:NH] - g_rows

    causal = ri <= ci
    blocks = [(c, h) for c in range(nc) for h in range(NH)]
    tsl = lambda c: slice(c * L, (c + 1) * L)
    hsl = lambda h: slice(h * DH, (h + 1) * DH)
    k_b = {b: k_all[tsl(b[0]), hsl(b[1])].astype(BF16) for b in blocks}
    q_b = {b: q_t[hsl(b[1]), tsl(b[0])].astype(BF16) for b in blocks}
    v_b = {b: v_t[hsl(b[1]), tsl(b[0])] for b in blocks}
    g_row = {b: g_rows[b[1]:b[1] + 1, tsl(b[0])] for b in blocks}

    m_loc, num_loc, den_loc, g_end, m_add, c_add, n_add = {}, {}, {}, {}, {}, {}, {}
    for b in blocks:
        c, h = b
        d_t = jnp.where(causal, g_row[b] + dcol[tsl(c), h:h + 1], -jnp.inf)
        m_loc[b] = jnp.max(d_t, axis=0, keepdims=True)
        s_t = _dot(k_b[b], q_b[b]) * jnp.exp(d_t - m_loc[b])
        num_loc[b] = _dot(v_b[b].astype(BF16), s_t.astype(BF16))
        den_loc[b] = jnp.sum(s_t, axis=0, keepdims=True)
        g_end[b] = g_row[b][:, L - 1:L]
        d_state = g_end[b] + c_rows[h:h + 1, tsl(c)]
        m_add[b] = jnp.max(d_state, axis=1, keepdims=True)
        w_row = jnp.exp(d_state - m_add[b])
        c_add[b] = _dot((v_b[b] * w_row).astype(BF16), k_b[b])
        n_add[b] = _dot(jnp.broadcast_to(w_row, (SUBLANES, L)).astype(BF16), k_b[b])

    ct = [ct_ref[h] for h in range(NH)]
    n8 = [n_ref[h] for h in range(NH)]
    m_st = [m_ref[h][:, 0:1] for h in range(NH)]
    h_rows = []
    for c in range(nc):
        h_cols = []
        for h in range(NH):
            b = (c, h)
            d_inter = g_row[b] + m_st[h]
            m_row = jnp.maximum(d_inter, m_loc[b])
            e_loc = jnp.exp(m_loc[b] - m_row)
            w_inter = jnp.exp(d_inter - m_row)
            num = e_loc * num_loc[b] + w_inter * _dot(ct[h].astype(BF16), q_b[b])
            den = e_loc * den_loc[b] + w_inter * _dot(n8[h].astype(BF16), q_b[b])[0:1]
            h_cols.append(num / jnp.maximum(jnp.abs(den), jnp.exp(-m_row)))

            m_new = jnp.maximum(g_end[b] + m_st[h], m_add[b])
            keep = jnp.exp(g_end[b] + m_st[h] - m_new)
            take = jnp.exp(m_add[b] - m_new)
            ct[h] = keep * ct[h] + take * c_add[b]
            n8[h] = keep * n8[h] + take * n_add[b]
            m_st[h] = m_new
        h_rows.append(jnp.concatenate(h_cols, axis=0))
    for h in range(NH):
        ct_ref[h] = ct[h]
        n_ref[h] = n8[h]
        m_ref[h] = jnp.broadcast_to(m_st[h], (1, LANES))
    hh = jnp.concatenate(h_rows, axis=1).T
    og = _sigmoid(pm_ref[:, 3 * GW:4 * GW])
    y_ref[...] = _head_norm(hh, HEAD_LN_EPS) * ng_ref[...] * og


def _mlstm(pm, pg, cw, cb, gb, ng, bsz, seq):
    MB = M_BLOCK
    nb = seq // MB
    row = lambda b, i: (b * nb + i, 0)
    const = lambda b, i: (0, 0)
    return pl.pallas_call(
        _mlstm_kernel,
        grid=(bsz, nb),
        in_specs=[pl.BlockSpec((MB, 4 * GW), row), pl.BlockSpec((MB, LANES), row),
                  pl.BlockSpec(cw.shape, const), pl.BlockSpec(cb.shape, const),
                  pl.BlockSpec(gb.shape, const), pl.BlockSpec(ng.shape, const)],
        out_specs=pl.BlockSpec((MB, GW), row),
        out_shape=jax.ShapeDtypeStruct((bsz * seq, GW), F32),
        scratch_shapes=[pltpu.VMEM((SUBLANES + MB, 2 * GW), F32), pltpu.VMEM((NH, DH, DH), F32),
                        pltpu.VMEM((NH, SUBLANES, DH), F32), pltpu.VMEM((NH, 1, LANES), F32)],
        compiler_params=_cparams(("parallel", "arbitrary")),
        name="mlstm",
    )(pm, pg, cw, cb, gb, ng)


def _to_blocks(x, rows):
    return jnp.stack([x[c * rows:(c + 1) * rows, h * DH:(h + 1) * DH]
                      for c in range(x.shape[0] // rows) for h in range(NH)], axis=0)


def _from_heads(x):
    return jnp.concatenate([x[h] for h in range(NH)], axis=-1)


def _bmm(a, b):
    return jnp.einsum('hij,hjk->hik', a.astype(BF16), b.astype(BF16), preferred_element_type=F32)


def _bmm_nt(a, b):
    return jnp.einsum('hik,hjk->hij', a.astype(BF16), b.astype(BF16), preferred_element_type=F32)


def _bmm_tn(a, b):
    return jnp.einsum('hki,hkj->hij', a.astype(BF16), b.astype(BF16), preferred_element_type=F32)


def _unit_lower_inverse(a, ri, ci, log2_n):
    same = lambda s: (ri >> s) == (ci >> s)
    eye = jnp.where(ri == ci, 1.0, 0.0).astype(F32)
    a8 = jnp.where(same(3), a, 0.0)
    a8_2 = _bmm(a8, a8)
    a8_4 = _bmm(a8_2, a8_2)
    inv = _bmm(_bmm(eye - a8, eye + a8_2), eye + a8_4)
    for s in range(3, log2_n):
        off = jnp.where(same(s + 1) & jnp.logical_not(same(s)), a, 0.0)
        inv = inv - _bmm(inv, _bmm(off, inv))
    return inv


def _rwkv_kernel(pr_ref, mu_ref, w0_ref, a0_ref, wc_ref, gup_ref, kk_ref, ka_ref, rk_ref, lg_ref, lb_ref,
                 y_ref, xs_ref, st_ref):
    RB, L = R_BLOCK, R_CHUNK
    log2_l = L.bit_length() - 1

    @pl.when(pl.program_id(1) == 0)
    def _():
        xs_ref[0:SUBLANES, :] = jnp.zeros((SUBLANES, 4 * GW), F32)
        st_ref[...] = jnp.zeros_like(st_ref)

    pr = pr_ref[...]
    xs_ref[SUBLANES:SUBLANES + RB, :] = pr
    shifted = xs_ref[pl.ds(SUBLANES - 1, RB), :]
    xs_ref[0:SUBLANES, :] = pr[RB - SUBLANES:RB, :]
    prm = pr + mu_ref[...] * (shifted - pr)

    r = prm[:, 0:GW]
    k = prm[:, GW:2 * GW]
    v = prm[:, 2 * GW:3 * GW]
    z = prm[:, 3 * GW:3 * GW + LANES]
    lane = lax.broadcasted_iota(jnp.int32, (1, LANES), 1)
    wa = _bdot(jnp.where(lane < DH, jnp.tanh(z), z), wc_ref[...])
    w_log = -_softplus(-(w0_ref[...] + wa[:, 0:GW])) - 0.5
    logw = -jnp.exp(w_log)
    a = _sigmoid(a0_ref[...] + wa[:, GW:2 * GW])
    g = _bdot(_sigmoid(prm[:, 3 * GW + LANES:4 * GW]), gup_ref[...])

    ones_bd = _same_head_matrix(1.0)
    kk = k * kk_ref[...]
    kk = kk / jnp.maximum(jnp.sqrt(_bdot(kk * kk, ones_bd)), 1e-12)
    k2 = k * (1.0 + (a - 1.0) * ka_ref[...])
    bonus = _bdot(r * k2 * rk_ref[...], ones_bd) * v
    b = kk * a

    nc = RB // L
    rl = lax.broadcasted_iota(jnp.int32, (L, L), 0)
    cl = lax.broadcasted_iota(jnp.int32, (L, L), 1)
    tril = jnp.where(rl >= cl, 1.0, 0.0)
    cum_c = [_sel_dot(tril, logw[c * L:(c + 1) * L], 2) for c in range(nc)]
    cum = jnp.concatenate(cum_c, axis=0)
    p_tot_c = [jnp.exp(cc[L - 1:L]) for cc in cum_c]
    p_inv = jnp.exp(-cum)
    p_end = jnp.concatenate([p_inv[c * L:(c + 1) * L] * p_tot_c[c] for c in range(nc)], axis=0)
    kkd = _to_blocks(kk * jnp.exp(cum - logw), L).astype(BF16)
    rd = _to_blocks(r * jnp.exp(cum), L)
    bi = _to_blocks(b * p_inv, L).astype(BF16)
    ki = _to_blocks(k2 * p_inv, L).astype(BF16)
    be = _to_blocks(b * p_end, L).astype(BF16)
    ke = _to_blocks(k2 * p_end, L).astype(BF16)
    vh = _to_blocks(v, L).astype(BF16)
    p_tot = _to_blocks(jnp.concatenate(p_tot_c, axis=0), 1)

    strict = rl > cl
    incl = rl >= cl

    kr = jnp.concatenate([kkd, rd.astype(BF16)], axis=1)
    xb = _bmm_nt(kr, bi)
    xk = _bmm_nt(kr, ki)
    a_b = jnp.where(strict, xb[:, 0:L], 0.0)
    b_b = jnp.where(incl, xb[:, L:2 * L], 0.0)
    a_k = jnp.where(strict, xk[:, 0:L], 0.0)
    b_k = jnp.where(incl, xk[:, L:2 * L], 0.0)
    t_inv = _unit_lower_inverse(a_b, rl, cl, log2_l)
    w_q = _bmm(t_inv, kkd)
    u_0 = _bmm(t_inv, _bmm(a_k, vh))
    q_eff = rd - _bmm(b_b, w_q)
    y_0 = _bmm(b_k, vh) - _bmm(b_b, u_0)
    m_t = jnp.where(rl == cl, p_tot, 0.0) - _bmm_tn(be, w_q)
    n_t = _bmm_tn(ke, vh) - _bmm_tn(be, u_0)

    st = st_ref[...]
    ys = []
    for c in range(nc):
        blk = slice(c * NH, (c + 1) * NH)
        ys.append(_from_heads(_bmm(q_eff[blk], st) + y_0[blk]))
        st = _bmm(m_t[blk], st) + n_t[blk]
    st_ref[...] = st

    y = _head_norm(jnp.concatenate(ys, axis=0), RWKV_LN_EPS) * lg_ref[...] + lb_ref[...]
    y_ref[...] = (y + bonus) * g


def _rwkv(pr, mu, w0, a0, wc, gup, kk, ka, rk, lg, lb, bsz, seq):
    RB = R_BLOCK
    nb = seq // RB
    row = lambda b, i: (b * nb + i, 0)
    const = lambda b, i: (0, 0)
    params = (mu, w0, a0, wc, gup, kk, ka, rk, lg, lb)
    return pl.pallas_call(
        _rwkv_kernel,
        grid=(bsz, nb),
        in_specs=[pl.BlockSpec((RB, 4 * GW), row)] + [pl.BlockSpec(p.shape, const) for p in params],
        out_specs=pl.BlockSpec((RB, GW), row),
        out_shape=jax.ShapeDtypeStruct((bsz * seq, GW), F32),
        scratch_shapes=[pltpu.VMEM((SUBLANES + RB, 4 * GW), F32), pltpu.VMEM((NH, DH, DH), F32)],
        compiler_params=_cparams(("parallel", "arbitrary")),
        name="rwkv",
    )(pr, *params)


def _ret_kernel(pt_ref, cos_ref, sa_ref, sb_ref, dm_ref, qd_ref, kd_ref, cd_ref, gg_ref, gb_ref,
                y_ref, rt_ref):
    MB, L = M_BLOCK, M_CHUNK
    nc = MB // L

    @pl.when(pl.program_id(1) == 0)
    def _():
        rt_ref[...] = jnp.zeros_like(rt_ref)

    cos = cos_ref[...]
    sa = sa_ref[...]
    sb = sb_ref[...]

    def rope(t):
        return t * cos + pltpu.roll(t, GW - DH // 2, 1) * sa + pltpu.roll(t, DH // 2, 1) * sb

    q_t = rope(pt_ref[:, 0:GW]).T
    k_all = rope(pt_ref[:, GW:2 * GW]) * (DH ** -0.5)
    v_t = pt_ref[:, 2 * GW:3 * GW].T

    blocks = [(c, h) for c in range(nc) for h in range(NH)]
    q_b, o_loc, r_add = {}, {}, {}
    for b in blocks:
        c, h = b
        tsl, hsl = slice(c * L, (c + 1) * L), slice(h * DH, (h + 1) * DH)
        k_b = k_all[tsl, hsl].astype(BF16)
        q_b[b] = q_t[hsl, tsl].astype(BF16)
        v_b = v_t[hsl, tsl]
        s_t = _dot(k_b, q_b[b]) * dm_ref[h]
        o_loc[b] = _dot(v_b.astype(BF16), s_t.astype(BF16))
        r_add[b] = _dot((v_b * kd_ref[h:h + 1, :]).astype(BF16), k_b)

    rt = [rt_ref[h] for h in range(NH)]
    o_rows = []
    for c in range(nc):
        o_cols = []
        for h in range(NH):
            b = (c, h)
            o_cols.append(o_loc[b] + _dot(rt[h].astype(BF16), q_b[b]) * qd_ref[h:h + 1, :])
            rt[h] = rt[h] * cd_ref[h] + r_add[b]
        o_rows.append(jnp.concatenate(o_cols, axis=0))
    for h in range(NH):
        rt_ref[h] = rt[h]
    o = jnp.concatenate(o_rows, axis=1).T
    gate = pt_ref[:, 3 * GW:4 * GW]
    y_ref[...] = gate * _sigmoid(gate) * (_head_norm(o, HEAD_LN_EPS) * gg_ref[...] + gb_ref[...])


def _retention(pt, cos, sa, sb, dm, qd, kd, cd, gg, gb, bsz, seq):
    MB = M_BLOCK
    nb = seq // MB
    row = lambda b, i: (b * nb + i, 0)
    pos = lambda b, i: (i, 0)
    const2 = lambda b, i: (0, 0)
    const3 = lambda b, i: (0, 0, 0)
    return pl.pallas_call(
        _ret_kernel,
        grid=(bsz, nb),
        in_specs=[pl.BlockSpec((MB, 4 * GW), row),
                  pl.BlockSpec((MB, GW), pos), pl.BlockSpec((MB, GW), pos), pl.BlockSpec((MB, GW), pos),
                  pl.BlockSpec(dm.shape, const3), pl.BlockSpec(qd.shape, const2), pl.BlockSpec(kd.shape, const2),
                  pl.BlockSpec(cd.shape, const3), pl.BlockSpec(gg.shape, const2), pl.BlockSpec(gb.shape, const2)],
        out_specs=pl.BlockSpec((MB, GW), row),
        out_shape=jax.ShapeDtypeStruct((bsz * seq, GW), F32),
        scratch_shapes=[pltpu.VMEM((NH, DH, DH), F32)],
        compiler_params=_cparams(("parallel", "arbitrary")),
        name="retention",
    )(pt, cos, sa, sb, dm, qd, kd, cd, gg, gb)


def _gelu_tanh(x):
    return 0.5 * x * (1.0 + jnp.tanh(0.7978845608028654 * (x + 0.044715 * (x * x * x))))


def _s5_kernel(u_ref, ar_ref, ai_ref, bm_ref, cm_ref, d_ref, wg_ref, bg_ref, y_ref, bu_ref, x_ref):
    nb, T, _ = u_ref.shape

    @pl.when(pl.program_id(0) == 0)
    def _():
        x_ref[...] = jnp.zeros_like(x_ref)

    u = jnp.swapaxes(u_ref[...], 0, 1).reshape(T * nb, GW)
    bu_ref[...] = _dot(u.astype(BF16), bm_ref[...])
    ar = jnp.broadcast_to(ar_ref[...], (nb, S5_N))
    ai = jnp.broadcast_to(ai_ref[...], (nb, S5_N))

    def step(t, carry):
        xr, xi = carry
        r0 = pl.multiple_of(t * nb, nb)
        nr = ar * xr - ai * xi + bu_ref[pl.ds(r0, nb), 0:S5_N]
        ni = ar * xi + ai * xr + bu_ref[pl.ds(r0, nb), S5_N:2 * S5_N]
        bu_ref[pl.ds(r0, nb), 0:S5_N] = nr
        bu_ref[pl.ds(r0, nb), S5_N:2 * S5_N] = ni
        return nr, ni

    xr, xi = lax.fori_loop(0, T, step, (x_ref[:, 0:S5_N], x_ref[:, S5_N:2 * S5_N]), unroll=True)
    x_ref[:, 0:S5_N] = xr
    x_ref[:, S5_N:2 * S5_N] = xi

    y = _dot(bu_ref[...].astype(BF16), cm_ref[...])
    y = _gelu_tanh(y + d_ref[...] * u)
    z = _dot(y.astype(BF16), wg_ref[...]) + bg_ref[...]
    out = z[:, 0:GW] * _sigmoid(z[:, GW:2 * GW])
    y_ref[...] = jnp.swapaxes(out.reshape(T, nb, GW), 0, 1)


def _s5(ps, ar, ai, bm, cm, d, wg, bg, bsz, seq):
    T = S5_T
    const = lambda i: (0, 0)
    blk = pl.BlockSpec((bsz, T, GW), lambda i: (0, i, 0))
    params = (ar, ai, bm, cm, d, wg, bg)
    return pl.pallas_call(
        _s5_kernel,
        grid=(seq // T,),
        in_specs=[blk] + [pl.BlockSpec(p.shape, const) for p in params],
        out_specs=blk,
        out_shape=jax.ShapeDtypeStruct((bsz, seq, GW), F32),
        scratch_shapes=[pltpu.VMEM((T * bsz, 2 * S5_N), F32), pltpu.VMEM((bsz, 2 * S5_N), F32)],
        compiler_params=_cparams(("arbitrary",)),
        name="s5",
    )(ps.reshape(bsz, seq, GW), *params).reshape(bsz * seq, GW)


def _outffn_kernel(x_ref, ym_ref, yr_ref, yt_ref, ys_ref, wo_ref, g1_ref, g2_ref, g3_ref, w1_ref, w2_ref, o_ref):
    y = _dot(ym_ref[...].astype(BF16), wo_ref[0:GW, :])
    y = y + _dot(yr_ref[...].astype(BF16), wo_ref[GW:2 * GW, :])
    y = y + _dot(yt_ref[...].astype(BF16), wo_ref[2 * GW:3 * GW, :])
    y = y + _dot(ys_ref[...].astype(BF16), wo_ref[3 * GW:4 * GW, :])
    x1 = x_ref[...] + _rms(y, g1_ref[...])
    h = _rms(x1, g2_ref[...]).astype(BF16)
    f = jnp.zeros_like(x1)
    for j in range(D_FF // D_MODEL):
        a = jnp.maximum(_dot(h, w1_ref[:, j * D_MODEL:(j + 1) * D_MODEL]), 0.0)
        f = f + _dot((a * a).astype(BF16), w2_ref[j * D_MODEL:(j + 1) * D_MODEL, :])
    o_ref[...] = x1 + _rms(f, g3_ref[...])


def _outffn(x2, ym, yr, yt, ys, wo, g1, g2, g3, w1, w2, bsz, seq):
    tm = TM_PROJ
    nt = seq // tm
    row = lambda b, i: (b * nt + i, 0)
    const = lambda b, i: (0, 0)
    wspec = lambda w: pl.BlockSpec(w.shape, const, pipeline_mode=pl.Buffered(1))
    return pl.pallas_call(
        _outffn_kernel,
        grid=(bsz, nt),
        in_specs=[pl.BlockSpec((tm, D_MODEL), row),
                  pl.BlockSpec((tm, GW), row), pl.BlockSpec((tm, GW), row), pl.BlockSpec((tm, GW), row),
                  pl.BlockSpec((tm, GW), row),
                  wspec(wo), pl.BlockSpec((1, D_MODEL), const), pl.BlockSpec((1, D_MODEL), const),
                  pl.BlockSpec((1, D_MODEL), const), wspec(w1), wspec(w2)],
        out_specs=pl.BlockSpec((tm, D_MODEL), row),
        out_shape=jax.ShapeDtypeStruct((bsz * seq, D_MODEL), F32),
        compiler_params=_cparams(("parallel", "parallel")),
        name="outffn",
    )(x2, ym, yr, yt, ys, wo, g1, g2, g3, w1, w2)


def _rope_tables(seq):
    half = DH // 2
    inv = ROPE_BASE ** (-jnp.arange(half, dtype=F32) / half)
    ang = jnp.arange(seq, dtype=F32)[:, None] * inv[None, :]
    cos, sin = jnp.cos(ang), jnp.sin(ang)
    zero = jnp.zeros_like(sin)
    tile = lambda first, second: jnp.tile(jnp.concatenate([first, second], axis=-1), (1, NH))
    return tile(cos, cos), tile(-sin, zero), tile(zero, sin)


def _retention_tables():
    L = M_CHUNK
    log_gamma = jnp.log(1.0 - 2.0 ** (-5.0 - jnp.arange(NH, dtype=F32)))
    idx = jnp.arange(L, dtype=F32)
    diff = idx[:, None] - idx[None, :]
    causal = diff >= 0
    decay_mat = jnp.where(causal, jnp.exp(jnp.where(causal, diff, 0.0) * log_gamma[:, None, None]), 0.0)
    decay_t = jnp.swapaxes(decay_mat, 1, 2)
    q_decay = jnp.exp((idx + 1.0) * log_gamma[:, None])
    k_decay = jnp.exp((L - 1.0 - idx) * log_gamma[:, None])
    chunk_decay = jnp.broadcast_to(jnp.exp(L * log_gamma)[:, None, None], (NH, 1, DH))
    return decay_t, q_decay, k_decay, chunk_decay


def _s5_tables(lam_re, lam_im, log_dt, b_re, b_im, c_re, c_im):
    G, P, C = S5_GROUPS, S5_STATE, S5_GROUP
    dt = jnp.exp(log_dt)[:, None]
    mag = jnp.exp(lam_re * dt)
    abar_re, abar_im = mag * jnp.cos(lam_im * dt), mag * jnp.sin(lam_im * dt)
    den = lam_re * lam_re + lam_im * lam_im
    num_re, num_im = abar_re - 1.0, abar_im
    f_re = (num_re * lam_re + num_im * lam_im) / den
    f_im = (num_im * lam_re - num_re * lam_im) / den
    bbar_re = f_re[..., None] * b_re - f_im[..., None] * b_im
    bbar_im = f_re[..., None] * b_im + f_im[..., None] * b_re
    eye = jnp.eye(G, dtype=F32)
    bd_in = lambda t: jnp.einsum('gpc,gh->gchp', t, eye).reshape(G * C, G * P)
    bd_out = lambda t: jnp.einsum('gcp,gh->gphc', t, eye).reshape(G * P, G * C)
    bmat = jnp.concatenate([bd_in(bbar_re), bd_in(bbar_im)], axis=1)
    cmat = jnp.concatenate([bd_out(c_re), -bd_out(c_im)], axis=0)
    return abar_re.reshape(1, G * P), abar_im.reshape(1, G * P), bmat.astype(BF16), cmat.astype(BF16)


def _pad_lanes(t, width=LANES):
    return jnp.pad(t, ((0, 0), (0, width - t.shape[-1])))


def kernel(x, norm_mix_pre, norm_mix_post, w_in, m_conv_w, m_conv_b, m_i_bias, m_f_bias, m_norm, r_mu, r_w0, r_w_up, r_a0, r_a_up, r_g_up, r_k_k, r_k_a, r_r_k, r_ln_g, r_ln_b, t_gn_g, t_gn_b, s_lam_re, s_lam_im, s_log_dt, s_b_re, s_b_im, s_c_re, s_c_im, s_d, s_w_glu, s_b_glu, w_out, norm_ffn_pre, norm_ffn_post, w_ff1, w_ff2):
    bsz, seq, _ = x.shape
    depth = w_in.shape[0]
    assert seq % TM_PROJ == 0 and seq % M_BLOCK == 0 and seq % R_BLOCK == 0 and seq % S5_T == 0
    assert bsz % SUBLANES == 0
    row = lambda t: t.reshape(1, -1)
    m_cols = 4 * GW + 2 * NH
    o_r = m_cols
    o_t = o_r + 4 * GW
    o_s = o_t + 4 * GW

    cos, sin_a, sin_b = _rope_tables(seq)
    dm, qd, kd, cd = _retention_tables()

    x2 = x.reshape(bsz * seq, D_MODEL)
    for l in range(depth):
        wl = w_in[l]
        wm = wl[:, 0:4 * GW].astype(BF16)
        wg = _pad_lanes(wl[:, 4 * GW:m_cols]).astype(BF16)
        wr = wl[:, o_r:o_t].astype(BF16)
        wt = wl[:, o_t:o_s].astype(BF16)
        ws = wl[:, o_s:].astype(BF16)
        pm, pg, pr, pt, ps = _inproj(x2, row(norm_mix_pre[l]), wm, wg, wr, wt, ws, bsz, seq)

        gate_bias = _pad_lanes(jnp.concatenate([m_i_bias[l], m_f_bias[l]])[None, :])
        y_m = _mlstm(pm, pg, m_conv_w[l], row(m_conv_b[l]), gate_bias, row(m_norm[l]), bsz, seq)

        zeros = jnp.zeros((DH, GW), F32)
        w_comb = jnp.concatenate([jnp.concatenate([r_w_up[l], zeros], axis=1),
                                  jnp.concatenate([zeros, r_a_up[l]], axis=1)], axis=0)
        y_r = _rwkv(pr, row(r_mu[l]), row(r_w0[l]), row(r_a0[l]), w_comb, r_g_up[l], row(r_k_k[l]),
                    row(r_k_a[l]), row(r_r_k[l]), row(r_ln_g[l]), row(r_ln_b[l]), bsz, seq)

        y_t = _retention(pt, cos, sin_a, sin_b, dm, qd, kd, cd, row(t_gn_g[l]), row(t_gn_b[l]), bsz, seq)

        ar, ai, bmat, cmat = _s5_tables(s_lam_re[l], s_lam_im[l], s_log_dt[l], s_b_re[l], s_b_im[l],
                                        s_c_re[l], s_c_im[l])
        y_s = _s5(ps, ar, ai, bmat, cmat, row(s_d[l]), s_w_glu[l].astype(BF16), row(s_b_glu[l]), bsz, seq)

        x2 = _outffn(x2, y_m, y_r, y_t, y_s, w_out[l].astype(BF16),
                     row(norm_mix_post[l]), row(norm_ffn_pre[l]), row(norm_ffn_post[l]),
                     w_ff1[l].astype(BF16), w_ff2[l].astype(BF16), bsz, seq)
    return x2.reshape(bsz, seq, D_MODEL)
```

```python
import jax
import jax.numpy as jnp
from jax import lax
from jax.experimental import pallas as pl
from jax.experimental.pallas import tpu as pltpu

F32 = jnp.float32
BF16 = jnp.bfloat16

D_MODEL = 1024
GW = 256
NH = 4
DH = 64
HEAD_SHIFT = 6
D_FF = 4 * D_MODEL
S5_GROUPS = 16
S5_GROUP = 16
S5_STATE = 64
S5_N = S5_GROUPS * S5_STATE
CONV_WIDTH = 4
ROPE_BASE = 10000.0
RMS_EPS = 1e-6
HEAD_LN_EPS = 1e-5
RWKV_LN_EPS = 64e-5

SUBLANES = 8
LANES = 128
M_CHUNK = 128
M_BLOCK = 1024
R_CHUNK = 64
R_BLOCK = 1024
S5_T = 128
TM_IN = 1024
TM_PROJ = 512
VMEM_LIMIT = 56 * 1024 * 1024


def _cparams(sem):
    return pltpu.CompilerParams(dimension_semantics=sem, vmem_limit_bytes=VMEM_LIMIT)


def _dot(a, b, precision=None):
    return jnp.dot(a, b, preferred_element_type=F32, precision=precision)


def _dot_nt(a, b, precision=None):
    return lax.dot_general(a, b, (((1,), (1,)), ((), ())), preferred_element_type=F32, precision=precision)


def _dot_tn(a, b, precision=None):
    return lax.dot_general(a, b, (((0,), (0,)), ((), ())), preferred_element_type=F32, precision=precision)


_FORMS = {"nn": _dot, "nt": _dot_nt, "tn": _dot_tn}


def _bdot(a, b, form="nn"):
    return _FORMS[form](a.astype(BF16), b.astype(BF16))


def _split_bf16(a, terms):
    parts = []
    for _ in range(terms - 1):
        hi = a.astype(BF16)
        parts.append(hi)
        a = a - hi.astype(F32)
    parts.append(a.astype(BF16))
    return parts


def _sel_dot(sel, b, terms):
    sel = sel.astype(BF16)
    return sum(_dot(sel, p) for p in _split_bf16(b, terms))


def _sigmoid(x):
    return 1.0 / (1.0 + jnp.exp(-x))


def _softplus(x):
    return jnp.maximum(x, 0.0) + jnp.log(1.0 + jnp.exp(-jnp.abs(x)))


def _same_head_matrix(scale):
    r = lax.broadcasted_iota(jnp.int32, (GW, GW), 0)
    c = lax.broadcasted_iota(jnp.int32, (GW, GW), 1)
    return jnp.where((r >> HEAD_SHIFT) == (c >> HEAD_SHIFT), scale, 0.0).astype(F32)


def _head_norm(x, eps):
    avg = _same_head_matrix(1.0 / DH)
    mu = _bdot(x, avg)
    xc = x - mu
    var = _bdot(xc * xc, avg)
    return xc * lax.rsqrt(var + eps)


def _rms(x, g):
    return x * lax.rsqrt(jnp.mean(x * x, axis=-1, keepdims=True) + RMS_EPS) * g


def _inproj_kernel(x_ref, g_ref, wm_ref, wg_ref, wr_ref, wt_ref, ws_ref,
                   pm_ref, pg_ref, pr_ref, pt_ref, ps_ref):
    h = _rms(x_ref[...], g_ref[...]).astype(BF16)
    pm_ref[...] = _dot(h, wm_ref[...])
    pg_ref[...] = _dot(h, wg_ref[...])
    pr_ref[...] = _dot(h, wr_ref[...])
    pt_ref[...] = _dot(h, wt_ref[...])
    ps_ref[...] = _dot(h, ws_ref[...])


def _inproj(x2, g, wm, wg, wr, wt, ws, bsz, seq):
    n = bsz * seq
    tm = TM_IN
    nt = seq // tm
    row = lambda b, i: (b * nt + i, 0)
    const = lambda b, i: (0, 0)
    wspec = lambda w: pl.BlockSpec(w.shape, const, pipeline_mode=pl.Buffered(1))
    return pl.pallas_call(
        _inproj_kernel,
        grid=(bsz, nt),
        in_specs=[pl.BlockSpec((tm, D_MODEL), row), pl.BlockSpec((1, D_MODEL), const),
                  wspec(wm), wspec(wg), wspec(wr), wspec(wt), wspec(ws)],
        out_specs=[pl.BlockSpec((tm, 4 * GW), row), pl.BlockSpec((tm, LANES), row),
                   pl.BlockSpec((tm, 4 * GW), row), pl.BlockSpec((tm, 4 * GW), row),
                   pl.BlockSpec((tm, GW), row)],
        out_shape=[jax.ShapeDtypeStruct((n, 4 * GW), F32), jax.ShapeDtypeStruct((n, LANES), F32),
                   jax.ShapeDtypeStruct((n, 4 * GW), F32), jax.ShapeDtypeStruct((n, 4 * GW), F32),
                   jax.ShapeDtypeStruct((n, GW), F32)],
        compiler_params=_cparams(("parallel", "parallel")),
        name="inproj",
    )(x2, g, wm, wg, wr, wt, ws)


def _mlstm_kernel(pm_ref, pg_ref, cw_ref, cb_ref, gb_ref, ng_ref, y_ref, xs_ref, ct_ref, n_ref, m_ref):
    MB, L = M_BLOCK, M_CHUNK
    nc = MB // L

    @pl.when(pl.program_id(1) == 0)
    def _():
        xs_ref[0:SUBLANES, :] = jnp.zeros((SUBLANES, 2 * GW), F32)
        ct_ref[...] = jnp.zeros_like(ct_ref)
        n_ref[...] = jnp.zeros_like(n_ref)
        m_ref[...] = jnp.zeros_like(m_ref)

    xqk = pm_ref[:, 0:2 * GW]
    xs_ref[SUBLANES:SUBLANES + MB, :] = xqk
    conv = cb_ref[...]
    for j in range(CONV_WIDTH):
        conv = conv + xs_ref[pl.ds(SUBLANES - (CONV_WIDTH - 1) + j, MB), :] * cw_ref[j:j + 1, :]
    xs_ref[0:SUBLANES, :] = xqk[MB - SUBLANES:MB, :]
    qk = conv * _sigmoid(conv)
    q_t = qk[:, 0:GW].T
    k_all = qk[:, GW:2 * GW] * (DH ** -0.5)
    v_t = pm_ref[:, 2 * GW:3 * GW].T

    lane = lax.broadcasted_iota(jnp.int32, (1, LANES), 1)
    g2 = pg_ref[...] + gb_ref[...]
    lf = jnp.where((lane >= NH) & (lane < 2 * NH), -_softplus(-g2), 0.0)
    ri = lax.broadcasted_iota(jnp.int32, (L, L), 0)
    ci = lax.broadcasted_iota(jnp.int32, (L, L), 1)
    tril = jnp.where(ri >= ci, 1.0, 0.0)
    gcum = jnp.concatenate([_sel_dot(tril, lf[c * L:(c + 1) * L], 2) for c in range(nc)], axis=0)
    colmat = jnp.where(lane < NH, g2, gcum)
    dcol = colmat - pltpu.roll(colmat, LANES - NH, 1)
    rowmat = colmat.T
    g_rows = rowmat[NH:2 * NH]
    c_rows = rowmat[0:NH] - g_rows

    causal = ri <= ci
    blocks = [(c, h) for c in range(nc) for h in range(NH)]
    tsl = lambda c: slice(c * L, (c + 1) * L)
    hsl = lambda h: slice(h * DH, (h + 1) * DH)
    k_b = {b: k_all[tsl(b[0]), hsl(b[1])].astype(BF16) for b in blocks}
    q_b = {b: q_t[hsl(b[1]), tsl(b[0])].astype(BF16) for b in blocks}
    v_b = {b: v_t[hsl(b[1]), tsl(b[0])] for b in blocks}
    g_row = {b: g_rows[b[1]:b[1] + 1, tsl(b[0])] for b in blocks}

    m_loc, num_loc, den_loc, g_end, m_add, c_add, n_add = {}, {}, {}, {}, {}, {}, {}
    for b in blocks:
        c, h = b
        d_t = jnp.where(causal, g_row[b] + dcol[tsl(c), h:h + 1], -jnp.inf)
        m_loc[b] = jnp.max(d_t, axis=0, keepdims=True)
        s_t = _dot(k_b[b], q_b[b]) * jnp.exp(d_t - m_loc[b])
        num_loc[b] = _dot(v_b[b].astype(BF16), s_t.astype(BF16))
        den_loc[b] = jnp.sum(s_t, axis=0, keepdims=True)
        g_end[b] = g_row[b][:, L - 1:L]
        d_state = g_end[b] + c_rows[h:h + 1, tsl(c)]
        m_add[b] = jnp.max(d_state, axis=1, keepdims=True)
        w_row = jnp.exp(d_state - m_add[b])
        c_add[b] = _dot((v_b[b] * w_row).astype(BF16), k_b[b])
        n_add[b] = _dot(jnp.broadcast_to(w_row, (SUBLANES, L)).astype(BF16), k_b[b])

    ct = [ct_ref[h] for h in range(NH)]
    n8 = [n_ref[h] for h in range(NH)]
    m_st = [m_ref[h][:, 0:1] for h in range(NH)]
    h_rows = []
    for c in range(nc):
        h_cols = []
        for h in range(NH):
            b = (c, h)
            d_inter = g_row[b] + m_st[h]
            m_row = jnp.maximum(d_inter, m_loc[b])
            e_loc = jnp.exp(m_loc[b] - m_row)
            w_inter = jnp.exp(d_inter - m_row)
            num = e_loc * num_loc[b] + w_inter * _dot(ct[h].astype(BF16), q_b[b])
            den = e_loc * den_loc[b] + w_inter * _dot(n8[h].astype(BF16), q_b[b])[0:1]
            h_cols.append(num / jnp.maximum(jnp.abs(den), jnp.exp(-m_row)))

            m_new = jnp.maximum(g_end[b] + m_st[h], m_add[b])
            keep = jnp.exp(g_end[b] + m_st[h] - m_new)
            take = jnp.exp(m_add[b] - m_new)
            ct[h] = keep * ct[h] + take * c_add[b]
            n8[h] = keep * n8[h] + take * n_add[b]
            m_st[h] = m_new
        h_rows.append(jnp.concatenate(h_cols, axis=0))
    for h in range(NH):
        ct_ref[h] = ct[h]
        n_ref[h] = n8[h]
        m_ref[h] = jnp.broadcast_to(m_st[h], (1, LANES))
    hh = jnp.concatenate(h_rows, axis=1).T
    og = _sigmoid(pm_ref[:, 3 * GW:4 * GW])
    y_ref[...] = _head_norm(hh, HEAD_LN_EPS) * ng_ref[...] * og


def _mlstm(pm, pg, cw, cb, gb, ng, bsz, seq):
    MB = M_BLOCK
    nb = seq // MB
    row = lambda b, i: (b * nb + i, 0)
    const = lambda b, i: (0, 0)
    return pl.pallas_call(
        _mlstm_kernel,
        grid=(bsz, nb),
        in_specs=[pl.BlockSpec((MB, 4 * GW), row), pl.BlockSpec((MB, LANES), row),
                  pl.BlockSpec(cw.shape, const), pl.BlockSpec(cb.shape, const),
                  pl.BlockSpec(gb.shape, const), pl.BlockSpec(ng.shape, const)],
        out_specs=pl.BlockSpec((MB, GW), row),
        out_shape=jax.ShapeDtypeStruct((bsz * seq, GW), F32),
        scratch_shapes=[pltpu.VMEM((SUBLANES + MB, 2 * GW), F32), pltpu.VMEM((NH, DH, DH), F32),
                        pltpu.VMEM((NH, SUBLANES, DH), F32), pltpu.VMEM((NH, 1, LANES), F32)],
        compiler_params=_cparams(("parallel", "arbitrary")),
        name="mlstm",
    )(pm, pg, cw, cb, gb, ng)


def _to_blocks(x, rows):
    return jnp.stack([x[c * rows:(c + 1) * rows, h * DH:(h + 1) * DH]
                      for c in range(x.shape[0] // rows) for h in range(NH)], axis=0)


def _from_heads(x):
    return jnp.concatenate([x[h] for h in range(NH)], axis=-1)


def _bmm(a, b):
    return jnp.einsum('hij,hjk->hik', a.astype(BF16), b.astype(BF16), preferred_element_type=F32)


def _bmm_nt(a, b):
    return jnp.einsum('hik,hjk->hij', a.astype(BF16), b.astype(BF16), preferred_element_type=F32)


def _bmm_tn(a, b):
    return jnp.einsum('hki,hkj->hij', a.astype(BF16), b.astype(BF16), preferred_element_type=F32)


def _unit_lower_inverse(a, ri, ci, log2_n):
    same = lambda s: (ri >> s) == (ci >> s)
    eye = jnp.where(ri == ci, 1.0, 0.0).astype(F32)
    a8 = jnp.where(same(3), a, 0.0)
    a8_2 = _bmm(a8, a8)
    a8_4 = _bmm(a8_2, a8_2)
    inv = _bmm(_bmm(eye - a8, eye + a8_2), eye + a8_4)
    for s in range(3, log2_n):
        off = jnp.where(same(s + 1) & jnp.logical_not(same(s)), a, 0.0)
        inv = inv - _bmm(inv, _bmm(off, inv))
    return inv


def _rwkv_kernel(pr_ref, mu_ref, w0_ref, a0_ref, wc_ref, gup_ref, kk_ref, ka_ref, rk_ref, lg_ref, lb_ref,
                 y_ref, xs_ref, st_ref):
    RB, L = R_BLOCK, R_CHUNK
    log2_l = L.bit_length() - 1

    @pl.when(pl.program_id(1) == 0)
    def _():
        xs_ref[...] = jnp.zeros_like(xs_ref)
        st_ref[...] = jnp.zeros_like(st_ref)

    pr = pr_ref[...]
    rolled = pltpu.roll(pr, 1, 0)
    row0 = lax.broadcasted_iota(jnp.int32, (SUBLANES, 1), 0) == 0
    head = jnp.where(row0, xs_ref[SUBLANES - 1:SUBLANES, :], rolled[0:SUBLANES])
    shifted = jnp.concatenate([head, rolled[SUBLANES:RB]], axis=0)
    xs_ref[...] = pr[RB - SUBLANES:RB, :]
    prm = pr + mu_ref[...] * (shifted - pr)

    r = prm[:, 0:GW]
    k = prm[:, GW:2 * GW]
    v = prm[:, 2 * GW:3 * GW]
    z = prm[:, 3 * GW:3 * GW + LANES]
    lane = lax.broadcasted_iota(jnp.int32, (1, LANES), 1)
    wa = _bdot(jnp.where(lane < DH, jnp.tanh(z), z), wc_ref[...])
    w_log = -_softplus(-(w0_ref[...] + wa[:, 0:GW])) - 0.5
    logw = -jnp.exp(w_log)
    a = _sigmoid(a0_ref[...] + wa[:, GW:2 * GW])
    g = _bdot(_sigmoid(prm[:, 3 * GW + LANES:4 * GW]), gup_ref[...])

    ones_bd = _same_head_matrix(1.0)
    kk = k * kk_ref[...]
    kk = kk / jnp.maximum(jnp.sqrt(_bdot(kk * kk, ones_bd)), 1e-12)
    k2 = k * (1.0 + (a - 1.0) * ka_ref[...])
    bonus = _bdot(r * k2 * rk_ref[...], ones_bd) * v
    b = kk * a

    nc = RB // L
    rl = lax.broadcasted_iota(jnp.int32, (L, L), 0)
    cl = lax.broadcasted_iota(jnp.int32, (L, L), 1)
    tril = jnp.where(rl >= cl, 1.0, 0.0)
    cum_c = [_sel_dot(tril, logw[c * L:(c + 1) * L], 2) for c in range(nc)]
    cum = jnp.concatenate(cum_c, axis=0)
    p_tot_c = [jnp.exp(cc[L - 1:L]) for cc in cum_c]
    p_inv = jnp.exp(-cum)
    kkd = _to_blocks(kk * jnp.exp(cum - logw), L).astype(BF16)
    rd = _to_blocks(r * jnp.exp(cum), L)
    p_tot = _to_blocks(jnp.concatenate(p_tot_c, axis=0), 1)
    bi_f = _to_blocks(b * p_inv, L)
    ki_f = _to_blocks(k2 * p_inv, L)
    bi, ki = bi_f.astype(BF16), ki_f.astype(BF16)
    be = (bi_f * p_tot).astype(BF16)
    ke = (ki_f * p_tot).astype(BF16)
    vh = _to_blocks(v, L).astype(BF16)

    strict = rl > cl
    incl = rl >= cl

    kr = jnp.concatenate([kkd, rd.astype(BF16)], axis=1)
    xb = _bmm_nt(kr, bi)
    xk = _bmm_nt(kr, ki)
    a_b = jnp.where(strict, xb[:, 0:L], 0.0)
    b_b = jnp.where(incl, xb[:, L:2 * L], 0.0)
    a_k = jnp.where(strict, xk[:, 0:L], 0.0)
    b_k = jnp.where(incl, xk[:, L:2 * L], 0.0)
    t_inv = _unit_lower_inverse(a_b, rl, cl, log2_l)
    w_q = _bmm(t_inv, kkd)
    u_0 = _bmm(t_inv, _bmm(a_k, vh))
    q_eff = rd - _bmm(b_b, w_q)
    y_0 = _bmm(b_k, vh) - _bmm(b_b, u_0)
    m_t = jnp.where(rl == cl, p_tot, 0.0) - _bmm_tn(be, w_q)
    n_t = _bmm_tn(ke, vh) - _bmm_tn(be, u_0)

    pairs = lambda t: t.reshape((nc // 2, 2 * NH) + t.shape[1:])
    m_p, n_p, q_p, y_p = pairs(m_t), pairs(n_t), pairs(q_eff), pairs(y_0)
    m_a, m_b = m_p[:, 0:NH].reshape(-1, L, L), m_p[:, NH:2 * NH].reshape(-1, L, L)
    n_a, n_b = n_p[:, 0:NH].reshape(-1, L, L), n_p[:, NH:2 * NH].reshape(-1, L, L)
    m_ab = _bmm(m_b, m_a)
    n_ab = _bmm(m_b, n_a) + n_b

    st = st_ref[...]
    st_a = []
    for j in range(nc // 2):
        blk = slice(j * NH, (j + 1) * NH)
        st_a.append(st)
        st = _bmm(m_ab[blk], st) + n_ab[blk]
    st_ref[...] = st
    st_a = jnp.concatenate(st_a, axis=0)
    st_b = _bmm(m_a, st_a) + n_a
    y_a = _bmm(q_p[:, 0:NH].reshape(-1, L, L), st_a) + y_p[:, 0:NH].reshape(-1, L, L)
    y_b = _bmm(q_p[:, NH:2 * NH].reshape(-1, L, L), st_b) + y_p[:, NH:2 * NH].reshape(-1, L, L)
    ys = []
    for j in range(nc // 2):
        blk = slice(j * NH, (j + 1) * NH)
        ys.append(_from_heads(y_a[blk]))
        ys.append(_from_heads(y_b[blk]))

    y = _head_norm(jnp.concatenate(ys, axis=0), RWKV_LN_EPS) * lg_ref[...] + lb_ref[...]
    y_ref[...] = (y + bonus) * g


def _rwkv(pr, mu, w0, a0, wc, gup, kk, ka, rk, lg, lb, bsz, seq):
    RB = R_BLOCK
    nb = seq // RB
    row = lambda b, i: (b * nb + i, 0)
    const = lambda b, i: (0, 0)
    params = (mu, w0, a0, wc, gup, kk, ka, rk, lg, lb)
    return pl.pallas_call(
        _rwkv_kernel,
        grid=(bsz, nb),
        in_specs=[pl.BlockSpec((RB, 4 * GW), row)] + [pl.BlockSpec(p.shape, const) for p in params],
        out_specs=pl.BlockSpec((RB, GW), row),
        out_shape=jax.ShapeDtypeStruct((bsz * seq, GW), F32),
        scratch_shapes=[pltpu.VMEM((SUBLANES, 4 * GW), F32), pltpu.VMEM((NH, DH, DH), F32)],
        compiler_params=_cparams(("parallel", "arbitrary")),
        name="rwkv",
    )(pr, *params)


def _ret_kernel(pt_ref, cos_ref, sa_ref, sb_ref, dm_ref, qd_ref, kd_ref, cd_ref, gg_ref, gb_ref,
                y_ref, rt_ref):
    MB, L = M_BLOCK, M_CHUNK
    nc = MB // L

    @pl.when(pl.program_id(1) == 0)
    def _():
        rt_ref[...] = jnp.zeros_like(rt_ref)

    cos = cos_ref[...]
    sa = sa_ref[...]
    sb = sb_ref[...]

    def rope(t):
        return t * cos + pltpu.roll(t, GW - DH // 2, 1) * sa + pltpu.roll(t, DH // 2, 1) * sb

    q_t = rope(pt_ref[:, 0:GW]).T
    k_all = rope(pt_ref[:, GW:2 * GW]) * (DH ** -0.5)
    v_t = pt_ref[:, 2 * GW:3 * GW].T

    blocks = [(c, h) for c in range(nc) for h in range(NH)]
    q_b, o_loc, r_add = {}, {}, {}
    for b in blocks:
        c, h = b
        tsl, hsl = slice(c * L, (c + 1) * L), slice(h * DH, (h + 1) * DH)
        k_b = k_all[tsl, hsl].astype(BF16)
        q_b[b] = q_t[hsl, tsl].astype(BF16)
        v_b = v_t[hsl, tsl]
        s_t = _dot(k_b, q_b[b]) * dm_ref[h]
        o_loc[b] = _dot(v_b.astype(BF16), s_t.astype(BF16))
        r_add[b] = _dot((v_b * kd_ref[h:h + 1, :]).astype(BF16), k_b)

    rt = [rt_ref[h] for h in range(NH)]
    o_rows = []
    for c in range(nc):
        o_cols = []
        for h in range(NH):
            b = (c, h)
            o_cols.append(o_loc[b] + _dot(rt[h].astype(BF16), q_b[b]) * qd_ref[h:h + 1, :])
            rt[h] = rt[h] * cd_ref[h] + r_add[b]
        o_rows.append(jnp.concatenate(o_cols, axis=0))
    for h in range(NH):
        rt_ref[h] = rt[h]
    o = jnp.concatenate(o_rows, axis=1).T
    gate = pt_ref[:, 3 * GW:4 * GW]
    y_ref[...] = gate * _sigmoid(gate) * (_head_norm(o, HEAD_LN_EPS) * gg_ref[...] + gb_ref[...])


def _retention(pt, cos, sa, sb, dm, qd, kd, cd, gg, gb, bsz, seq):
    MB = M_BLOCK
    nb = seq // MB
    row = lambda b, i: (b * nb + i, 0)
    pos = lambda b, i: (i, 0)
    const2 = lambda b, i: (0, 0)
    const3 = lambda b, i: (0, 0, 0)
    return pl.pallas_call(
        _ret_kernel,
        grid=(bsz, nb),
        in_specs=[pl.BlockSpec((MB, 4 * GW), row),
                  pl.BlockSpec((MB, GW), pos), pl.BlockSpec((MB, GW), pos), pl.BlockSpec((MB, GW), pos),
                  pl.BlockSpec(dm.shape, const3), pl.BlockSpec(qd.shape, const2), pl.BlockSpec(kd.shape, const2),
                  pl.BlockSpec(cd.shape, const3), pl.BlockSpec(gg.shape, const2), pl.BlockSpec(gb.shape, const2)],
        out_specs=pl.BlockSpec((MB, GW), row),
        out_shape=jax.ShapeDtypeStruct((bsz * seq, GW), F32),
        scratch_shapes=[pltpu.VMEM((NH, DH, DH), F32)],
        compiler_params=_cparams(("parallel", "arbitrary")),
        name="retention",
    )(pt, cos, sa, sb, dm, qd, kd, cd, gg, gb)


def _gelu_tanh(x):
    return 0.5 * x * (1.0 + jnp.tanh(0.7978845608028654 * (x + 0.044715 * (x * x * x))))


def _s5_kernel(u_ref, ar_ref, ai_ref, bm_ref, cm_ref, d_ref, wg_ref, bg_ref, y_ref, bu_ref, x_ref):
    nb, T, _ = u_ref.shape

    @pl.when(pl.program_id(0) == 0)
    def _():
        x_ref[...] = jnp.zeros_like(x_ref)

    u = jnp.swapaxes(u_ref[...], 0, 1).reshape(T * nb, GW)
    bu_ref[...] = _dot(u.astype(BF16), bm_ref[...])
    ar = jnp.broadcast_to(ar_ref[...], (nb, S5_N))
    ai = jnp.broadcast_to(ai_ref[...], (nb, S5_N))

    def step(t, carry):
        xr, xi = carry
        r0 = pl.multiple_of(t * nb, nb)
        nr = ar * xr - ai * xi + bu_ref[pl.ds(r0, nb), 0:S5_N]
        ni = ar * xi + ai * xr + bu_ref[pl.ds(r0, nb), S5_N:2 * S5_N]
        bu_ref[pl.ds(r0, nb), 0:S5_N] = nr
        bu_ref[pl.ds(r0, nb), S5_N:2 * S5_N] = ni
        return nr, ni

    xr, xi = lax.fori_loop(0, T, step, (x_ref[:, 0:S5_N], x_ref[:, S5_N:2 * S5_N]), unroll=True)
    x_ref[:, 0:S5_N] = xr
    x_ref[:, S5_N:2 * S5_N] = xi

    y = _dot(bu_ref[...].astype(BF16), cm_ref[...])
    y = _gelu_tanh(y + d_ref[...] * u)
    z = _dot(y.astype(BF16), wg_ref[...]) + bg_ref[...]
    out = z[:, 0:GW] * _sigmoid(z[:, GW:2 * GW])
    y_ref[...] = jnp.swapaxes(out.reshape(T, nb, GW), 0, 1)


def _s5(ps, ar, ai, bm, cm, d, wg, bg, bsz, seq):
    T = S5_T
    const = lambda i: (0, 0)
    blk = pl.BlockSpec((bsz, T, GW), lambda i: (0, i, 0))
    params = (ar, ai, bm, cm, d, wg, bg)
    return pl.pallas_call(
        _s5_kernel,
        grid=(seq // T,),
        in_specs=[blk] + [pl.BlockSpec(p.shape, const) for p in params],
        out_specs=blk,
        out_shape=jax.ShapeDtypeStruct((bsz, seq, GW), F32),
        scratch_shapes=[pltpu.VMEM((T * bsz, 2 * S5_N), F32), pltpu.VMEM((bsz, 2 * S5_N), F32)],
        compiler_params=_cparams(("arbitrary",)),
        name="s5",
    )(ps.reshape(bsz, seq, GW), *params).reshape(bsz * seq, GW)


def _outffn_kernel(x_ref, ym_ref, yr_ref, yt_ref, ys_ref, wo_ref, g1_ref, g2_ref, g3_ref, w1_ref, w2_ref, o_ref):
    y = _dot(ym_ref[...].astype(BF16), wo_ref[0:GW, :])
    y = y + _dot(yr_ref[...].astype(BF16), wo_ref[GW:2 * GW, :])
    y = y + _dot(yt_ref[...].astype(BF16), wo_ref[2 * GW:3 * GW, :])
    y = y + _dot(ys_ref[...].astype(BF16), wo_ref[3 * GW:4 * GW, :])
    x1 = x_ref[...] + _rms(y, g1_ref[...])
    h = _rms(x1, g2_ref[...]).astype(BF16)
    f = jnp.zeros_like(x1)
    for j in range(D_FF // D_MODEL):
        a = jnp.maximum(_dot(h, w1_ref[:, j * D_MODEL:(j + 1) * D_MODEL]), 0.0)
        f = f + _dot((a * a).astype(BF16), w2_ref[j * D_MODEL:(j + 1) * D_MODEL, :])
    o_ref[...] = x1 + _rms(f, g3_ref[...])


def _outffn(x2, ym, yr, yt, ys, wo, g1, g2, g3, w1, w2, bsz, seq):
    tm = TM_PROJ
    nt = seq // tm
    row = lambda b, i: (b * nt + i, 0)
    const = lambda b, i: (0, 0)
    wspec = lambda w: pl.BlockSpec(w.shape, const, pipeline_mode=pl.Buffered(1))
    return pl.pallas_call(
        _outffn_kernel,
        grid=(bsz, nt),
        in_specs=[pl.BlockSpec((tm, D_MODEL), row),
                  pl.BlockSpec((tm, GW), row), pl.BlockSpec((tm, GW), row), pl.BlockSpec((tm, GW), row),
                  pl.BlockSpec((tm, GW), row),
                  wspec(wo), pl.BlockSpec((1, D_MODEL), const), pl.BlockSpec((1, D_MODEL), const),
                  pl.BlockSpec((1, D_MODEL), const), wspec(w1), wspec(w2)],
        out_specs=pl.BlockSpec((tm, D_MODEL), row),
        out_shape=jax.ShapeDtypeStruct((bsz * seq, D_MODEL), F32),
        compiler_params=_cparams(("parallel", "parallel")),
        name="outffn",
    )(x2, ym, yr, yt, ys, wo, g1, g2, g3, w1, w2)


def _rope_tables(seq):
    half = DH // 2
    inv = ROPE_BASE ** (-jnp.arange(half, dtype=F32) / half)
    ang = jnp.arange(seq, dtype=F32)[:, None] * inv[None, :]
    cos, sin = jnp.cos(ang), jnp.sin(ang)
    zero = jnp.zeros_like(sin)
    tile = lambda first, second: jnp.tile(jnp.concatenate([first, second], axis=-1), (1, NH))
    return tile(cos, cos), tile(-sin, zero), tile(zero, sin)


def _retention_tables():
    L = M_CHUNK
    log_gamma = jnp.log(1.0 - 2.0 ** (-5.0 - jnp.arange(NH, dtype=F32)))
    idx = jnp.arange(L, dtype=F32)
    diff = idx[:, None] - idx[None, :]
    causal = diff >= 0
    decay_mat = jnp.where(causal, jnp.exp(jnp.where(causal, diff, 0.0) * log_gamma[:, None, None]), 0.0)
    decay_t = jnp.swapaxes(decay_mat, 1, 2)
    q_decay = jnp.exp((idx + 1.0) * log_gamma[:, None])
    k_decay = jnp.exp((L - 1.0 - idx) * log_gamma[:, None])
    chunk_decay = jnp.broadcast_to(jnp.exp(L * log_gamma)[:, None, None], (NH, 1, DH))
    return decay_t, q_decay, k_decay, chunk_decay


def _s5_tables(lam_re, lam_im, log_dt, b_re, b_im, c_re, c_im):
    G, P, C = S5_GROUPS, S5_STATE, S5_GROUP
    dt = jnp.exp(log_dt)[:, None]
    mag = jnp.exp(lam_re * dt)
    abar_re, abar_im = mag * jnp.cos(lam_im * dt), mag * jnp.sin(lam_im * dt)
    den = lam_re * lam_re + lam_im * lam_im
    num_re, num_im = abar_re - 1.0, abar_im
    f_re = (num_re * lam_re + num_im * lam_im) / den
    f_im = (num_im * lam_re - num_re * lam_im) / den
    bbar_re = f_re[..., None] * b_re - f_im[..., None] * b_im
    bbar_im = f_re[..., None] * b_im + f_im[..., None] * b_re
    eye = jnp.eye(G, dtype=F32)
    bd_in = lambda t: jnp.einsum('gpc,gh->gchp', t, eye).reshape(G * C, G * P)
    bd_out = lambda t: jnp.einsum('gcp,gh->gphc', t, eye).reshape(G * P, G * C)
    bmat = jnp.concatenate([bd_in(bbar_re), bd_in(bbar_im)], axis=1)
    cmat = jnp.concatenate([bd_out(c_re), -bd_out(c_im)], axis=0)
    return abar_re.reshape(1, G * P), abar_im.reshape(1, G * P), bmat.astype(BF16), cmat.astype(BF16)


def _pad_lanes(t, width=LANES):
    return jnp.pad(t, ((0, 0), (0, width - t.shape[-1])))


def kernel(x, norm_mix_pre, norm_mix_post, w_in, m_conv_w, m_conv_b, m_i_bias, m_f_bias, m_norm, r_mu, r_w0, r_w_up, r_a0, r_a_up, r_g_up, r_k_k, r_k_a, r_r_k, r_ln_g, r_ln_b, t_gn_g, t_gn_b, s_lam_re, s_lam_im, s_log_dt, s_b_re, s_b_im, s_c_re, s_c_im, s_d, s_w_glu, s_b_glu, w_out, norm_ffn_pre, norm_ffn_post, w_ff1, w_ff2):
    bsz, seq, _ = x.shape
    depth = w_in.shape[0]
    assert seq % TM_IN == 0 and seq % TM_PROJ == 0 and seq % M_BLOCK == 0 and seq % R_BLOCK == 0 and seq % S5_T == 0
    assert bsz % SUBLANES == 0
    row = lambda t: t.reshape(1, -1)
    m_cols = 4 * GW + 2 * NH
    o_r = m_cols
    o_t = o_r + 4 * GW
    o_s = o_t + 4 * GW

    cos, sin_a, sin_b = _rope_tables(seq)
    dm, qd, kd, cd = _retention_tables()

    x2 = x.reshape(bsz * seq, D_MODEL)
    for l in range(depth):
        wl = w_in[l]
        wm = wl[:, 0:4 * GW].astype(BF16)
        wg = _pad_lanes(wl[:, 4 * GW:m_cols]).astype(BF16)
        wr = wl[:, o_r:o_t].astype(BF16)
        wt = wl[:, o_t:o_s].astype(BF16)
        ws = wl[:, o_s:].astype(BF16)
        pm, pg, pr, pt, ps = _inproj(x2, row(norm_mix_pre[l]), wm, wg, wr, wt, ws, bsz, seq)

        gate_bias = _pad_lanes(jnp.concatenate([m_i_bias[l], m_f_bias[l]])[None, :])
        y_m = _mlstm(pm, pg, m_conv_w[l], row(m_conv_b[l]), gate_bias, row(m_norm[l]), bsz, seq)

        zeros = jnp.zeros((DH, GW), F32)
        w_comb = jnp.concatenate([jnp.concatenate([r_w_up[l], zeros], axis=1),
                                  jnp.concatenate([zeros, r_a_up[l]], axis=1)], axis=0)
        y_r = _rwkv(pr, row(r_mu[l]), row(r_w0[l]), row(r_a0[l]), w_comb, r_g_up[l], row(r_k_k[l]),
                    row(r_k_a[l]), row(r_r_k[l]), row(r_ln_g[l]), row(r_ln_b[l]), bsz, seq)

        y_t = _retention(pt, cos, sin_a, sin_b, dm, qd, kd, cd, row(t_gn_g[l]), row(t_gn_b[l]), bsz, seq)

        ar, ai, bmat, cmat = _s5_tables(s_lam_re[l], s_lam_im[l], s_log_dt[l], s_b_re[l], s_b_im[l],
                                        s_c_re[l], s_c_im[l])
        y_s = _s5(ps, ar, ai, bmat, cmat, row(s_d[l]), s_w_glu[l].astype(BF16), row(s_b_glu[l]), bsz, seq)

        x2 = _outffn(x2, y_m, y_r, y_t, y_s, w_out[l].astype(BF16),
                     row(norm_mix_post[l]), row(norm_ffn_pre[l]), row(norm_ffn_post[l]),
                     w_ff1[l].astype(BF16), w_ff2[l].astype(BF16), bsz, seq)
    return x2.reshape(bsz, seq, D_MODEL)
```

```python
import jax
import jax.numpy as jnp
from jax import lax
from jax.experimental import pallas as pl
from jax.experimental.pallas import tpu as pltpu

F32 = jnp.float32
BF16 = jnp.bfloat16

D_MODEL = 1024
GW = 256
NH = 4
DH = 64
HEAD_SHIFT = 6
D_FF = 4 * D_MODEL
S5_GROUPS = 16
S5_GROUP = 16
S5_STATE = 64
S5_N = S5_GROUPS * S5_STATE
CONV_WIDTH = 4
ROPE_BASE = 10000.0
RMS_EPS = 1e-6
HEAD_LN_EPS = 1e-5
RWKV_LN_EPS = 64e-5

SUBLANES = 8
LANES = 128
M_CHUNK = 128
M_BLOCK = 1024
R_CHUNK = 64
R_BLOCK = 1024
S5_T = 128
TM_IN = 1024
TM_PROJ = 512
VMEM_LIMIT = 56 * 1024 * 1024


def _cparams(sem):
    return pltpu.CompilerParams(dimension_semantics=sem, vmem_limit_bytes=VMEM_LIMIT)


def _dot(a, b, precision=None):
    return jnp.dot(a, b, preferred_element_type=F32, precision=precision)


def _dot_nt(a, b, precision=None):
    return lax.dot_general(a, b, (((1,), (1,)), ((), ())), preferred_element_type=F32, precision=precision)


def _dot_tn(a, b, precision=None):
    return lax.dot_general(a, b, (((0,), (0,)), ((), ())), preferred_element_type=F32, precision=precision)


_FORMS = {"nn": _dot, "nt": _dot_nt, "tn": _dot_tn}


def _bdot(a, b, form="nn"):
    return _FORMS[form](a.astype(BF16), b.astype(BF16))


def _split_bf16(a, terms):
    parts = []
    for _ in range(terms - 1):
        hi = a.astype(BF16)
        parts.append(hi)
        a = a - hi.astype(F32)
    parts.append(a.astype(BF16))
    return parts


def _sel_dot(sel, b, terms):
    sel = sel.astype(BF16)
    return sum(_dot(sel, p) for p in _split_bf16(b, terms))


def _sigmoid(x):
    return 1.0 / (1.0 + jnp.exp(-x))


def _softplus(x):
    return jnp.maximum(x, 0.0) + jnp.log(1.0 + jnp.exp(-jnp.abs(x)))


def _same_head_matrix(scale):
    r = lax.broadcasted_iota(jnp.int32, (GW, GW), 0)
    c = lax.broadcasted_iota(jnp.int32, (GW, GW), 1)
    return jnp.where((r >> HEAD_SHIFT) == (c >> HEAD_SHIFT), scale, 0.0).astype(F32)


def _head_norm(x, eps):
    avg = _same_head_matrix(1.0 / DH)
    mu = _bdot(x, avg)
    xc = x - mu
    var = _bdot(xc * xc, avg)
    return xc * lax.rsqrt(var + eps)


def _rms(x, g):
    return x * lax.rsqrt(jnp.mean(x * x, axis=-1, keepdims=True) + RMS_EPS) * g


def _inproj_kernel(x_ref, g_ref, wm_ref, wg_ref, wr_ref, wt_ref, ws_ref,
                   pm_ref, pg_ref, pr_ref, pt_ref, ps_ref):
    h = _rms(x_ref[...], g_ref[...]).astype(BF16)
    pm_ref[...] = _dot(h, wm_ref[...])
    pg_ref[...] = _dot(h, wg_ref[...])
    pr_ref[...] = _dot(h, wr_ref[...])
    pt_ref[...] = _dot(h, wt_ref[...])
    ps_ref[...] = _dot(h, ws_ref[...])


def _inproj(x2, g, wm, wg, wr, wt, ws, bsz, seq):
    n = bsz * seq
    tm = TM_IN
    nt = seq // tm
    row = lambda b, i: (b * nt + i, 0)
    const = lambda b, i: (0, 0)
    wspec = lambda w: pl.BlockSpec(w.shape, const, pipeline_mode=pl.Buffered(1))
    return pl.pallas_call(
        _inproj_kernel,
        grid=(bsz, nt),
        in_specs=[pl.BlockSpec((tm, D_MODEL), row), pl.BlockSpec((1, D_MODEL), const),
                  wspec(wm), wspec(wg), wspec(wr), wspec(wt), wspec(ws)],
        out_specs=[pl.BlockSpec((tm, 4 * GW), row), pl.BlockSpec((tm, LANES), row),
                   pl.BlockSpec((tm, 4 * GW), row), pl.BlockSpec((tm, 4 * GW), row),
                   pl.BlockSpec((tm, GW), row)],
        out_shape=[jax.ShapeDtypeStruct((n, 4 * GW), F32), jax.ShapeDtypeStruct((n, LANES), F32),
                   jax.ShapeDtypeStruct((n, 4 * GW), F32), jax.ShapeDtypeStruct((n, 4 * GW), F32),
                   jax.ShapeDtypeStruct((n, GW), F32)],
        compiler_params=_cparams(("parallel", "parallel")),
        name="inproj",
    )(x2, g, wm, wg, wr, wt, ws)


def _mlstm_kernel(pm_ref, pg_ref, cw_ref, cb_ref, gb_ref, ng_ref, y_ref, xs_ref, ct_ref, n_ref, m_ref):
    MB, L = M_BLOCK, M_CHUNK
    nc = MB // L

    @pl.when(pl.program_id(1) == 0)
    def _():
        xs_ref[0:SUBLANES, :] = jnp.zeros((SUBLANES, 2 * GW), F32)
        ct_ref[...] = jnp.zeros_like(ct_ref)
        n_ref[...] = jnp.zeros_like(n_ref)
        m_ref[...] = jnp.zeros_like(m_ref)

    xqk = pm_ref[:, 0:2 * GW]
    xs_ref[SUBLANES:SUBLANES + MB, :] = xqk
    conv = cb_ref[...]
    for j in range(CONV_WIDTH):
        conv = conv + xs_ref[pl.ds(SUBLANES - (CONV_WIDTH - 1) + j, MB), :] * cw_ref[j:j + 1, :]
    xs_ref[0:SUBLANES, :] = xqk[MB - SUBLANES:MB, :]
    qk = conv * _sigmoid(conv)
    q_t = qk[:, 0:GW].T
    k_all = qk[:, GW:2 * GW] * (DH ** -0.5)
    v_t = pm_ref[:, 2 * GW:3 * GW].T

    lane = lax.broadcasted_iota(jnp.int32, (1, LANES), 1)
    g2 = pg_ref[...] + gb_ref[...]
    lf = jnp.where((lane >= NH) & (lane < 2 * NH), -_softplus(-g2), 0.0)
    ri = lax.broadcasted_iota(jnp.int32, (L, L), 0)
    ci = lax.broadcasted_iota(jnp.int32, (L, L), 1)
    tril = jnp.where(ri >= ci, 1.0, 0.0)
    gcum = jnp.concatenate([_sel_dot(tril, lf[c * L:(c + 1) * L], 2) for c in range(nc)], axis=0)
    colmat = jnp.where(lane < NH, g2, gcum)
    dcol = colmat - pltpu.roll(colmat, LANES - NH, 1)
    rowmat = colmat.T
    g_rows = rowmat[NH:2 * NH]
    c_rows = rowmat[0:NH] - g_rows

    causal = ri <= ci
    blocks = [(c, h) for c in range(nc) for h in range(NH)]
    tsl = lambda c: slice(c * L, (c + 1) * L)
    hsl = lambda h: slice(h * DH, (h + 1) * DH)
    k_b = {b: k_all[tsl(b[0]), hsl(b[1])].astype(BF16) for b in blocks}
    q_b = {b: q_t[hsl(b[1]), tsl(b[0])].astype(BF16) for b in blocks}
    v_b = {b: v_t[hsl(b[1]), tsl(b[0])] for b in blocks}
    g_row = {b: g_rows[b[1]:b[1] + 1, tsl(b[0])] for b in blocks}

    m_loc, num_loc, den_loc, g_end, m_add, c_add, n_add = {}, {}, {}, {}, {}, {}, {}
    for b in blocks:
        c, h = b
        d_t = jnp.where(causal, g_row[b] + dcol[tsl(c), h:h + 1], -jnp.inf)
        m_loc[b] = jnp.max(d_t, axis=0, keepdims=True)
        s_t = _dot(k_b[b], q_b[b]) * jnp.exp(d_t - m_loc[b])
        num_loc[b] = _dot(v_b[b].astype(BF16), s_t.astype(BF16))
        den_loc[b] = jnp.sum(s_t, axis=0, keepdims=True)
        g_end[b] = g_row[b][:, L - 1:L]
        d_state = g_end[b] + c_rows[h:h + 1, tsl(c)]
        m_add[b] = jnp.max(d_state, axis=1, keepdims=True)
        w_row = jnp.exp(d_state - m_add[b])
        c_add[b] = _dot((v_b[b] * w_row).astype(BF16), k_b[b])
        n_add[b] = _dot(jnp.broadcast_to(w_row, (SUBLANES, L)).astype(BF16), k_b[b])

    ct = [ct_ref[h] for h in range(NH)]
    n8 = [n_ref[h] for h in range(NH)]
    m_st = [m_ref[h][:, 0:1] for h in range(NH)]
    h_rows = []
    for c in range(nc):
        h_cols = []
        for h in range(NH):
            b = (c, h)
            d_inter = g_row[b] + m_st[h]
            m_row = jnp.maximum(d_inter, m_loc[b])
            e_loc = jnp.exp(m_loc[b] - m_row)
            w_inter = jnp.exp(d_inter - m_row)
            num = e_loc * num_loc[b] + w_inter * _dot(ct[h].astype(BF16), q_b[b])
            den = e_loc * den_loc[b] + w_inter * _dot(n8[h].astype(BF16), q_b[b])[0:1]
            h_cols.append(num / jnp.maximum(jnp.abs(den), jnp.exp(-m_row)))

            m_new = jnp.maximum(g_end[b] + m_st[h], m_add[b])
            keep = jnp.exp(g_end[b] + m_st[h] - m_new)
            take = jnp.exp(m_add[b] - m_new)
            ct[h] = keep * ct[h] + take * c_add[b]
            n8[h] = keep * n8[h] + take * n_add[b]
            m_st[h] = m_new
        h_rows.append(jnp.concatenate(h_cols, axis=0))
    for h in range(NH):
        ct_ref[h] = ct[h]
        n_ref[h] = n8[h]
        m_ref[h] = jnp.broadcast_to(m_st[h], (1, LANES))
    hh = jnp.concatenate(h_rows, axis=1).T
    og = _sigmoid(pm_ref[:, 3 * GW:4 * GW])
    y_ref[...] = _head_norm(hh, HEAD_LN_EPS) * ng_ref[...] * og


def _mlstm(pm, pg, cw, cb, gb, ng, bsz, seq):
    MB = M_BLOCK
    nb = seq // MB
    row = lambda b, i: (b * nb + i, 0)
    const = lambda b, i: (0, 0)
    return pl.pallas_call(
        _mlstm_kernel,
        grid=(bsz, nb),
        in_specs=[pl.BlockSpec((MB, 4 * GW), row), pl.BlockSpec((MB, LANES), row),
                  pl.BlockSpec(cw.shape, const), pl.BlockSpec(cb.shape, const),
                  pl.BlockSpec(gb.shape, const), pl.BlockSpec(ng.shape, const)],
        out_specs=pl.BlockSpec((MB, GW), row),
        out_shape=jax.ShapeDtypeStruct((bsz * seq, GW), F32),
        scratch_shapes=[pltpu.VMEM((SUBLANES + MB, 2 * GW), F32), pltpu.VMEM((NH, DH, DH), F32),
                        pltpu.VMEM((NH, SUBLANES, DH), F32), pltpu.VMEM((NH, 1, LANES), F32)],
        compiler_params=_cparams(("parallel", "arbitrary")),
        name="mlstm",
    )(pm, pg, cw, cb, gb, ng)


def _to_blocks(x, rows):
    return jnp.stack([x[c * rows:(c + 1) * rows, h * DH:(h + 1) * DH]
                      for c in range(x.shape[0] // rows) for h in range(NH)], axis=0)


def _from_heads(x):
    return jnp.concatenate([x[h] for h in range(NH)], axis=-1)


def _bmm(a, b):
    return jnp.einsum('hij,hjk->hik', a.astype(BF16), b.astype(BF16), preferred_element_type=F32)


def _bmm_nt(a, b):
    return jnp.einsum('hik,hjk->hij', a.astype(BF16), b.astype(BF16), preferred_element_type=F32)


def _bmm_tn(a, b):
    return jnp.einsum('hki,hkj->hij', a.astype(BF16), b.astype(BF16), preferred_element_type=F32)


def _unit_lower_inverse(a, ri, ci, log2_n):
    same = lambda s: (ri >> s) == (ci >> s)
    eye = jnp.where(ri == ci, 1.0, 0.0).astype(F32)
    a8 = jnp.where(same(3), a, 0.0)
    a8_2 = _bmm(a8, a8)
    a8_4 = _bmm(a8_2, a8_2)
    inv = _bmm(_bmm(eye - a8, eye + a8_2), eye + a8_4)
    for s in range(3, log2_n):
        off = jnp.where(same(s + 1) & jnp.logical_not(same(s)), a, 0.0)
        inv = inv - _bmm(inv, _bmm(off, inv))
    return inv


def _rwkv_kernel(pr_ref, mu_ref, w0_ref, a0_ref, wc_ref, gup_ref, kk_ref, ka_ref, rk_ref, lg_ref, lb_ref,
                 y_ref, xs_ref, st_ref):
    RB, L = R_BLOCK, R_CHUNK
    log2_l = L.bit_length() - 1

    @pl.when(pl.program_id(1) == 0)
    def _():
        xs_ref[...] = jnp.zeros_like(xs_ref)
        st_ref[...] = jnp.zeros_like(st_ref)

    pr = pr_ref[...]
    rolled = pltpu.roll(pr, 1, 0)
    row0 = lax.broadcasted_iota(jnp.int32, (SUBLANES, 1), 0) == 0
    head = jnp.where(row0, xs_ref[SUBLANES - 1:SUBLANES, :], rolled[0:SUBLANES])
    shifted = jnp.concatenate([head, rolled[SUBLANES:RB]], axis=0)
    xs_ref[...] = pr[RB - SUBLANES:RB, :]
    prm = pr + mu_ref[...] * (shifted - pr)

    r = prm[:, 0:GW]
    k = prm[:, GW:2 * GW]
    v = prm[:, 2 * GW:3 * GW]
    z = prm[:, 3 * GW:3 * GW + LANES]
    lane = lax.broadcasted_iota(jnp.int32, (1, LANES), 1)
    wa = _bdot(jnp.where(lane < DH, jnp.tanh(z), z), wc_ref[...])
    w_log = -_softplus(-(w0_ref[...] + wa[:, 0:GW])) - 0.5
    logw = -jnp.exp(w_log)
    a = _sigmoid(a0_ref[...] + wa[:, GW:2 * GW])
    g = _bdot(_sigmoid(prm[:, 3 * GW + LANES:4 * GW]), gup_ref[...])

    ones_bd = _same_head_matrix(1.0)
    kk = k * kk_ref[...]
    kk = kk / jnp.maximum(jnp.sqrt(_bdot(kk * kk, ones_bd)), 1e-12)
    k2 = k * (1.0 + (a - 1.0) * ka_ref[...])
    bonus = _bdot(r * k2 * rk_ref[...], ones_bd) * v
    b = kk * a

    nc = RB // L
    rl = lax.broadcasted_iota(jnp.int32, (L, L), 0)
    cl = lax.broadcasted_iota(jnp.int32, (L, L), 1)
    tril = jnp.where(rl >= cl, 1.0, 0.0)
    cum_c = [_sel_dot(tril, logw[c * L:(c + 1) * L], 2) for c in range(nc)]
    cum = jnp.concatenate(cum_c, axis=0)
    p_tot_c = [jnp.exp(cc[L - 1:L]) for cc in cum_c]
    p_inv = jnp.exp(-cum)
    kkd = _to_blocks(kk * jnp.exp(cum - logw), L).astype(BF16)
    rd = _to_blocks(r * jnp.exp(cum), L)
    p_tot = _to_blocks(jnp.concatenate(p_tot_c, axis=0), 1)
    bi_f = _to_blocks(b * p_inv, L)
    ki_f = _to_blocks(k2 * p_inv, L)
    bi, ki = bi_f.astype(BF16), ki_f.astype(BF16)
    be = (bi_f * p_tot).astype(BF16)
    ke = (ki_f * p_tot).astype(BF16)
    vh = _to_blocks(v, L).astype(BF16)

    strict = rl > cl
    incl = rl >= cl

    kr = jnp.concatenate([kkd, rd.astype(BF16)], axis=1)
    xb = _bmm_nt(kr, bi)
    xk = _bmm_nt(kr, ki)
    a_b = jnp.where(strict, xb[:, 0:L], 0.0)
    b_b = jnp.where(incl, xb[:, L:2 * L], 0.0)
    a_k = jnp.where(strict, xk[:, 0:L], 0.0)
    b_k = jnp.where(incl, xk[:, L:2 * L], 0.0)
    t_inv = _unit_lower_inverse(a_b, rl, cl, log2_l)
    w_q = _bmm(t_inv, kkd)
    u_0 = _bmm(t_inv, _bmm(a_k, vh))
    q_eff = rd - _bmm(b_b, w_q)
    y_0 = _bmm(b_k, vh) - _bmm(b_b, u_0)
    m_t = jnp.where(rl == cl, p_tot, 0.0) - _bmm_tn(be, w_q)
    n_t = _bmm_tn(ke, vh) - _bmm_tn(be, u_0)

    pairs = lambda t: t.reshape((nc // 2, 2 * NH) + t.shape[1:])
    m_p, n_p, q_p, y_p = pairs(m_t), pairs(n_t), pairs(q_eff), pairs(y_0)
    m_a, m_b = m_p[:, 0:NH].reshape(-1, L, L), m_p[:, NH:2 * NH].reshape(-1, L, L)
    n_a, n_b = n_p[:, 0:NH].reshape(-1, L, L), n_p[:, NH:2 * NH].reshape(-1, L, L)
    m_ab = _bmm(m_b, m_a)
    n_ab = _bmm(m_b, n_a) + n_b

    st = st_ref[...]
    st_a = []
    for j in range(nc // 2):
        blk = slice(j * NH, (j + 1) * NH)
        st_a.append(st)
        st = _bmm(m_ab[blk], st) + n_ab[blk]
    st_ref[...] = st
    st_a = jnp.concatenate(st_a, axis=0)
    st_b = _bmm(m_a, st_a) + n_a
    y_a = _bmm(q_p[:, 0:NH].reshape(-1, L, L), st_a) + y_p[:, 0:NH].reshape(-1, L, L)
    y_b = _bmm(q_p[:, NH:2 * NH].reshape(-1, L, L), st_b) + y_p[:, NH:2 * NH].reshape(-1, L, L)
    ys = []
    for j in range(nc // 2):
        blk = slice(j * NH, (j + 1) * NH)
        ys.append(_from_heads(y_a[blk]))
        ys.append(_from_heads(y_b[blk]))

    y = _head_norm(jnp.concatenate(ys, axis=0), RWKV_LN_EPS) * lg_ref[...] + lb_ref[...]
    y_ref[...] = (y + bonus) * g


def _rwkv(pr, mu, w0, a0, wc, gup, kk, ka, rk, lg, lb, bsz, seq):
    RB = R_BLOCK
    nb = seq // RB
    row = lambda b, i: (b * nb + i, 0)
    const = lambda b, i: (0, 0)
    params = (mu, w0, a0, wc, gup, kk, ka, rk, lg, lb)
    return pl.pallas_call(
        _rwkv_kernel,
        grid=(bsz, nb),
        in_specs=[pl.BlockSpec((RB, 4 * GW), row)] + [pl.BlockSpec(p.shape, const) for p in params],
        out_specs=pl.BlockSpec((RB, GW), row),
        out_shape=jax.ShapeDtypeStruct((bsz * seq, GW), F32),
        scratch_shapes=[pltpu.VMEM((SUBLANES, 4 * GW), F32), pltpu.VMEM((NH, DH, DH), F32)],
        compiler_params=_cparams(("parallel", "arbitrary")),
        name="rwkv",
    )(pr, *params)


def _ret_kernel(pt_ref, cos_ref, sin_ref, rot_ref, dm_ref, qd_ref, kd_ref, cd_ref, gg_ref, gb_ref,
                y_ref, rt_ref):
    MB, L = M_BLOCK, M_CHUNK
    nc = MB // L

    @pl.when(pl.program_id(1) == 0)
    def _():
        rt_ref[...] = jnp.zeros_like(rt_ref)

    cos = cos_ref[...]
    sin = sin_ref[...]
    rot = rot_ref[...]

    def rope(t):
        return t * cos + sum(_dot(p, rot) for p in _split_bf16(t, 2)) * sin

    q_t = rope(pt_ref[:, 0:GW]).T
    k_all = rope(pt_ref[:, GW:2 * GW]) * (DH ** -0.5)
    v_t = pt_ref[:, 2 * GW:3 * GW].T

    blocks = [(c, h) for c in range(nc) for h in range(NH)]
    q_b, o_loc, r_add = {}, {}, {}
    for b in blocks:
        c, h = b
        tsl, hsl = slice(c * L, (c + 1) * L), slice(h * DH, (h + 1) * DH)
        k_b = k_all[tsl, hsl].astype(BF16)
        q_b[b] = q_t[hsl, tsl].astype(BF16)
        v_b = v_t[hsl, tsl]
        s_t = _dot(k_b, q_b[b]) * dm_ref[h]
        o_loc[b] = _dot(v_b.astype(BF16), s_t.astype(BF16))
        r_add[b] = _dot((v_b * kd_ref[h:h + 1, :]).astype(BF16), k_b)

    rt = [rt_ref[h] for h in range(NH)]
    o_rows = []
    for c in range(nc):
        o_cols = []
        for h in range(NH):
            b = (c, h)
            o_cols.append(o_loc[b] + _dot(rt[h].astype(BF16), q_b[b]) * qd_ref[h:h + 1, :])
            rt[h] = rt[h] * cd_ref[h] + r_add[b]
        o_rows.append(jnp.concatenate(o_cols, axis=0))
    for h in range(NH):
        rt_ref[h] = rt[h]
    o = jnp.concatenate(o_rows, axis=1).T
    gate = pt_ref[:, 3 * GW:4 * GW]
    y_ref[...] = gate * _sigmoid(gate) * (_head_norm(o, HEAD_LN_EPS) * gg_ref[...] + gb_ref[...])


def _retention(pt, cos, sin, rot, dm, qd, kd, cd, gg, gb, bsz, seq):
    MB = M_BLOCK
    nb = seq // MB
    row = lambda b, i: (b * nb + i, 0)
    pos = lambda b, i: (i, 0)
    const2 = lambda b, i: (0, 0)
    const3 = lambda b, i: (0, 0, 0)
    return pl.pallas_call(
        _ret_kernel,
        grid=(bsz, nb),
        in_specs=[pl.BlockSpec((MB, 4 * GW), row),
                  pl.BlockSpec((MB, GW), pos), pl.BlockSpec((MB, GW), pos), pl.BlockSpec(rot.shape, const2),
                  pl.BlockSpec(dm.shape, const3), pl.BlockSpec(qd.shape, const2), pl.BlockSpec(kd.shape, const2),
                  pl.BlockSpec(cd.shape, const3), pl.BlockSpec(gg.shape, const2), pl.BlockSpec(gb.shape, const2)],
        out_specs=pl.BlockSpec((MB, GW), row),
        out_shape=jax.ShapeDtypeStruct((bsz * seq, GW), F32),
        scratch_shapes=[pltpu.VMEM((NH, DH, DH), F32)],
        compiler_params=_cparams(("parallel", "arbitrary")),
        name="retention",
    )(pt, cos, sin, rot, dm, qd, kd, cd, gg, gb)


def _gelu_tanh(x):
    return 0.5 * x * (1.0 + jnp.tanh(0.7978845608028654 * (x + 0.044715 * (x * x * x))))


def _s5_kernel(u_ref, ar_ref, ai_ref, bm_ref, cm_ref, d_ref, wg_ref, bg_ref, y_ref, bu_ref, x_ref):
    nb, T, _ = u_ref.shape

    @pl.when(pl.program_id(0) == 0)
    def _():
        x_ref[...] = jnp.zeros_like(x_ref)

    u = jnp.swapaxes(u_ref[...], 0, 1).reshape(T * nb, GW)
    bu_ref[...] = _dot(u.astype(BF16), bm_ref[...])
    ar = jnp.broadcast_to(ar_ref[...], (nb, S5_N))
    ai = jnp.broadcast_to(ai_ref[...], (nb, S5_N))

    def step(t, carry):
        xr, xi = carry
        r0 = pl.multiple_of(t * nb, nb)
        nr = ar * xr - ai * xi + bu_ref[pl.ds(r0, nb), 0:S5_N]
        ni = ar * xi + ai * xr + bu_ref[pl.ds(r0, nb), S5_N:2 * S5_N]
        bu_ref[pl.ds(r0, nb), 0:S5_N] = nr
        bu_ref[pl.ds(r0, nb), S5_N:2 * S5_N] = ni
        return nr, ni

    xr, xi = lax.fori_loop(0, T, step, (x_ref[:, 0:S5_N], x_ref[:, S5_N:2 * S5_N]), unroll=True)
    x_ref[:, 0:S5_N] = xr
    x_ref[:, S5_N:2 * S5_N] = xi

    y = _dot(bu_ref[...].astype(BF16), cm_ref[...])
    y = _gelu_tanh(y + d_ref[...] * u)
    z = _dot(y.astype(BF16), wg_ref[...]) + bg_ref[...]
    out = z[:, 0:GW] * _sigmoid(z[:, GW:2 * GW])
    y_ref[...] = jnp.swapaxes(out.reshape(T, nb, GW), 0, 1)


def _s5(ps, ar, ai, bm, cm, d, wg, bg, bsz, seq):
    T = S5_T
    const = lambda i: (0, 0)
    blk = pl.BlockSpec((bsz, T, GW), lambda i: (0, i, 0))
    params = (ar, ai, bm, cm, d, wg, bg)
    return pl.pallas_call(
        _s5_kernel,
        grid=(seq // T,),
        in_specs=[blk] + [pl.BlockSpec(p.shape, const) for p in params],
        out_specs=blk,
        out_shape=jax.ShapeDtypeStruct((bsz, seq, GW), F32),
        scratch_shapes=[pltpu.VMEM((T * bsz, 2 * S5_N), F32), pltpu.VMEM((bsz, 2 * S5_N), F32)],
        compiler_params=_cparams(("arbitrary",)),
        name="s5",
    )(ps.reshape(bsz, seq, GW), *params).reshape(bsz * seq, GW)


def _outffn_kernel(x_ref, ym_ref, yr_ref, yt_ref, ys_ref, wo_ref, g1_ref, g2_ref, g3_ref, w1_ref, w2_ref, o_ref):
    y = _dot(ym_ref[...].astype(BF16), wo_ref[0:GW, :])
    y = y + _dot(yr_ref[...].astype(BF16), wo_ref[GW:2 * GW, :])
    y = y + _dot(yt_ref[...].astype(BF16), wo_ref[2 * GW:3 * GW, :])
    y = y + _dot(ys_ref[...].astype(BF16), wo_ref[3 * GW:4 * GW, :])
    x1 = x_ref[...] + _rms(y, g1_ref[...])
    h = _rms(x1, g2_ref[...]).astype(BF16)
    f = jnp.zeros_like(x1)
    for j in range(D_FF // D_MODEL):
        a = jnp.maximum(_dot(h, w1_ref[:, j * D_MODEL:(j + 1) * D_MODEL]), 0.0)
        f = f + _dot((a * a).astype(BF16), w2_ref[j * D_MODEL:(j + 1) * D_MODEL, :])
    o_ref[...] = x1 + _rms(f, g3_ref[...])


def _outffn(x2, ym, yr, yt, ys, wo, g1, g2, g3, w1, w2, bsz, seq):
    tm = TM_PROJ
    nt = seq // tm
    row = lambda b, i: (b * nt + i, 0)
    const = lambda b, i: (0, 0)
    wspec = lambda w: pl.BlockSpec(w.shape, const, pipeline_mode=pl.Buffered(1))
    return pl.pallas_call(
        _outffn_kernel,
        grid=(bsz, nt),
        in_specs=[pl.BlockSpec((tm, D_MODEL), row),
                  pl.BlockSpec((tm, GW), row), pl.BlockSpec((tm, GW), row), pl.BlockSpec((tm, GW), row),
                  pl.BlockSpec((tm, GW), row),
                  wspec(wo), pl.BlockSpec((1, D_MODEL), const), pl.BlockSpec((1, D_MODEL), const),
                  pl.BlockSpec((1, D_MODEL), const), wspec(w1), wspec(w2)],
        out_specs=pl.BlockSpec((tm, D_MODEL), row),
        out_shape=jax.ShapeDtypeStruct((bsz * seq, D_MODEL), F32),
        compiler_params=_cparams(("parallel", "parallel")),
        name="outffn",
    )(x2, ym, yr, yt, ys, wo, g1, g2, g3, w1, w2)


def _rope_tables(seq):
    half = DH // 2
    inv = ROPE_BASE ** (-jnp.arange(half, dtype=F32) / half)
    ang = jnp.arange(seq, dtype=F32)[:, None] * inv[None, :]
    tile = lambda t: jnp.tile(jnp.concatenate([t, t], axis=-1), (1, NH))
    src = lax.broadcasted_iota(jnp.int32, (GW, GW), 0)
    dst = lax.broadcasted_iota(jnp.int32, (GW, GW), 1)
    first = (dst & (DH - 1)) < half
    rot = jnp.where(first & (src == dst + half), -1.0, jnp.where(jnp.logical_not(first) & (src == dst - half), 1.0, 0.0))
    return tile(jnp.cos(ang)), tile(jnp.sin(ang)), rot.astype(BF16)


def _retention_tables():
    L = M_CHUNK
    log_gamma = jnp.log(1.0 - 2.0 ** (-5.0 - jnp.arange(NH, dtype=F32)))
    idx = jnp.arange(L, dtype=F32)
    diff = idx[:, None] - idx[None, :]
    causal = diff >= 0
    decay_mat = jnp.where(causal, jnp.exp(jnp.where(causal, diff, 0.0) * log_gamma[:, None, None]), 0.0)
    decay_t = jnp.swapaxes(decay_mat, 1, 2)
    q_decay = jnp.exp((idx + 1.0) * log_gamma[:, None])
    k_decay = jnp.exp((L - 1.0 - idx) * log_gamma[:, None])
    chunk_decay = jnp.broadcast_to(jnp.exp(L * log_gamma)[:, None, None], (NH, 1, DH))
    return decay_t, q_decay, k_decay, chunk_decay


def _s5_tables(lam_re, lam_im, log_dt, b_re, b_im, c_re, c_im):
    G, P, C = S5_GROUPS, S5_STATE, S5_GROUP
    dt = jnp.exp(log_dt)[:, None]
    mag = jnp.exp(lam_re * dt)
    abar_re, abar_im = mag * jnp.cos(lam_im * dt), mag * jnp.sin(lam_im * dt)
    den = lam_re * lam_re + lam_im * lam_im
    num_re, num_im = abar_re - 1.0, abar_im
    f_re = (num_re * lam_re + num_im * lam_im) / den
    f_im = (num_im * lam_re - num_re * lam_im) / den
    bbar_re = f_re[..., None] * b_re - f_im[..., None] * b_im
    bbar_im = f_re[..., None] * b_im + f_im[..., None] * b_re
    eye = jnp.eye(G, dtype=F32)
    bd_in = lambda t: jnp.einsum('gpc,gh->gchp', t, eye).reshape(G * C, G * P)
    bd_out = lambda t: jnp.einsum('gcp,gh->gphc', t, eye).reshape(G * P, G * C)
    bmat = jnp.concatenate([bd_in(bbar_re), bd_in(bbar_im)], axis=1)
    cmat = jnp.concatenate([bd_out(c_re), -bd_out(c_im)], axis=0)
    return abar_re.reshape(1, G * P), abar_im.reshape(1, G * P), bmat.astype(BF16), cmat.astype(BF16)


def _pad_lanes(t, width=LANES):
    return jnp.pad(t, ((0, 0), (0, width - t.shape[-1])))


def kernel(x, norm_mix_pre, norm_mix_post, w_in, m_conv_w, m_conv_b, m_i_bias, m_f_bias, m_norm, r_mu, r_w0, r_w_up, r_a0, r_a_up, r_g_up, r_k_k, r_k_a, r_r_k, r_ln_g, r_ln_b, t_gn_g, t_gn_b, s_lam_re, s_lam_im, s_log_dt, s_b_re, s_b_im, s_c_re, s_c_im, s_d, s_w_glu, s_b_glu, w_out, norm_ffn_pre, norm_ffn_post, w_ff1, w_ff2):
    bsz, seq, _ = x.shape
    depth = w_in.shape[0]
    assert seq % TM_IN == 0 and seq % TM_PROJ == 0 and seq % M_BLOCK == 0 and seq % R_BLOCK == 0 and seq % S5_T == 0
    assert bsz % SUBLANES == 0
    row = lambda t: t.reshape(1, -1)
    m_cols = 4 * GW + 2 * NH
    o_r = m_cols
    o_t = o_r + 4 * GW
    o_s = o_t + 4 * GW

    cos, sin, rot = _rope_tables(seq)
    dm, qd, kd, cd = _retention_tables()

    x2 = x.reshape(bsz * seq, D_MODEL)
    for l in range(depth):
        wl = w_in[l]
        wm = wl[:, 0:4 * GW].astype(BF16)
        wg = _pad_lanes(wl[:, 4 * GW:m_cols]).astype(BF16)
        wr = wl[:, o_r:o_t].astype(BF16)
        wt = wl[:, o_t:o_s].astype(BF16)
        ws = wl[:, o_s:].astype(BF16)
        pm, pg, pr, pt, ps = _inproj(x2, row(norm_mix_pre[l]), wm, wg, wr, wt, ws, bsz, seq)

        gate_bias = _pad_lanes(jnp.concatenate([m_i_bias[l], m_f_bias[l]])[None, :])
        y_m = _mlstm(pm, pg, m_conv_w[l], row(m_conv_b[l]), gate_bias, row(m_norm[l]), bsz, seq)

        zeros = jnp.zeros((DH, GW), F32)
        w_comb = jnp.concatenate([jnp.concatenate([r_w_up[l], zeros], axis=1),
                                  jnp.concatenate([zeros, r_a_up[l]], axis=1)], axis=0)
        y_r = _rwkv(pr, row(r_mu[l]), row(r_w0[l]), row(r_a0[l]), w_comb, r_g_up[l], row(r_k_k[l]),
                    row(r_k_a[l]), row(r_r_k[l]), row(r_ln_g[l]), row(r_ln_b[l]), bsz, seq)

        y_t = _retention(pt, cos, sin, rot, dm, qd, kd, cd, row(t_gn_g[l]), row(t_gn_b[l]), bsz, seq)

        ar, ai, bmat, cmat = _s5_tables(s_lam_re[l], s_lam_im[l], s_log_dt[l], s_b_re[l], s_b_im[l],
                                        s_c_re[l], s_c_im[l])
        y_s = _s5(ps, ar, ai, bmat, cmat, row(s_d[l]), s_w_glu[l].astype(BF16), row(s_b_glu[l]), bsz, seq)

        x2 = _outffn(x2, y_m, y_r, y_t, y_s, w_out[l].astype(BF16),
                     row(norm_mix_post[l]), row(norm_ffn_pre[l]), row(norm_ffn_post[l]),
                     w_ff1[l].astype(BF16), w_ff2[l].astype(BF16), bsz, seq)
    return x2.reshape(bsz, seq, D_MODEL)
```

```python
import jax
import jax.numpy as jnp
from jax import lax
from jax.experimental import pallas as pl
from jax.experimental.pallas import tpu as pltpu

F32 = jnp.float32
BF16 = jnp.bfloat16

D_MODEL = 1024
GW = 256
NH = 4
DH = 64
HEAD_SHIFT = 6
D_FF = 4 * D_MODEL
S5_GROUPS = 16
S5_GROUP = 16
S5_STATE = 64
S5_N = S5_GROUPS * S5_STATE
CONV_WIDTH = 4
ROPE_BASE = 10000.0
RMS_EPS = 1e-6
HEAD_LN_EPS = 1e-5
RWKV_LN_EPS = 64e-5

SUBLANES = 8
LANES = 128
M_CHUNK = 128
M_BLOCK = 1024
R_CHUNK = 64
R_BLOCK = 1024
S5_T = 128
TM_IN = 1024
TM_PROJ = 512
VMEM_LIMIT = 56 * 1024 * 1024


def _cparams(sem):
    return pltpu.CompilerParams(dimension_semantics=sem, vmem_limit_bytes=VMEM_LIMIT)


def _dot(a, b, precision=None):
    return jnp.dot(a, b, preferred_element_type=F32, precision=precision)


def _dot_nt(a, b, precision=None):
    return lax.dot_general(a, b, (((1,), (1,)), ((), ())), preferred_element_type=F32, precision=precision)


def _dot_tn(a, b, precision=None):
    return lax.dot_general(a, b, (((0,), (0,)), ((), ())), preferred_element_type=F32, precision=precision)


_FORMS = {"nn": _dot, "nt": _dot_nt, "tn": _dot_tn}


def _bdot(a, b, form="nn"):
    return _FORMS[form](a.astype(BF16), b.astype(BF16))


def _split_bf16(a, terms):
    parts = []
    for _ in range(terms - 1):
        hi = a.astype(BF16)
        parts.append(hi)
        a = a - hi.astype(F32)
    parts.append(a.astype(BF16))
    return parts


def _sel_dot(sel, b, terms):
    sel = sel.astype(BF16)
    return sum(_dot(sel, p) for p in _split_bf16(b, terms))


def _sigmoid(x):
    return 1.0 / (1.0 + jnp.exp(-x))


def _softplus(x):
    return jnp.maximum(x, 0.0) + jnp.log(1.0 + jnp.exp(-jnp.abs(x)))


def _same_head_matrix(scale):
    r = lax.broadcasted_iota(jnp.int32, (GW, GW), 0)
    c = lax.broadcasted_iota(jnp.int32, (GW, GW), 1)
    return jnp.where((r >> HEAD_SHIFT) == (c >> HEAD_SHIFT), scale, 0.0).astype(F32)


def _head_norm(x, eps):
    avg = _same_head_matrix(1.0 / DH)
    mu = _bdot(x, avg)
    xc = x - mu
    var = _bdot(xc * xc, avg)
    return xc * lax.rsqrt(var + eps)


def _rms(x, g):
    return x * lax.rsqrt(jnp.mean(x * x, axis=-1, keepdims=True) + RMS_EPS) * g


def _inproj_kernel(x_ref, g_ref, wm_ref, wg_ref, wr_ref, wt_ref, ws_ref,
                   pm_ref, pg_ref, pr_ref, pt_ref, ps_ref):
    h = _rms(x_ref[...], g_ref[...]).astype(BF16)
    pm_ref[...] = _dot(h, wm_ref[...])
    pg_ref[...] = _dot(h, wg_ref[...])
    pr_ref[...] = _dot(h, wr_ref[...])
    pt_ref[...] = _dot(h, wt_ref[...])
    ps_ref[...] = _dot(h, ws_ref[...])


def _inproj(x2, g, wm, wg, wr, wt, ws, bsz, seq):
    n = bsz * seq
    tm = TM_IN
    nt = seq // tm
    row = lambda b, i: (b * nt + i, 0)
    const = lambda b, i: (0, 0)
    wspec = lambda w: pl.BlockSpec(w.shape, const, pipeline_mode=pl.Buffered(1))
    return pl.pallas_call(
        _inproj_kernel,
        grid=(bsz, nt),
        in_specs=[pl.BlockSpec((tm, D_MODEL), row), pl.BlockSpec((1, D_MODEL), const),
                  wspec(wm), wspec(wg), wspec(wr), wspec(wt), wspec(ws)],
        out_specs=[pl.BlockSpec((tm, 4 * GW), row), pl.BlockSpec((tm, LANES), row),
                   pl.BlockSpec((tm, 4 * GW), row), pl.BlockSpec((tm, 4 * GW), row),
                   pl.BlockSpec((tm, GW), row)],
        out_shape=[jax.ShapeDtypeStruct((n, 4 * GW), F32), jax.ShapeDtypeStruct((n, LANES), F32),
                   jax.ShapeDtypeStruct((n, 4 * GW), F32), jax.ShapeDtypeStruct((n, 4 * GW), F32),
                   jax.ShapeDtypeStruct((n, GW), F32)],
        compiler_params=_cparams(("parallel", "parallel")),
        name="inproj",
    )(x2, g, wm, wg, wr, wt, ws)


def _mlstm_kernel(pm_ref, pg_ref, cw_ref, cb_ref, gb_ref, ng_ref, y_ref, xs_ref, ct_ref, n_ref, m_ref):
    MB, L = M_BLOCK, M_CHUNK
    nc = MB // L

    @pl.when(pl.program_id(1) == 0)
    def _():
        xs_ref[0:SUBLANES, :] = jnp.zeros((SUBLANES, 2 * GW), F32)
        ct_ref[...] = jnp.zeros_like(ct_ref)
        n_ref[...] = jnp.zeros_like(n_ref)
        m_ref[...] = jnp.zeros_like(m_ref)

    xqk = pm_ref[:, 0:2 * GW]
    xs_ref[SUBLANES:SUBLANES + MB, :] = xqk
    conv = cb_ref[...]
    for j in range(CONV_WIDTH):
        conv = conv + xs_ref[pl.ds(SUBLANES - (CONV_WIDTH - 1) + j, MB), :] * cw_ref[j:j + 1, :]
    xs_ref[0:SUBLANES, :] = xqk[MB - SUBLANES:MB, :]
    qk = conv * _sigmoid(conv)
    q_t = qk[:, 0:GW].T
    k_all = qk[:, GW:2 * GW] * (DH ** -0.5)
    v_t = pm_ref[:, 2 * GW:3 * GW].T

    lane = lax.broadcasted_iota(jnp.int32, (1, LANES), 1)
    g2 = pg_ref[...] + gb_ref[...]
    lf = jnp.where((lane >= NH) & (lane < 2 * NH), -_softplus(-g2), 0.0)
    ri = lax.broadcasted_iota(jnp.int32, (L, L), 0)
    ci = lax.broadcasted_iota(jnp.int32, (L, L), 1)
    tril = jnp.where(ri >= ci, 1.0, 0.0)
    gcum = jnp.concatenate([_sel_dot(tril, lf[c * L:(c + 1) * L], 2) for c in range(nc)], axis=0)
    colmat = jnp.where(lane < NH, g2, gcum)
    dcol = colmat - pltpu.roll(colmat, LANES - NH, 1)
    rowmat = colmat.T
    g_rows = rowmat[NH:2 * NH]
    c_rows = rowmat[0:NH] - g_rows

    causal = ri <= ci
    blocks = [(c, h) for c in range(nc) for h in range(NH)]
    tsl = lambda c: slice(c * L, (c + 1) * L)
    hsl = lambda h: slice(h * DH, (h + 1) * DH)
    k_b = {b: k_all[tsl(b[0]), hsl(b[1])].astype(BF16) for b in blocks}
    q_b = {b: q_t[hsl(b[1]), tsl(b[0])].astype(BF16) for b in blocks}
    v_b = {b: v_t[hsl(b[1]), tsl(b[0])] for b in blocks}
    g_row = {b: g_rows[b[1]:b[1] + 1, tsl(b[0])] for b in blocks}

    m_loc, num_loc, den_loc, g_end, m_add, c_add, n_add = {}, {}, {}, {}, {}, {}, {}
    for b in blocks:
        c, h = b
        d_t = jnp.where(causal, g_row[b] + dcol[tsl(c), h:h + 1], -jnp.inf)
        m_loc[b] = jnp.max(d_t, axis=0, keepdims=True)
        s_t = _dot(k_b[b], q_b[b]) * jnp.exp(d_t - m_loc[b])
        num_loc[b] = _dot(v_b[b].astype(BF16), s_t.astype(BF16))
        den_loc[b] = jnp.sum(s_t, axis=0, keepdims=True)
        g_end[b] = g_row[b][:, L - 1:L]
        d_state = g_end[b] + c_rows[h:h + 1, tsl(c)]
        m_add[b] = jnp.max(d_state, axis=1, keepdims=True)
        w_row = jnp.exp(d_state - m_add[b])
        c_add[b] = _dot((v_b[b] * w_row).astype(BF16), k_b[b])
        n_add[b] = _dot(jnp.broadcast_to(w_row, (SUBLANES, L)).astype(BF16), k_b[b])

    ct = [ct_ref[h] for h in range(NH)]
    n8 = [n_ref[h] for h in range(NH)]
    m_st = [m_ref[h][:, 0:1] for h in range(NH)]
    h_rows = []
    for c in range(nc):
        h_cols = []
        for h in range(NH):
            b = (c, h)
            d_inter = g_row[b] + m_st[h]
            m_row = jnp.maximum(d_inter, m_loc[b])
            e_loc = jnp.exp(m_loc[b] - m_row)
            w_inter = jnp.exp(d_inter - m_row)
            num = e_loc * num_loc[b] + w_inter * _dot(ct[h].astype(BF16), q_b[b])
            den = e_loc * den_loc[b] + w_inter * _dot(n8[h].astype(BF16), q_b[b])[0:1]
            h_cols.append(num / jnp.maximum(jnp.abs(den), jnp.exp(-m_row)))

            m_new = jnp.maximum(g_end[b] + m_st[h], m_add[b])
            keep = jnp.exp(g_end[b] + m_st[h] - m_new)
            take = jnp.exp(m_add[b] - m_new)
            ct[h] = keep * ct[h] + take * c_add[b]
            n8[h] = keep * n8[h] + take * n_add[b]
            m_st[h] = m_new
        h_rows.append(jnp.concatenate(h_cols, axis=0))
    for h in range(NH):
        ct_ref[h] = ct[h]
        n_ref[h] = n8[h]
        m_ref[h] = jnp.broadcast_to(m_st[h], (1, LANES))
    hh = jnp.concatenate(h_rows, axis=1).T
    og = _sigmoid(pm_ref[:, 3 * GW:4 * GW])
    y_ref[...] = _head_norm(hh, HEAD_LN_EPS) * ng_ref[...] * og


def _mlstm(pm, pg, cw, cb, gb, ng, bsz, seq):
    MB = M_BLOCK
    nb = seq // MB
    row = lambda b, i: (b * nb + i, 0)
    const = lambda b, i: (0, 0)
    return pl.pallas_call(
        _mlstm_kernel,
        grid=(bsz, nb),
        in_specs=[pl.BlockSpec((MB, 4 * GW), row), pl.BlockSpec((MB, LANES), row),
                  pl.BlockSpec(cw.shape, const), pl.BlockSpec(cb.shape, const),
                  pl.BlockSpec(gb.shape, const), pl.BlockSpec(ng.shape, const)],
        out_specs=pl.BlockSpec((MB, GW), row),
        out_shape=jax.ShapeDtypeStruct((bsz * seq, GW), F32),
        scratch_shapes=[pltpu.VMEM((SUBLANES + MB, 2 * GW), F32), pltpu.VMEM((NH, DH, DH), F32),
                        pltpu.VMEM((NH, SUBLANES, DH), F32), pltpu.VMEM((NH, 1, LANES), F32)],
        compiler_params=_cparams(("parallel", "arbitrary")),
        name="mlstm",
    )(pm, pg, cw, cb, gb, ng)


def _to_blocks(x, rows):
    return jnp.stack([x[c * rows:(c + 1) * rows, h * DH:(h + 1) * DH]
                      for c in range(x.shape[0] // rows) for h in range(NH)], axis=0)


def _from_heads(x):
    return jnp.concatenate([x[h] for h in range(NH)], axis=-1)


def _bmm(a, b):
    return jnp.einsum('hij,hjk->hik', a.astype(BF16), b.astype(BF16), preferred_element_type=F32)


def _bmm_nt(a, b):
    return jnp.einsum('hik,hjk->hij', a.astype(BF16), b.astype(BF16), preferred_element_type=F32)


def _bmm_tn(a, b):
    return jnp.einsum('hki,hkj->hij', a.astype(BF16), b.astype(BF16), preferred_element_type=F32)


def _unit_lower_inverse(a, ri, ci, log2_n):
    same = lambda s: (ri >> s) == (ci >> s)
    eye = jnp.where(ri == ci, 1.0, 0.0).astype(F32)
    a8 = jnp.where(same(3), a, 0.0)
    a8_2 = _bmm(a8, a8)
    a8_4 = _bmm(a8_2, a8_2)
    inv = _bmm(_bmm(eye - a8, eye + a8_2), eye + a8_4)
    for s in range(3, log2_n):
        off = jnp.where(same(s + 1) & jnp.logical_not(same(s)), a, 0.0)
        inv = inv - _bmm(inv, _bmm(off, inv))
    return inv


def _rwkv_kernel(pr_ref, mu_ref, w0_ref, a0_ref, wc_ref, gup_ref, kk_ref, ka_ref, rk_ref, lg_ref, lb_ref,
                 y_ref, xs_ref, st_ref):
    RB, L = R_BLOCK, R_CHUNK
    log2_l = L.bit_length() - 1

    @pl.when(pl.program_id(1) == 0)
    def _():
        xs_ref[...] = jnp.zeros_like(xs_ref)
        st_ref[...] = jnp.zeros_like(st_ref)

    pr = pr_ref[...]
    rolled = pltpu.roll(pr, 1, 0)
    row0 = lax.broadcasted_iota(jnp.int32, (SUBLANES, 1), 0) == 0
    head = jnp.where(row0, xs_ref[SUBLANES - 1:SUBLANES, :], rolled[0:SUBLANES])
    shifted = jnp.concatenate([head, rolled[SUBLANES:RB]], axis=0)
    xs_ref[...] = pr[RB - SUBLANES:RB, :]
    prm = pr + mu_ref[...] * (shifted - pr)

    r = prm[:, 0:GW]
    k = prm[:, GW:2 * GW]
    v = prm[:, 2 * GW:3 * GW]
    z = prm[:, 3 * GW:3 * GW + LANES]
    lane = lax.broadcasted_iota(jnp.int32, (1, LANES), 1)
    wa = _bdot(jnp.where(lane < DH, jnp.tanh(z), z), wc_ref[...])
    w_log = -_softplus(-(w0_ref[...] + wa[:, 0:GW])) - 0.5
    logw = -jnp.exp(w_log)
    a = _sigmoid(a0_ref[...] + wa[:, GW:2 * GW])
    g = _bdot(_sigmoid(prm[:, 3 * GW + LANES:4 * GW]), gup_ref[...])

    ones_bd = _same_head_matrix(1.0)
    kk = k * kk_ref[...]
    kk = kk / jnp.maximum(jnp.sqrt(_bdot(kk * kk, ones_bd)), 1e-12)
    k2 = k * (1.0 + (a - 1.0) * ka_ref[...])
    bonus = _bdot(r * k2 * rk_ref[...], ones_bd) * v
    b = kk * a

    nc = RB // L
    rl = lax.broadcasted_iota(jnp.int32, (L, L), 0)
    cl = lax.broadcasted_iota(jnp.int32, (L, L), 1)
    tril = jnp.where(rl >= cl, 1.0, 0.0)
    cum_c = [_sel_dot(tril, logw[c * L:(c + 1) * L], 2) for c in range(nc)]
    cum = jnp.concatenate(cum_c, axis=0)
    p_tot_c = [jnp.exp(cc[L - 1:L]) for cc in cum_c]
    p_inv = jnp.exp(-cum)
    kkd = _to_blocks(kk * jnp.exp(cum - logw), L).astype(BF16)
    rd = _to_blocks(r * jnp.exp(cum), L)
    p_tot = _to_blocks(jnp.concatenate(p_tot_c, axis=0), 1)
    bi_f = _to_blocks(b * p_inv, L)
    ki_f = _to_blocks(k2 * p_inv, L)
    bi, ki = bi_f.astype(BF16), ki_f.astype(BF16)
    be = (bi_f * p_tot).astype(BF16)
    ke = (ki_f * p_tot).astype(BF16)
    vh = _to_blocks(v, L).astype(BF16)

    strict = rl > cl
    incl = rl >= cl

    kr = jnp.concatenate([kkd, rd.astype(BF16)], axis=1)
    x = _bmm_nt(kr, jnp.concatenate([bi, ki], axis=1))
    a_b = jnp.where(strict, x[:, 0:L, 0:L], 0.0)
    b_b = jnp.where(incl, x[:, L:2 * L, 0:L], 0.0)
    a_k = jnp.where(strict, x[:, 0:L, L:2 * L], 0.0)
    b_k = jnp.where(incl, x[:, L:2 * L, L:2 * L], 0.0)
    t_inv = _unit_lower_inverse(a_b, rl, cl, log2_l)
    tw = _bmm(t_inv, jnp.concatenate([kkd, _bmm(a_k, vh).astype(BF16)], axis=2))
    bbw = _bmm(b_b, tw)
    q_eff = rd - bbw[:, :, 0:L]
    y_0 = _bmm(b_k, vh) - bbw[:, :, L:2 * L]
    bew = _bmm_tn(be, tw)
    m_t = jnp.where(rl == cl, p_tot, 0.0) - bew[:, :, 0:L]
    n_t = _bmm_tn(ke, vh) - bew[:, :, L:2 * L]

    pairs = lambda t: t.reshape((nc // 2, 2 * NH) + t.shape[1:])
    m_p, n_p, q_p, y_p = pairs(m_t), pairs(n_t), pairs(q_eff), pairs(y_0)
    m_a, m_b = m_p[:, 0:NH].reshape(-1, L, L), m_p[:, NH:2 * NH].reshape(-1, L, L)
    n_a, n_b = n_p[:, 0:NH].reshape(-1, L, L), n_p[:, NH:2 * NH].reshape(-1, L, L)
    m_ab = _bmm(m_b, m_a)
    n_ab = _bmm(m_b, n_a) + n_b

    st = st_ref[...]
    st_a = []
    for j in range(nc // 2):
        blk = slice(j * NH, (j + 1) * NH)
        st_a.append(st)
        st = _bmm(m_ab[blk], st) + n_ab[blk]
    st_ref[...] = st
    st_a = jnp.concatenate(st_a, axis=0)
    st_b = _bmm(m_a, st_a) + n_a
    y_a = _bmm(q_p[:, 0:NH].reshape(-1, L, L), st_a) + y_p[:, 0:NH].reshape(-1, L, L)
    y_b = _bmm(q_p[:, NH:2 * NH].reshape(-1, L, L), st_b) + y_p[:, NH:2 * NH].reshape(-1, L, L)
    ys = []
    for j in range(nc // 2):
        blk = slice(j * NH, (j + 1) * NH)
        ys.append(_from_heads(y_a[blk]))
        ys.append(_from_heads(y_b[blk]))

    y = _head_norm(jnp.concatenate(ys, axis=0), RWKV_LN_EPS) * lg_ref[...] + lb_ref[...]
    y_ref[...] = (y + bonus) * g


def _rwkv(pr, mu, w0, a0, wc, gup, kk, ka, rk, lg, lb, bsz, seq):
    RB = R_BLOCK
    nb = seq // RB
    row = lambda b, i: (b * nb + i, 0)
    const = lambda b, i: (0, 0)
    params = (mu, w0, a0, wc, gup, kk, ka, rk, lg, lb)
    return pl.pallas_call(
        _rwkv_kernel,
        grid=(bsz, nb),
        in_specs=[pl.BlockSpec((RB, 4 * GW), row)] + [pl.BlockSpec(p.shape, const) for p in params],
        out_specs=pl.BlockSpec((RB, GW), row),
        out_shape=jax.ShapeDtypeStruct((bsz * seq, GW), F32),
        scratch_shapes=[pltpu.VMEM((SUBLANES, 4 * GW), F32), pltpu.VMEM((NH, DH, DH), F32)],
        compiler_params=_cparams(("parallel", "arbitrary")),
        name="rwkv",
    )(pr, *params)


def _ret_kernel(pt_ref, cos_ref, sin_ref, rot_ref, dm_ref, qd_ref, kd_ref, cd_ref, gg_ref, gb_ref,
                y_ref, rt_ref):
    MB, L = M_BLOCK, M_CHUNK
    nc = MB // L

    @pl.when(pl.program_id(1) == 0)
    def _():
        rt_ref[...] = jnp.zeros_like(rt_ref)

    cos = cos_ref[...]
    sin = sin_ref[...]
    rot = rot_ref[...]

    def rope(t):
        return t * cos + sum(_dot(p, rot) for p in _split_bf16(t, 2)) * sin

    q_t = rope(pt_ref[:, 0:GW]).T
    k_all = rope(pt_ref[:, GW:2 * GW]) * (DH ** -0.5)
    v_t = pt_ref[:, 2 * GW:3 * GW].T

    blocks = [(c, h) for c in range(nc) for h in range(NH)]
    q_b, o_loc, r_add = {}, {}, {}
    for b in blocks:
        c, h = b
        tsl, hsl = slice(c * L, (c + 1) * L), slice(h * DH, (h + 1) * DH)
        k_b = k_all[tsl, hsl].astype(BF16)
        q_b[b] = q_t[hsl, tsl].astype(BF16)
        v_b = v_t[hsl, tsl]
        s_t = _dot(k_b, q_b[b]) * dm_ref[h]
        o_loc[b] = _dot(v_b.astype(BF16), s_t.astype(BF16))
        r_add[b] = _dot((v_b * kd_ref[h:h + 1, :]).astype(BF16), k_b)

    rt = [rt_ref[h] for h in range(NH)]
    o_rows = []
    for c in range(nc):
        o_cols = []
        for h in range(NH):
            b = (c, h)
            o_cols.append(o_loc[b] + _dot(rt[h].astype(BF16), q_b[b]) * qd_ref[h:h + 1, :])
            rt[h] = rt[h] * cd_ref[h] + r_add[b]
        o_rows.append(jnp.concatenate(o_cols, axis=0))
    for h in range(NH):
        rt_ref[h] = rt[h]
    o = jnp.concatenate(o_rows, axis=1).T
    gate = pt_ref[:, 3 * GW:4 * GW]
    y_ref[...] = gate * _sigmoid(gate) * (_head_norm(o, HEAD_LN_EPS) * gg_ref[...] + gb_ref[...])


def _retention(pt, cos, sin, rot, dm, qd, kd, cd, gg, gb, bsz, seq):
    MB = M_BLOCK
    nb = seq // MB
    row = lambda b, i: (b * nb + i, 0)
    pos = lambda b, i: (i, 0)
    const2 = lambda b, i: (0, 0)
    const3 = lambda b, i: (0, 0, 0)
    return pl.pallas_call(
        _ret_kernel,
        grid=(bsz, nb),
        in_specs=[pl.BlockSpec((MB, 4 * GW), row),
                  pl.BlockSpec((MB, GW), pos), pl.BlockSpec((MB, GW), pos), pl.BlockSpec(rot.shape, const2),
                  pl.BlockSpec(dm.shape, const3), pl.BlockSpec(qd.shape, const2), pl.BlockSpec(kd.shape, const2),
                  pl.BlockSpec(cd.shape, const3), pl.BlockSpec(gg.shape, const2), pl.BlockSpec(gb.shape, const2)],
        out_specs=pl.BlockSpec((MB, GW), row),
        out_shape=jax.ShapeDtypeStruct((bsz * seq, GW), F32),
        scratch_shapes=[pltpu.VMEM((NH, DH, DH), F32)],
        compiler_params=_cparams(("parallel", "arbitrary")),
        name="retention",
    )(pt, cos, sin, rot, dm, qd, kd, cd, gg, gb)


def _gelu_tanh(x):
    return 0.5 * x * (1.0 + jnp.tanh(0.7978845608028654 * (x + 0.044715 * (x * x * x))))


def _s5_kernel(u_ref, ar_ref, ai_ref, bm_ref, cm_ref, d_ref, wg_ref, bg_ref, y_ref, bu_ref, x_ref):
    nb, T, _ = u_ref.shape

    @pl.when(pl.program_id(0) == 0)
    def _():
        x_ref[...] = jnp.zeros_like(x_ref)

    u = jnp.swapaxes(u_ref[...], 0, 1).reshape(T * nb, GW)
    bu_ref[...] = _dot(u.astype(BF16), bm_ref[...])
    ar = jnp.broadcast_to(ar_ref[...], (nb, S5_N))
    ai = jnp.broadcast_to(ai_ref[...], (nb, S5_N))

    def step(t, carry):
        xr, xi = carry
        r0 = pl.multiple_of(t * nb, nb)
        nr = ar * xr - ai * xi + bu_ref[pl.ds(r0, nb), 0:S5_N]
        ni = ar * xi + ai * xr + bu_ref[pl.ds(r0, nb), S5_N:2 * S5_N]
        bu_ref[pl.ds(r0, nb), 0:S5_N] = nr
        bu_ref[pl.ds(r0, nb), S5_N:2 * S5_N] = ni
        return nr, ni

    xr, xi = lax.fori_loop(0, T, step, (x_ref[:, 0:S5_N], x_ref[:, S5_N:2 * S5_N]), unroll=True)
    x_ref[:, 0:S5_N] = xr
    x_ref[:, S5_N:2 * S5_N] = xi

    y = _dot(bu_ref[...].astype(BF16), cm_ref[...])
    y = _gelu_tanh(y + d_ref[...] * u)
    z = _dot(y.astype(BF16), wg_ref[...]) + bg_ref[...]
    out = z[:, 0:GW] * _sigmoid(z[:, GW:2 * GW])
    y_ref[...] = jnp.swapaxes(out.reshape(T, nb, GW), 0, 1)


def _s5(ps, ar, ai, bm, cm, d, wg, bg, bsz, seq):
    T = S5_T
    const = lambda i: (0, 0)
    blk = pl.BlockSpec((bsz, T, GW), lambda i: (0, i, 0))
    params = (ar, ai, bm, cm, d, wg, bg)
    return pl.pallas_call(
        _s5_kernel,
        grid=(seq // T,),
        in_specs=[blk] + [pl.BlockSpec(p.shape, const) for p in params],
        out_specs=blk,
        out_shape=jax.ShapeDtypeStruct((bsz, seq, GW), F32),
        scratch_shapes=[pltpu.VMEM((T * bsz, 2 * S5_N), F32), pltpu.VMEM((bsz, 2 * S5_N), F32)],
        compiler_params=_cparams(("arbitrary",)),
        name="s5",
    )(ps.reshape(bsz, seq, GW), *params).reshape(bsz * seq, GW)


def _outffn_kernel(x_ref, ym_ref, yr_ref, yt_ref, ys_ref, wo_ref, g1_ref, g2_ref, g3_ref, w1_ref, w2_ref, o_ref):
    y = _dot(ym_ref[...].astype(BF16), wo_ref[0:GW, :])
    y = y + _dot(yr_ref[...].astype(BF16), wo_ref[GW:2 * GW, :])
    y = y + _dot(yt_ref[...].astype(BF16), wo_ref[2 * GW:3 * GW, :])
    y = y + _dot(ys_ref[...].astype(BF16), wo_ref[3 * GW:4 * GW, :])
    x1 = x_ref[...] + _rms(y, g1_ref[...])
    h = _rms(x1, g2_ref[...]).astype(BF16)
    f = jnp.zeros_like(x1)
    for j in range(D_FF // D_MODEL):
        a = jnp.maximum(_dot(h, w1_ref[:, j * D_MODEL:(j + 1) * D_MODEL]), 0.0)
        f = f + _dot((a * a).astype(BF16), w2_ref[j * D_MODEL:(j + 1) * D_MODEL, :])
    o_ref[...] = x1 + _rms(f, g3_ref[...])


def _outffn(x2, ym, yr, yt, ys, wo, g1, g2, g3, w1, w2, bsz, seq):
    tm = TM_PROJ
    nt = seq // tm
    row = lambda b, i: (b * nt + i, 0)
    const = lambda b, i: (0, 0)
    wspec = lambda w: pl.BlockSpec(w.shape, const, pipeline_mode=pl.Buffered(1))
    return pl.pallas_call(
        _outffn_kernel,
        grid=(bsz, nt),
        in_specs=[pl.BlockSpec((tm, D_MODEL), row),
                  pl.BlockSpec((tm, GW), row), pl.BlockSpec((tm, GW), row), pl.BlockSpec((tm, GW), row),
                  pl.BlockSpec((tm, GW), row),
                  wspec(wo), pl.BlockSpec((1, D_MODEL), const), pl.BlockSpec((1, D_MODEL), const),
                  pl.BlockSpec((1, D_MODEL), const), wspec(w1), wspec(w2)],
        out_specs=pl.BlockSpec((tm, D_MODEL), row),
        out_shape=jax.ShapeDtypeStruct((bsz * seq, D_MODEL), F32),
        compiler_params=_cparams(("parallel", "parallel")),
        name="outffn",
    )(x2, ym, yr, yt, ys, wo, g1, g2, g3, w1, w2)


def _rope_tables(seq):
    half = DH // 2
    inv = ROPE_BASE ** (-jnp.arange(half, dtype=F32) / half)
    ang = jnp.arange(seq, dtype=F32)[:, None] * inv[None, :]
    tile = lambda t: jnp.tile(jnp.concatenate([t, t], axis=-1), (1, NH))
    src = lax.broadcasted_iota(jnp.int32, (GW, GW), 0)
    dst = lax.broadcasted_iota(jnp.int32, (GW, GW), 1)
    first = (dst & (DH - 1)) < half
    rot = jnp.where(first & (src == dst + half), -1.0, jnp.where(jnp.logical_not(first) & (src == dst - half), 1.0, 0.0))
    return tile(jnp.cos(ang)), tile(jnp.sin(ang)), rot.astype(BF16)


def _retention_tables():
    L = M_CHUNK
    log_gamma = jnp.log(1.0 - 2.0 ** (-5.0 - jnp.arange(NH, dtype=F32)))
    idx = jnp.arange(L, dtype=F32)
    diff = idx[:, None] - idx[None, :]
    causal = diff >= 0
    decay_mat = jnp.where(causal, jnp.exp(jnp.where(causal, diff, 0.0) * log_gamma[:, None, None]), 0.0)
    decay_t = jnp.swapaxes(decay_mat, 1, 2)
    q_decay = jnp.exp((idx + 1.0) * log_gamma[:, None])
    k_decay = jnp.exp((L - 1.0 - idx) * log_gamma[:, None])
    chunk_decay = jnp.broadcast_to(jnp.exp(L * log_gamma)[:, None, None], (NH, 1, DH))
    return decay_t, q_decay, k_decay, chunk_decay


def _s5_tables(lam_re, lam_im, log_dt, b_re, b_im, c_re, c_im):
    G, P, C = S5_GROUPS, S5_STATE, S5_GROUP
    dt = jnp.exp(log_dt)[:, None]
    mag = jnp.exp(lam_re * dt)
    abar_re, abar_im = mag * jnp.cos(lam_im * dt), mag * jnp.sin(lam_im * dt)
    den = lam_re * lam_re + lam_im * lam_im
    num_re, num_im = abar_re - 1.0, abar_im
    f_re = (num_re * lam_re + num_im * lam_im) / den
    f_im = (num_im * lam_re - num_re * lam_im) / den
    bbar_re = f_re[..., None] * b_re - f_im[..., None] * b_im
    bbar_im = f_re[..., None] * b_im + f_im[..., None] * b_re
    eye = jnp.eye(G, dtype=F32)
    bd_in = lambda t: jnp.einsum('gpc,gh->gchp', t, eye).reshape(G * C, G * P)
    bd_out = lambda t: jnp.einsum('gcp,gh->gphc', t, eye).reshape(G * P, G * C)
    bmat = jnp.concatenate([bd_in(bbar_re), bd_in(bbar_im)], axis=1)
    cmat = jnp.concatenate([bd_out(c_re), -bd_out(c_im)], axis=0)
    return abar_re.reshape(1, G * P), abar_im.reshape(1, G * P), bmat.astype(BF16), cmat.astype(BF16)


def _pad_lanes(t, width=LANES):
    return jnp.pad(t, ((0, 0), (0, width - t.shape[-1])))


def kernel(x, norm_mix_pre, norm_mix_post, w_in, m_conv_w, m_conv_b, m_i_bias, m_f_bias, m_norm, r_mu, r_w0, r_w_up, r_a0, r_a_up, r_g_up, r_k_k, r_k_a, r_r_k, r_ln_g, r_ln_b, t_gn_g, t_gn_b, s_lam_re, s_lam_im, s_log_dt, s_b_re, s_b_im, s_c_re, s_c_im, s_d, s_w_glu, s_b_glu, w_out, norm_ffn_pre, norm_ffn_post, w_ff1, w_ff2):
    bsz, seq, _ = x.shape
    depth = w_in.shape[0]
    assert seq % TM_IN == 0 and seq % TM_PROJ == 0 and seq % M_BLOCK == 0 and seq % R_BLOCK == 0 and seq % S5_T == 0
    assert bsz % SUBLANES == 0
    row = lambda t: t.reshape(1, -1)
    m_cols = 4 * GW + 2 * NH
    o_r = m_cols
    o_t = o_r + 4 * GW
    o_s = o_t + 4 * GW

    cos, sin, rot = _rope_tables(seq)
    dm, qd, kd, cd = _retention_tables()

    x2 = x.reshape(bsz * seq, D_MODEL)
    for l in range(depth):
        wl = w_in[l]
        wm = wl[:, 0:4 * GW].astype(BF16)
        wg = _pad_lanes(wl[:, 4 * GW:m_cols]).astype(BF16)
        wr = wl[:, o_r:o_t].astype(BF16)
        wt = wl[:, o_t:o_s].astype(BF16)
        ws = wl[:, o_s:].astype(BF16)
        pm, pg, pr, pt, ps = _inproj(x2, row(norm_mix_pre[l]), wm, wg, wr, wt, ws, bsz, seq)

        gate_bias = _pad_lanes(jnp.concatenate([m_i_bias[l], m_f_bias[l]])[None, :])
        y_m = _mlstm(pm, pg, m_conv_w[l], row(m_conv_b[l]), gate_bias, row(m_norm[l]), bsz, seq)

        zeros = jnp.zeros((DH, GW), F32)
        w_comb = jnp.concatenate([jnp.concatenate([r_w_up[l], zeros], axis=1),
                                  jnp.concatenate([zeros, r_a_up[l]], axis=1)], axis=0)
        y_r = _rwkv(pr, row(r_mu[l]), row(r_w0[l]), row(r_a0[l]), w_comb, r_g_up[l], row(r_k_k[l]),
                    row(r_k_a[l]), row(r_r_k[l]), row(r_ln_g[l]), row(r_ln_b[l]), bsz, seq)

        y_t = _retention(pt, cos, sin, rot, dm, qd, kd, cd, row(t_gn_g[l]), row(t_gn_b[l]), bsz, seq)

        ar, ai, bmat, cmat = _s5_tables(s_lam_re[l], s_lam_im[l], s_log_dt[l], s_b_re[l], s_b_im[l],
                                        s_c_re[l], s_c_im[l])
        y_s = _s5(ps, ar, ai, bmat, cmat, row(s_d[l]), s_w_glu[l].astype(BF16), row(s_b_glu[l]), bsz, seq)

        x2 = _outffn(x2, y_m, y_r, y_t, y_s, w_out[l].astype(BF16),
                     row(norm_mix_post[l]), row(norm_ffn_pre[l]), row(norm_ffn_post[l]),
                     w_ff1[l].astype(BF16), w_ff2[l].astype(BF16), bsz, seq)
    return x2.reshape(bsz, seq, D_MODEL)
```

```python
import jax
import jax.numpy as jnp
from jax import lax
from jax.experimental import pallas as pl
from jax.experimental.pallas import tpu as pltpu

F32 = jnp.float32
BF16 = jnp.bfloat16

D_MODEL = 1024
GW = 256
NH = 4
DH = 64
HEAD_SHIFT = 6
D_FF = 4 * D_MODEL
S5_GROUPS = 16
S5_GROUP = 16
S5_STATE = 64
S5_N = S5_GROUPS * S5_STATE
CONV_WIDTH = 4
ROPE_BASE = 10000.0
RMS_EPS = 1e-6
HEAD_LN_EPS = 1e-5
RWKV_LN_EPS = 64e-5

SUBLANES = 8
LANES = 128
M_CHUNK = 128
M_BLOCK = 1024
R_CHUNK = 64
R_BLOCK = 1024
S5_T = 128
TM_IN = 1024
TM_PROJ = 512
VMEM_LIMIT = 56 * 1024 * 1024


def _cparams(sem):
    return pltpu.CompilerParams(dimension_semantics=sem, vmem_limit_bytes=VMEM_LIMIT)


def _dot(a, b, precision=None):
    return jnp.dot(a, b, preferred_element_type=F32, precision=precision)


def _dot_nt(a, b, precision=None):
    return lax.dot_general(a, b, (((1,), (1,)), ((), ())), preferred_element_type=F32, precision=precision)


def _dot_tn(a, b, precision=None):
    return lax.dot_general(a, b, (((0,), (0,)), ((), ())), preferred_element_type=F32, precision=precision)


_FORMS = {"nn": _dot, "nt": _dot_nt, "tn": _dot_tn}


def _bdot(a, b, form="nn"):
    return _FORMS[form](a.astype(BF16), b.astype(BF16))


def _split_bf16(a, terms):
    parts = []
    for _ in range(terms - 1):
        hi = a.astype(BF16)
        parts.append(hi)
        a = a - hi.astype(F32)
    parts.append(a.astype(BF16))
    return parts


def _sel_dot(sel, b, terms):
    sel = sel.astype(BF16)
    return sum(_dot(sel, p) for p in _split_bf16(b, terms))


def _sigmoid(x):
    return 1.0 / (1.0 + jnp.exp(-x))


def _softplus(x):
    return jnp.maximum(x, 0.0) + jnp.log(1.0 + jnp.exp(-jnp.abs(x)))


def _same_head_matrix(scale):
    r = lax.broadcasted_iota(jnp.int32, (GW, GW), 0)
    c = lax.broadcasted_iota(jnp.int32, (GW, GW), 1)
    return jnp.where((r >> HEAD_SHIFT) == (c >> HEAD_SHIFT), scale, 0.0).astype(F32)


def _head_norm(x, eps):
    avg = _same_head_matrix(1.0 / DH)
    mu = _bdot(x, avg)
    xc = x - mu
    var = _bdot(xc * xc, avg)
    return xc * lax.rsqrt(var + eps)


def _rms(x, g):
    return x * lax.rsqrt(jnp.mean(x * x, axis=-1, keepdims=True) + RMS_EPS) * g


def _inproj_kernel(x_ref, g_ref, wm_ref, wg_ref, wr_ref, wt_ref, ws_ref,
                   pm_ref, pg_ref, pr_ref, pt_ref, ps_ref):
    h = _rms(x_ref[...], g_ref[...]).astype(BF16)
    pm_ref[...] = _dot(h, wm_ref[...])
    pg_ref[...] = _dot(h, wg_ref[...])
    pr_ref[...] = _dot(h, wr_ref[...])
    pt_ref[...] = _dot(h, wt_ref[...])
    ps_ref[...] = _dot(h, ws_ref[...])


def _inproj(x2, g, wm, wg, wr, wt, ws, bsz, seq):
    n = bsz * seq
    tm = TM_IN
    nt = seq // tm
    row = lambda b, i: (b * nt + i, 0)
    const = lambda b, i: (0, 0)
    wspec = lambda w: pl.BlockSpec(w.shape, const, pipeline_mode=pl.Buffered(1))
    return pl.pallas_call(
        _inproj_kernel,
        grid=(bsz, nt),
        in_specs=[pl.BlockSpec((tm, D_MODEL), row), pl.BlockSpec((1, D_MODEL), const),
                  wspec(wm), wspec(wg), wspec(wr), wspec(wt), wspec(ws)],
        out_specs=[pl.BlockSpec((tm, 4 * GW), row), pl.BlockSpec((tm, LANES), row),
                   pl.BlockSpec((tm, 4 * GW), row), pl.BlockSpec((tm, 4 * GW), row),
                   pl.BlockSpec((tm, GW), row)],
        out_shape=[jax.ShapeDtypeStruct((n, 4 * GW), F32), jax.ShapeDtypeStruct((n, LANES), F32),
                   jax.ShapeDtypeStruct((n, 4 * GW), F32), jax.ShapeDtypeStruct((n, 4 * GW), F32),
                   jax.ShapeDtypeStruct((n, GW), F32)],
        compiler_params=_cparams(("parallel", "parallel")),
        name="inproj",
    )(x2, g, wm, wg, wr, wt, ws)


def _mlstm_kernel(pm_ref, pg_ref, cw_ref, cb_ref, gb_ref, ng_ref, y_ref, xs_ref, ct_ref, n_ref, m_ref):
    MB, L = M_BLOCK, M_CHUNK
    nc = MB // L

    @pl.when(pl.program_id(1) == 0)
    def _():
        xs_ref[0:SUBLANES, :] = jnp.zeros((SUBLANES, 2 * GW), F32)
        ct_ref[...] = jnp.zeros_like(ct_ref)
        n_ref[...] = jnp.zeros_like(n_ref)
        m_ref[...] = jnp.zeros_like(m_ref)

    xqk = pm_ref[:, 0:2 * GW]
    xs_ref[SUBLANES:SUBLANES + MB, :] = xqk
    conv = cb_ref[...]
    for j in range(CONV_WIDTH):
        conv = conv + xs_ref[pl.ds(SUBLANES - (CONV_WIDTH - 1) + j, MB), :] * cw_ref[j:j + 1, :]
    xs_ref[0:SUBLANES, :] = xqk[MB - SUBLANES:MB, :]
    qk = conv * _sigmoid(conv)
    q_t = qk[:, 0:GW].T
    k_all = qk[:, GW:2 * GW] * (DH ** -0.5)
    v_t = pm_ref[:, 2 * GW:3 * GW].T

    lane = lax.broadcasted_iota(jnp.int32, (1, LANES), 1)
    g2 = pg_ref[...] + gb_ref[...]
    lf = jnp.where((lane >= NH) & (lane < 2 * NH), -_softplus(-g2), 0.0)
    ri = lax.broadcasted_iota(jnp.int32, (L, L), 0)
    ci = lax.broadcasted_iota(jnp.int32, (L, L), 1)
    tril = jnp.where(ri >= ci, 1.0, 0.0)
    gcum = jnp.concatenate([_sel_dot(tril, lf[c * L:(c + 1) * L], 2) for c in range(nc)], axis=0)
    colmat = jnp.where(lane < NH, g2, gcum)
    dcol = colmat - pltpu.roll(colmat, LANES - NH, 1)
    rowmat = colmat.T
    g_rows = rowmat[NH:2 * NH]
    c_rows = rowmat[0:NH] - g_rows

    causal = ri <= ci
    blocks = [(c, h) for c in range(nc) for h in range(NH)]
    tsl = lambda c: slice(c * L, (c + 1) * L)
    hsl = lambda h: slice(h * DH, (h + 1) * DH)
    k_b = {b: k_all[tsl(b[0]), hsl(b[1])].astype(BF16) for b in blocks}
    q_b = {b: q_t[hsl(b[1]), tsl(b[0])].astype(BF16) for b in blocks}
    v_b = {b: v_t[hsl(b[1]), tsl(b[0])] for b in blocks}
    g_row = {b: g_rows[b[1]:b[1] + 1, tsl(b[0])] for b in blocks}

    m_loc, num_loc, den_loc, g_end, m_add, c_add, n_add = {}, {}, {}, {}, {}, {}, {}
    for b in blocks:
        c, h = b
        d_t = jnp.where(causal, g_row[b] + dcol[tsl(c), h:h + 1], -jnp.inf)
        m_loc[b] = jnp.max(d_t, axis=0, keepdims=True)
        s_t = _dot(k_b[b], q_b[b]) * jnp.exp(d_t - m_loc[b])
        num_loc[b] = _dot(v_b[b].astype(BF16), s_t.astype(BF16))
        den_loc[b] = jnp.sum(s_t, axis=0, keepdims=True)
        g_end[b] = g_row[b][:, L - 1:L]
        d_state = g_end[b] + c_rows[h:h + 1, tsl(c)]
        m_add[b] = jnp.max(d_state, axis=1, keepdims=True)
        w_row = jnp.exp(d_state - m_add[b])
        c_add[b] = _dot((v_b[b] * w_row).astype(BF16), k_b[b])
        n_add[b] = _dot(jnp.broadcast_to(w_row, (SUBLANES, L)).astype(BF16), k_b[b])

    ct = [ct_ref[h] for h in range(NH)]
    n8 = [n_ref[h] for h in range(NH)]
    m_st = [m_ref[h][:, 0:1] for h in range(NH)]
    h_rows = []
    for c in range(nc):
        h_cols = []
        for h in range(NH):
            b = (c, h)
            d_inter = g_row[b] + m_st[h]
            m_row = jnp.maximum(d_inter, m_loc[b])
            e_loc = jnp.exp(m_loc[b] - m_row)
            w_inter = jnp.exp(d_inter - m_row)
            num = e_loc * num_loc[b] + w_inter * _dot(ct[h].astype(BF16), q_b[b])
            den = e_loc * den_loc[b] + w_inter * _dot(n8[h].astype(BF16), q_b[b])[0:1]
            h_cols.append(num / jnp.maximum(jnp.abs(den), jnp.exp(-m_row)))

            m_new = jnp.maximum(g_end[b] + m_st[h], m_add[b])
            keep = jnp.exp(g_end[b] + m_st[h] - m_new)
            take = jnp.exp(m_add[b] - m_new)
            ct[h] = keep * ct[h] + take * c_add[b]
            n8[h] = keep * n8[h] + take * n_add[b]
            m_st[h] = m_new
        h_rows.append(jnp.concatenate(h_cols, axis=0))
    for h in range(NH):
        ct_ref[h] = ct[h]
        n_ref[h] = n8[h]
        m_ref[h] = jnp.broadcast_to(m_st[h], (1, LANES))
    hh = jnp.concatenate(h_rows, axis=1).T
    og = _sigmoid(pm_ref[:, 3 * GW:4 * GW])
    y_ref[...] = _head_norm(hh, HEAD_LN_EPS) * ng_ref[...] * og


def _to_blocks(x, rows):
    return jnp.stack([x[c * rows:(c + 1) * rows, h * DH:(h + 1) * DH]
                      for c in range(x.shape[0] // rows) for h in range(NH)], axis=0)


def _from_heads(x):
    return jnp.concatenate([x[h] for h in range(NH)], axis=-1)


def _bmm(a, b):
    return jnp.einsum('hij,hjk->hik', a.astype(BF16), b.astype(BF16), preferred_element_type=F32)


def _bmm_nt(a, b):
    return jnp.einsum('hik,hjk->hij', a.astype(BF16), b.astype(BF16), preferred_element_type=F32)


def _bmm_tn(a, b):
    return jnp.einsum('hki,hkj->hij', a.astype(BF16), b.astype(BF16), preferred_element_type=F32)


def _unit_lower_inverse(a, ri, ci, log2_n):
    same = lambda s: (ri >> s) == (ci >> s)
    eye = jnp.where(ri == ci, 1.0, 0.0).astype(F32)
    a8 = jnp.where(same(3), a, 0.0)
    a8_2 = _bmm(a8, a8)
    a8_4 = _bmm(a8_2, a8_2)
    inv = _bmm(_bmm(eye - a8, eye + a8_2), eye + a8_4)
    for s in range(3, log2_n):
        off = jnp.where(same(s + 1) & jnp.logical_not(same(s)), a, 0.0)
        inv = inv - _bmm(inv, _bmm(off, inv))
    return inv


def _rwkv_kernel(pr_ref, mu_ref, w0_ref, a0_ref, wc_ref, gup_ref, kk_ref, ka_ref, rk_ref, lg_ref, lb_ref,
                 y_ref, xs_ref, st_ref):
    RB, L = R_BLOCK, R_CHUNK
    log2_l = L.bit_length() - 1

    @pl.when(pl.program_id(1) == 0)
    def _():
        xs_ref[...] = jnp.zeros_like(xs_ref)
        st_ref[...] = jnp.zeros_like(st_ref)

    pr = pr_ref[...]
    rolled = pltpu.roll(pr, 1, 0)
    row0 = lax.broadcasted_iota(jnp.int32, (SUBLANES, 1), 0) == 0
    head = jnp.where(row0, xs_ref[SUBLANES - 1:SUBLANES, :], rolled[0:SUBLANES])
    shifted = jnp.concatenate([head, rolled[SUBLANES:RB]], axis=0)
    xs_ref[...] = pr[RB - SUBLANES:RB, :]
    prm = pr + mu_ref[...] * (shifted - pr)

    r = prm[:, 0:GW]
    k = prm[:, GW:2 * GW]
    v = prm[:, 2 * GW:3 * GW]
    z = prm[:, 3 * GW:3 * GW + LANES]
    lane = lax.broadcasted_iota(jnp.int32, (1, LANES), 1)
    wa = _bdot(jnp.where(lane < DH, jnp.tanh(z), z), wc_ref[...])
    w_log = -_softplus(-(w0_ref[...] + wa[:, 0:GW])) - 0.5
    logw = -jnp.exp(w_log)
    a = _sigmoid(a0_ref[...] + wa[:, GW:2 * GW])
    g = _bdot(_sigmoid(prm[:, 3 * GW + LANES:4 * GW]), gup_ref[...])

    ones_bd = _same_head_matrix(1.0)
    kk = k * kk_ref[...]
    kk = kk / jnp.maximum(jnp.sqrt(_bdot(kk * kk, ones_bd)), 1e-12)
    k2 = k * (1.0 + (a - 1.0) * ka_ref[...])
    bonus = _bdot(r * k2 * rk_ref[...], ones_bd) * v
    b = kk * a

    nc = RB // L
    rl = lax.broadcasted_iota(jnp.int32, (L, L), 0)
    cl = lax.broadcasted_iota(jnp.int32, (L, L), 1)
    tril = jnp.where(rl >= cl, 1.0, 0.0)
    cum_c = [_sel_dot(tril, logw[c * L:(c + 1) * L], 2) for c in range(nc)]
    cum = jnp.concatenate(cum_c, axis=0)
    p_tot_c = [jnp.exp(cc[L - 1:L]) for cc in cum_c]
    p_inv = jnp.exp(-cum)
    kkd = _to_blocks(kk * jnp.exp(cum - logw), L).astype(BF16)
    rd = _to_blocks(r * jnp.exp(cum), L)
    p_tot = _to_blocks(jnp.concatenate(p_tot_c, axis=0), 1)
    bi_f = _to_blocks(b * p_inv, L)
    ki_f = _to_blocks(k2 * p_inv, L)
    bi, ki = bi_f.astype(BF16), ki_f.astype(BF16)
    be = (bi_f * p_tot).astype(BF16)
    ke = (ki_f * p_tot).astype(BF16)
    vh = _to_blocks(v, L).astype(BF16)

    strict = rl > cl
    incl = rl >= cl

    kr = jnp.concatenate([kkd, rd.astype(BF16)], axis=1)
    x = _bmm_nt(kr, jnp.concatenate([bi, ki], axis=1))
    a_b = jnp.where(strict, x[:, 0:L, 0:L], 0.0)
    b_b = jnp.where(incl, x[:, L:2 * L, 0:L], 0.0)
    a_k = jnp.where(strict, x[:, 0:L, L:2 * L], 0.0)
    b_k = jnp.where(incl, x[:, L:2 * L, L:2 * L], 0.0)
    t_inv = _unit_lower_inverse(a_b, rl, cl, log2_l)
    tw = _bmm(t_inv, jnp.concatenate([kkd, _bmm(a_k, vh).astype(BF16)], axis=2))
    bbw = _bmm(b_b, tw)
    q_eff = rd - bbw[:, :, 0:L]
    y_0 = _bmm(b_k, vh) - bbw[:, :, L:2 * L]
    bew = _bmm_tn(be, tw)
    m_t = jnp.where(rl == cl, p_tot, 0.0) - bew[:, :, 0:L]
    n_t = _bmm_tn(ke, vh) - bew[:, :, L:2 * L]

    pairs = lambda t: t.reshape((nc // 2, 2 * NH) + t.shape[1:])
    m_p, n_p, q_p, y_p = pairs(m_t), pairs(n_t), pairs(q_eff), pairs(y_0)
    m_a, m_b = m_p[:, 0:NH].reshape(-1, L, L), m_p[:, NH:2 * NH].reshape(-1, L, L)
    n_a, n_b = n_p[:, 0:NH].reshape(-1, L, L), n_p[:, NH:2 * NH].reshape(-1, L, L)
    m_ab = _bmm(m_b, m_a)
    n_ab = _bmm(m_b, n_a) + n_b

    st = st_ref[...]
    st_a = []
    for j in range(nc // 2):
        blk = slice(j * NH, (j + 1) * NH)
        st_a.append(st)
        st = _bmm(m_ab[blk], st) + n_ab[blk]
    st_ref[...] = st
    st_a = jnp.concatenate(st_a, axis=0)
    st_b = _bmm(m_a, st_a) + n_a
    y_a = _bmm(q_p[:, 0:NH].reshape(-1, L, L), st_a) + y_p[:, 0:NH].reshape(-1, L, L)
    y_b = _bmm(q_p[:, NH:2 * NH].reshape(-1, L, L), st_b) + y_p[:, NH:2 * NH].reshape(-1, L, L)
    ys = []
    for j in range(nc // 2):
        blk = slice(j * NH, (j + 1) * NH)
        ys.append(_from_heads(y_a[blk]))
        ys.append(_from_heads(y_b[blk]))

    y = _head_norm(jnp.concatenate(ys, axis=0), RWKV_LN_EPS) * lg_ref[...] + lb_ref[...]
    y_ref[...] = (y + bonus) * g


def _rwkv(pr, mu, w0, a0, wc, gup, kk, ka, rk, lg, lb, bsz, seq):
    RB = R_BLOCK
    nb = seq // RB
    row = lambda b, i: (b * nb + i, 0)
    const = lambda b, i: (0, 0)
    params = (mu, w0, a0, wc, gup, kk, ka, rk, lg, lb)
    return pl.pallas_call(
        _rwkv_kernel,
        grid=(bsz, nb),
        in_specs=[pl.BlockSpec((RB, 4 * GW), row)] + [pl.BlockSpec(p.shape, const) for p in params],
        out_specs=pl.BlockSpec((RB, GW), row),
        out_shape=jax.ShapeDtypeStruct((bsz * seq, GW), F32),
        scratch_shapes=[pltpu.VMEM((SUBLANES, 4 * GW), F32), pltpu.VMEM((NH, DH, DH), F32)],
        compiler_params=_cparams(("parallel", "arbitrary")),
        name="rwkv",
    )(pr, *params)


def _ret_kernel(pt_ref, cos_ref, sin_ref, rot_ref, dm_ref, qd_ref, kd_ref, cd_ref, gg_ref, gb_ref,
                y_ref, rt_ref):
    MB, L = M_BLOCK, M_CHUNK
    nc = MB // L

    @pl.when(pl.program_id(1) == 0)
    def _():
        rt_ref[...] = jnp.zeros_like(rt_ref)

    cos = cos_ref[...]
    sin = sin_ref[...]
    rot = rot_ref[...]

    def rope(t):
        return t * cos + sum(_dot(p, rot) for p in _split_bf16(t, 2)) * sin

    q_t = rope(pt_ref[:, 0:GW]).T
    k_all = rope(pt_ref[:, GW:2 * GW]) * (DH ** -0.5)
    v_t = pt_ref[:, 2 * GW:3 * GW].T

    blocks = [(c, h) for c in range(nc) for h in range(NH)]
    q_b, o_loc, r_add = {}, {}, {}
    for b in blocks:
        c, h = b
        tsl, hsl = slice(c * L, (c + 1) * L), slice(h * DH, (h + 1) * DH)
        k_b = k_all[tsl, hsl].astype(BF16)
        q_b[b] = q_t[hsl, tsl].astype(BF16)
        v_b = v_t[hsl, tsl]
        s_t = _dot(k_b, q_b[b]) * dm_ref[h]
        o_loc[b] = _dot(v_b.astype(BF16), s_t.astype(BF16))
        r_add[b] = _dot((v_b * kd_ref[h:h + 1, :]).astype(BF16), k_b)

    rt = [rt_ref[h] for h in range(NH)]
    o_rows = []
    for c in range(nc):
        o_cols = []
        for h in range(NH):
            b = (c, h)
            o_cols.append(o_loc[b] + _dot(rt[h].astype(BF16), q_b[b]) * qd_ref[h:h + 1, :])
            rt[h] = rt[h] * cd_ref[h] + r_add[b]
        o_rows.append(jnp.concatenate(o_cols, axis=0))
    for h in range(NH):
        rt_ref[h] = rt[h]
    o = jnp.concatenate(o_rows, axis=1).T
    gate = pt_ref[:, 3 * GW:4 * GW]
    y_ref[...] = gate * _sigmoid(gate) * (_head_norm(o, HEAD_LN_EPS) * gg_ref[...] + gb_ref[...])


def _mlstm_ret_kernel(pm_ref, pg_ref, cw_ref, cb_ref, gb_ref, ng_ref,
                      pt_ref, cos_ref, sin_ref, rot_ref, dm_ref, qd_ref, kd_ref, cd_ref, gg_ref, tb_ref,
                      ym_ref, yt_ref, xs_ref, ct_ref, n_ref, m_ref, rt_ref):
    _mlstm_kernel(pm_ref, pg_ref, cw_ref, cb_ref, gb_ref, ng_ref, ym_ref, xs_ref, ct_ref, n_ref, m_ref)
    _ret_kernel(pt_ref, cos_ref, sin_ref, rot_ref, dm_ref, qd_ref, kd_ref, cd_ref, gg_ref, tb_ref, yt_ref, rt_ref)


def _mlstm_ret(pm, pg, cw, cb, gb, ng, pt, cos, sin, rot, dm, qd, kd, cd, gg, tb, bsz, seq):
    MB = M_BLOCK
    nb = seq // MB
    row = lambda b, i: (b * nb + i, 0)
    pos = lambda b, i: (i, 0)
    full = lambda t: pl.BlockSpec(t.shape, lambda b, i: (0,) * t.ndim)
    out = jax.ShapeDtypeStruct((bsz * seq, GW), F32)
    return pl.pallas_call(
        _mlstm_ret_kernel,
        grid=(bsz, nb),
        in_specs=[pl.BlockSpec((MB, 4 * GW), row), pl.BlockSpec((MB, LANES), row),
                  full(cw), full(cb), full(gb), full(ng),
                  pl.BlockSpec((MB, 4 * GW), row), pl.BlockSpec((MB, GW), pos), pl.BlockSpec((MB, GW), pos),
                  full(rot), full(dm), full(qd), full(kd), full(cd), full(gg), full(tb)],
        out_specs=[pl.BlockSpec((MB, GW), row), pl.BlockSpec((MB, GW), row)],
        out_shape=[out, out],
        scratch_shapes=[pltpu.VMEM((SUBLANES + MB, 2 * GW), F32), pltpu.VMEM((NH, DH, DH), F32),
                        pltpu.VMEM((NH, SUBLANES, DH), F32), pltpu.VMEM((NH, 1, LANES), F32),
                        pltpu.VMEM((NH, DH, DH), F32)],
        compiler_params=_cparams(("parallel", "arbitrary")),
        name="mlstm_ret",
    )(pm, pg, cw, cb, gb, ng, pt, cos, sin, rot, dm, qd, kd, cd, gg, tb)


def _gelu_tanh(x):
    return 0.5 * x * (1.0 + jnp.tanh(0.7978845608028654 * (x + 0.044715 * (x * x * x))))


def _s5_kernel(u_ref, ar_ref, ai_ref, bm_ref, cm_ref, d_ref, wg_ref, bg_ref, y_ref, bu_ref, x_ref):
    nb, T, _ = u_ref.shape

    @pl.when(pl.program_id(0) == 0)
    def _():
        x_ref[...] = jnp.zeros_like(x_ref)

    u = jnp.swapaxes(u_ref[...], 0, 1).reshape(T * nb, GW)
    bu_ref[...] = _dot(u.astype(BF16), bm_ref[...])
    ar = jnp.broadcast_to(ar_ref[...], (nb, S5_N))
    ai = jnp.broadcast_to(ai_ref[...], (nb, S5_N))

    def step(t, carry):
        xr, xi = carry
        r0 = pl.multiple_of(t * nb, nb)
        nr = ar * xr - ai * xi + bu_ref[pl.ds(r0, nb), 0:S5_N]
        ni = ar * xi + ai * xr + bu_ref[pl.ds(r0, nb), S5_N:2 * S5_N]
        bu_ref[pl.ds(r0, nb), 0:S5_N] = nr
        bu_ref[pl.ds(r0, nb), S5_N:2 * S5_N] = ni
        return nr, ni

    xr, xi = lax.fori_loop(0, T, step, (x_ref[:, 0:S5_N], x_ref[:, S5_N:2 * S5_N]), unroll=True)
    x_ref[:, 0:S5_N] = xr
    x_ref[:, S5_N:2 * S5_N] = xi

    y = _dot(bu_ref[...].astype(BF16), cm_ref[...])
    y = _gelu_tanh(y + d_ref[...] * u)
    z = _dot(y.astype(BF16), wg_ref[...]) + bg_ref[...]
    out = z[:, 0:GW] * _sigmoid(z[:, GW:2 * GW])
    y_ref[...] = jnp.swapaxes(out.reshape(T, nb, GW), 0, 1)


def _s5(ps, ar, ai, bm, cm, d, wg, bg, bsz, seq):
    T = S5_T
    const = lambda i: (0, 0)
    blk = pl.BlockSpec((bsz, T, GW), lambda i: (0, i, 0))
    params = (ar, ai, bm, cm, d, wg, bg)
    return pl.pallas_call(
        _s5_kernel,
        grid=(seq // T,),
        in_specs=[blk] + [pl.BlockSpec(p.shape, const) for p in params],
        out_specs=blk,
        out_shape=jax.ShapeDtypeStruct((bsz, seq, GW), F32),
        scratch_shapes=[pltpu.VMEM((T * bsz, 2 * S5_N), F32), pltpu.VMEM((bsz, 2 * S5_N), F32)],
        compiler_params=_cparams(("arbitrary",)),
        name="s5",
    )(ps.reshape(bsz, seq, GW), *params).reshape(bsz * seq, GW)


def _outffn_kernel(x_ref, ym_ref, yr_ref, yt_ref, ys_ref, wo_ref, g1_ref, g2_ref, g3_ref, w1_ref, w2_ref, o_ref):
    y = _dot(ym_ref[...].astype(BF16), wo_ref[0:GW, :])
    y = y + _dot(yr_ref[...].astype(BF16), wo_ref[GW:2 * GW, :])
    y = y + _dot(yt_ref[...].astype(BF16), wo_ref[2 * GW:3 * GW, :])
    y = y + _dot(ys_ref[...].astype(BF16), wo_ref[3 * GW:4 * GW, :])
    x1 = x_ref[...] + _rms(y, g1_ref[...])
    h = _rms(x1, g2_ref[...]).astype(BF16)
    f = jnp.zeros_like(x1)
    for j in range(D_FF // D_MODEL):
        a = jnp.maximum(_dot(h, w1_ref[:, j * D_MODEL:(j + 1) * D_MODEL]), 0.0)
        f = f + _dot((a * a).astype(BF16), w2_ref[j * D_MODEL:(j + 1) * D_MODEL, :])
    o_ref[...] = x1 + _rms(f, g3_ref[...])


def _outffn(x2, ym, yr, yt, ys, wo, g1, g2, g3, w1, w2, bsz, seq):
    tm = TM_PROJ
    nt = seq // tm
    row = lambda b, i: (b * nt + i, 0)
    const = lambda b, i: (0, 0)
    wspec = lambda w: pl.BlockSpec(w.shape, const, pipeline_mode=pl.Buffered(1))
    return pl.pallas_call(
        _outffn_kernel,
        grid=(bsz, nt),
        in_specs=[pl.BlockSpec((tm, D_MODEL), row),
                  pl.BlockSpec((tm, GW), row), pl.BlockSpec((tm, GW), row), pl.BlockSpec((tm, GW), row),
                  pl.BlockSpec((tm, GW), row),
                  wspec(wo), pl.BlockSpec((1, D_MODEL), const), pl.BlockSpec((1, D_MODEL), const),
                  pl.BlockSpec((1, D_MODEL), const), wspec(w1), wspec(w2)],
        out_specs=pl.BlockSpec((tm, D_MODEL), row),
        out_shape=jax.ShapeDtypeStruct((bsz * seq, D_MODEL), F32),
        compiler_params=_cparams(("parallel", "parallel")),
        name="outffn",
    )(x2, ym, yr, yt, ys, wo, g1, g2, g3, w1, w2)


def _rope_tables(seq):
    half = DH // 2
    inv = ROPE_BASE ** (-jnp.arange(half, dtype=F32) / half)
    ang = jnp.arange(seq, dtype=F32)[:, None] * inv[None, :]
    tile = lambda t: jnp.tile(jnp.concatenate([t, t], axis=-1), (1, NH))
    src = lax.broadcasted_iota(jnp.int32, (GW, GW), 0)
    dst = lax.broadcasted_iota(jnp.int32, (GW, GW), 1)
    first = (dst & (DH - 1)) < half
    rot = jnp.where(first & (src == dst + half), -1.0, jnp.where(jnp.logical_not(first) & (src == dst - half), 1.0, 0.0))
    return tile(jnp.cos(ang)), tile(jnp.sin(ang)), rot.astype(BF16)


def _retention_tables():
    L = M_CHUNK
    log_gamma = jnp.log(1.0 - 2.0 ** (-5.0 - jnp.arange(NH, dtype=F32)))
    idx = jnp.arange(L, dtype=F32)
    diff = idx[:, None] - idx[None, :]
    causal = diff >= 0
    decay_mat = jnp.where(causal, jnp.exp(jnp.where(causal, diff, 0.0) * log_gamma[:, None, None]), 0.0)
    decay_t = jnp.swapaxes(decay_mat, 1, 2)
    q_decay = jnp.exp((idx + 1.0) * log_gamma[:, None])
    k_decay = jnp.exp((L - 1.0 - idx) * log_gamma[:, None])
    chunk_decay = jnp.broadcast_to(jnp.exp(L * log_gamma)[:, None, None], (NH, 1, DH))
    return decay_t, q_decay, k_decay, chunk_decay


def _s5_tables(lam_re, lam_im, log_dt, b_re, b_im, c_re, c_im):
    G, P, C = S5_GROUPS, S5_STATE, S5_GROUP
    dt = jnp.exp(log_dt)[:, None]
    mag = jnp.exp(lam_re * dt)
    abar_re, abar_im = mag * jnp.cos(lam_im * dt), mag * jnp.sin(lam_im * dt)
    den = lam_re * lam_re + lam_im * lam_im
    num_re, num_im = abar_re - 1.0, abar_im
    f_re = (num_re * lam_re + num_im * lam_im) / den
    f_im = (num_im * lam_re - num_re * lam_im) / den
    bbar_re = f_re[..., None] * b_re - f_im[..., None] * b_im
    bbar_im = f_re[..., None] * b_im + f_im[..., None] * b_re
    eye = jnp.eye(G, dtype=F32)
    bd_in = lambda t: jnp.einsum('gpc,gh->gchp', t, eye).reshape(G * C, G * P)
    bd_out = lambda t: jnp.einsum('gcp,gh->gphc', t, eye).reshape(G * P, G * C)
    bmat = jnp.concatenate([bd_in(bbar_re), bd_in(bbar_im)], axis=1)
    cmat = jnp.concatenate([bd_out(c_re), -bd_out(c_im)], axis=0)
    return abar_re.reshape(1, G * P), abar_im.reshape(1, G * P), bmat.astype(BF16), cmat.astype(BF16)


def _pad_lanes(t, width=LANES):
    return jnp.pad(t, ((0, 0), (0, width - t.shape[-1])))


def kernel(x, norm_mix_pre, norm_mix_post, w_in, m_conv_w, m_conv_b, m_i_bias, m_f_bias, m_norm, r_mu, r_w0, r_w_up, r_a0, r_a_up, r_g_up, r_k_k, r_k_a, r_r_k, r_ln_g, r_ln_b, t_gn_g, t_gn_b, s_lam_re, s_lam_im, s_log_dt, s_b_re, s_b_im, s_c_re, s_c_im, s_d, s_w_glu, s_b_glu, w_out, norm_ffn_pre, norm_ffn_post, w_ff1, w_ff2):
    bsz, seq, _ = x.shape
    depth = w_in.shape[0]
    assert seq % TM_IN == 0 and seq % TM_PROJ == 0 and seq % M_BLOCK == 0 and seq % R_BLOCK == 0 and seq % S5_T == 0
    assert bsz % SUBLANES == 0
    row = lambda t: t.reshape(1, -1)
    m_cols = 4 * GW + 2 * NH
    o_r = m_cols
    o_t = o_r + 4 * GW
    o_s = o_t + 4 * GW

    cos, sin, rot = _rope_tables(seq)
    dm, qd, kd, cd = _retention_tables()

    x2 = x.reshape(bsz * seq, D_MODEL)
    for l in range(depth):
        wl = w_in[l]
        wm = wl[:, 0:4 * GW].astype(BF16)
        wg = _pad_lanes(wl[:, 4 * GW:m_cols]).astype(BF16)
        wr = wl[:, o_r:o_t].astype(BF16)
        wt = wl[:, o_t:o_s].astype(BF16)
        ws = wl[:, o_s:].astype(BF16)
        pm, pg, pr, pt, ps = _inproj(x2, row(norm_mix_pre[l]), wm, wg, wr, wt, ws, bsz, seq)

        gate_bias = _pad_lanes(jnp.concatenate([m_i_bias[l], m_f_bias[l]])[None, :])
        y_m, y_t = _mlstm_ret(pm, pg, m_conv_w[l], row(m_conv_b[l]), gate_bias, row(m_norm[l]),
                              pt, cos, sin, rot, dm, qd, kd, cd, row(t_gn_g[l]), row(t_gn_b[l]), bsz, seq)

        zeros = jnp.zeros((DH, GW), F32)
        w_comb = jnp.concatenate([jnp.concatenate([r_w_up[l], zeros], axis=1),
                                  jnp.concatenate([zeros, r_a_up[l]], axis=1)], axis=0)
        y_r = _rwkv(pr, row(r_mu[l]), row(r_w0[l]), row(r_a0[l]), w_comb, r_g_up[l], row(r_k_k[l]),
                    row(r_k_a[l]), row(r_r_k[l]), row(r_ln_g[l]), row(r_ln_b[l]), bsz, seq)

        ar, ai, bmat, cmat = _s5_tables(s_lam_re[l], s_lam_im[l], s_log_dt[l], s_b_re[l], s_b_im[l],
                                        s_c_re[l], s_c_im[l])
        y_s = _s5(ps, ar, ai, bmat, cmat, row(s_d[l]), s_w_glu[l].astype(BF16), row(s_b_glu[l]), bsz, seq)

        x2 = _outffn(x2, y_m, y_r, y_t, y_s, w_out[l].astype(BF16),
                     row(norm_mix_post[l]), row(norm_ffn_pre[l]), row(norm_ffn_post[l]),
                     w_ff1[l].astype(BF16), w_ff2[l].astype(BF16), bsz, seq)
    return x2.reshape(bsz, seq, D_MODEL)
```
